```python
import jax, jax.numpy as jnp
from jax import lax
import numpy as np

D_MODEL = 2048
BATCH = 2
SEQ = 16384
DEPTH = 2

N_A_LAYERS = DEPTH // 2
N_B_LAYERS = DEPTH - N_A_LAYERS

MLA_HEADS = 16
Q_LORA = 512
KV_LORA = 512
QK_NOPE = 128
QK_ROPE = 64
QK_HEAD = QK_NOPE + QK_ROPE
V_HEAD = 128
MLA_WIDTH = MLA_HEADS * V_HEAD
A_IN_COLS = Q_LORA + KV_LORA + QK_ROPE + MLA_WIDTH
ROPE_THETA = 10000.0
Q_BLOCK = 128

DIL_GROUPS = ((128, 1), (512, 4), (2048, 16))
N_GROUPS = 3
DIL_HEADS = 16
DIL_HEAD_DIM = 128
DIL_WIDTH = DIL_HEADS * DIL_HEAD_DIM
B_Q_COLS = N_GROUPS * DIL_WIDTH
B_IN_COLS = B_Q_COLS + DIL_WIDTH
KV_SHARED_COLS = 2 * N_GROUPS * DIL_WIDTH
LOOKBACK = 128
BAND_BLK = 128
ALIBI_MAX = 8.0

EPS = 1e-6
NEG_INF = -1e30

kernel_name = "yoco_mla_dilated_alibi_hybrid"


def rms_norm(x, g):
    xf = x.astype(jnp.float32)
    y = xf * lax.rsqrt(jnp.mean(xf * xf, axis=-1, keepdims=True) + EPS)
    return y.astype(x.dtype) * g


def rope(x, pos):
    half = QK_ROPE // 2
    inv = ROPE_THETA ** (-jnp.arange(half, dtype=jnp.float32) / half)
    ang = pos.astype(jnp.float32)[:, None] * inv[None, :]
    cos = jnp.cos(ang)[None, :, None, :]
    sin = jnp.sin(ang)[None, :, None, :]
    x1 = x[..., :half].astype(jnp.float32)
    x2 = x[..., half:].astype(jnp.float32)
    out = jnp.concatenate([x1 * cos - x2 * sin, x2 * cos + x1 * sin], axis=-1)
    return out.astype(x.dtype)


def causal_dense_attention(q, k, v):
    B, S, H, E = q.shape
    nb = S // Q_BLOCK
    scale = E ** -0.5
    kpos = jnp.arange(S)

    def one_block(i):
        start = i * Q_BLOCK
        qb = lax.dynamic_slice_in_dim(q, start, Q_BLOCK, axis=1)
        s = jnp.einsum('bqhe,bkhe->bhqk', qb, k, preferred_element_type=jnp.float32) * scale
        qpos = start + jnp.arange(Q_BLOCK)
        s = jnp.where(kpos[None, :] <= qpos[:, None], s, NEG_INF)
        p = jax.nn.softmax(s, axis=-1)
        return jnp.einsum('bhqk,bkhd->bqhd', p.astype(v.dtype), v)

    o = lax.map(one_block, jnp.arange(nb))
    return jnp.moveaxis(o, 0, 1).reshape(B, S, H, v.shape[-1]).astype(q.dtype)


def dilated_group_attention(q, k, v, dilation, slopes):
    B, S, H, E = q.shape
    L = S // dilation
    nb = -(-L // BAND_BLK)
    Lp = nb * BAND_BLK

    def strided(a, front):
        a = a.reshape(B, L, dilation, H, E).transpose(0, 2, 1, 3, 4)
        return jnp.pad(a, ((0, 0), (0, 0), (front, Lp - L), (0, 0), (0, 0)))

    qs = strided(q, 0).reshape(B, dilation, nb, BAND_BLK, H, E)
    kp = strided(k, BAND_BLK)
    vp = strided(v, BAND_BLK)

    def band(a):
        prev = a[:, :, :Lp].reshape(B, dilation, nb, BAND_BLK, H, E)
        cur = a[:, :, BAND_BLK:].reshape(B, dilation, nb, BAND_BLK, H, E)
        return jnp.concatenate([prev, cur], axis=3)

    kb, vb = band(kp), band(vp)
    s = jnp.einsum('brnqhe,brnkhe->brnhqk', qs, kb, preferred_element_type=jnp.float32)
    rel = jnp.arange(BAND_BLK)[:, None] + BAND_BLK - jnp.arange(2 * BAND_BLK)[None, :]
    key_n = (jnp.arange(nb)[:, None] - 1) * BAND_BLK + jnp.arange(2 * BAND_BLK)[None, :]
    valid = ((rel >= 0) & (rel <= LOOKBACK))[None] & (key_n >= 0)[:, None, :]
    alibi = -slopes[:, None, None] * (dilation * rel).astype(jnp.float32)[None]
    s = jnp.where(valid[None, None, :, None], s * (E ** -0.5) + alibi[None, None, None], NEG_INF)
    m = jnp.max(s, axis=-1, keepdims=True)
    p = jnp.exp(s - m)
    den = jnp.sum(p, axis=-1)
    o = jnp.einsum('brnhqk,brnkhe->brnqhe', p.astype(v.dtype), vb, preferred_element_type=jnp.float32)
    o = o / jnp.moveaxis(den, 3, 4)[..., None]
    lse = jnp.moveaxis(m[..., 0] + jnp.log(den), 3, 4)

    def unstrided(a):
        a = a.reshape((B, dilation, Lp) + a.shape[4:])[:, :, :L]
        return jnp.swapaxes(a, 1, 2).reshape((B, S) + a.shape[3:])

    return unstrided(o), unstrided(lse)


def mla_layer(h, ln, w_in, ln_q, w_q_up, ln_kv, w_kv_up, g_q, g_k, w_o):
    B, S, _ = h.shape
    u = rms_norm(h, ln) @ w_in
    c_q = u[..., :Q_LORA]
    c_kv = u[..., Q_LORA:Q_LORA + KV_LORA]
    k_pe = u[..., Q_LORA + KV_LORA:Q_LORA + KV_LORA + QK_ROPE]
    gate = u[..., Q_LORA + KV_LORA + QK_ROPE:]
    q = (rms_norm(c_q, ln_q) @ w_q_up).reshape(B, S, MLA_HEADS, QK_HEAD)
    kv = (rms_norm(c_kv, ln_kv) @ w_kv_up).reshape(B, S, MLA_HEADS, QK_NOPE + V_HEAD)
    v = kv[..., QK_NOPE:]
    k = jnp.concatenate([kv[..., :QK_NOPE],
                         jnp.broadcast_to(k_pe[:, :, None, :], (B, S, MLA_HEADS, QK_ROPE))], axis=-1)
    q = rms_norm(q, g_q)
    k = rms_norm(k, g_k)
    pos = jnp.arange(S)
    q = jnp.concatenate([q[..., :QK_NOPE], rope(q[..., QK_NOPE:], pos)], axis=-1)
    k = jnp.concatenate([k[..., :QK_NOPE], rope(k[..., QK_NOPE:], pos)], axis=-1)
    o = causal_dense_attention(q, k, v).reshape(B, S, MLA_WIDTH)
    return h + (o * jax.nn.silu(gate)) @ w_o


def shared_kv(h, ln, w_kv, g_k):
    B, S, _ = h.shape
    u = (rms_norm(h, ln) @ w_kv).reshape(B, S, 2, N_GROUPS, DIL_HEADS, DIL_HEAD_DIM)
    k = rms_norm(u[:, :, 0], g_k[:, None, :])
    v = u[:, :, 1]
    return k, v


def dilated_layer(h, k_sh, v_sh, ln, w_in, g_q, w_o):
    B, S, _ = h.shape
    u = rms_norm(h, ln) @ w_in
    q_all = u[..., :B_Q_COLS].reshape(B, S, N_GROUPS, DIL_HEADS, DIL_HEAD_DIM)
    gate = u[..., B_Q_COLS:]
    slopes = 2.0 ** (-ALIBI_MAX * jnp.arange(1, DIL_HEADS + 1, dtype=jnp.float32) / DIL_HEADS)
    outs, lses = [], []
    for g, (_, dil) in enumerate(DIL_GROUPS):
        q = rms_norm(q_all[:, :, g], g_q[g])
        o_g, lse_g = dilated_group_attention(q, k_sh[:, :, g], v_sh[:, :, g], dil, slopes)
        outs.append(o_g)
        lses.append(lse_g)
    wts = jax.nn.softmax(jnp.stack(lses, axis=-1), axis=-1)
    o = jnp.einsum('bshge,bshg->bshe', jnp.stack(outs, axis=3), wts)
    o = o.astype(h.dtype).reshape(B, S, DIL_WIDTH)
    return h + (o * jax.nn.silu(gate)) @ w_o


def setup_inputs(seed: int = 0) -> dict:
    key = jax.random.key(seed)
    ks = jax.random.split(key, 20)
    f32 = jnp.float32

    def w(k, shape, fan_in):
        return jax.random.normal(k, shape, f32) * (fan_in ** -0.5)

    def gain(k, shape):
        return 1.0 + 0.02 * jax.random.normal(k, shape, f32)

    nA, nB = N_A_LAYERS, N_B_LAYERS
    return {
        "x": jax.random.normal(ks[0], (BATCH, SEQ, D_MODEL), f32),
        "a_ln": gain(ks[1], (nA, D_MODEL)),
        "a_w_in": w(ks[2], (nA, D_MODEL, A_IN_COLS), D_MODEL),
        "a_ln_q": gain(ks[3], (nA, Q_LORA)),
        "a_w_q_up": w(ks[4], (nA, Q_LORA, MLA_HEADS * QK_HEAD), Q_LORA),
        "a_ln_kv": gain(ks[5], (nA, KV_LORA)),
        "a_w_kv_up": w(ks[6], (nA, KV_LORA, MLA_HEADS * (QK_NOPE + V_HEAD)), KV_LORA),
        "a_q_norm": gain(ks[7], (nA, QK_HEAD)),
        "a_k_norm": gain(ks[8], (nA, QK_HEAD)),
        "a_w_o": w(ks[9], (nA, MLA_WIDTH, D_MODEL), MLA_WIDTH),
        "kv_ln": gain(ks[10], (D_MODEL,)),
        "kv_w": w(ks[11], (D_MODEL, KV_SHARED_COLS), D_MODEL),
        "kv_k_norm": gain(ks[12], (N_GROUPS, DIL_HEAD_DIM)),
        "b_ln": gain(ks[13], (nB, D_MODEL)),
        "b_w_in": w(ks[14], (nB, D_MODEL, B_IN_COLS), D_MODEL),
        "b_q_norm": gain(ks[15], (nB, N_GROUPS, DIL_HEAD_DIM)),
        "b_w_o": w(ks[16], (nB, DIL_WIDTH, D_MODEL), DIL_WIDTH),
    }


def reference(x, a_ln, a_w_in, a_ln_q, a_w_q_up, a_ln_kv, a_w_kv_up, a_q_norm, a_k_norm, a_w_o,
              kv_ln, kv_w, kv_k_norm, b_ln, b_w_in, b_q_norm, b_w_o):
    h = x
    k_sh, v_sh = None, None
    for layer in range(DEPTH):
        if layer < N_A_LAYERS:
            i = layer
            h = mla_layer(h, a_ln[i], a_w_in[i], a_ln_q[i], a_w_q_up[i], a_ln_kv[i], a_w_kv_up[i],
                          a_q_norm[i], a_k_norm[i], a_w_o[i])
        else:
            if layer == N_A_LAYERS:
                k_sh, v_sh = shared_kv(h, kv_ln, kv_w, kv_k_norm)
            j = layer - N_A_LAYERS
            h = dilated_layer(h, k_sh, v_sh, b_ln[j], b_w_in[j], b_q_norm[j], b_w_o[j])
    return h
```

```python
import functools
import math

import jax
import jax.numpy as jnp
from jax import lax
from jax.experimental import pallas as pl
from jax.experimental.pallas import tpu as pltpu

F32 = jnp.float32
BF16 = jnp.bfloat16

V7X_LANES = 128
V7X_VMEM_BYTES = 64 * 1024 * 1024
VMEM_LIMIT_BYTES = V7X_VMEM_BYTES * 7 // 8

D_MODEL = 2048
MLA_HEADS = 16
Q_LORA = 512
KV_LORA = 512
QK_NOPE = 128
QK_ROPE = 64
QK_HEAD = QK_NOPE + QK_ROPE
V_HEAD = 128
MLA_WIDTH = MLA_HEADS * V_HEAD
ROPE_THETA = 10000.0
DIL_GROUPS = ((128, 1), (512, 4), (2048, 16))
N_GROUPS = 3
DIL_HEADS = 16
DIL_HEAD_DIM = 128
DIL_WIDTH = DIL_HEADS * DIL_HEAD_DIM
LOOKBACK = 128
BAND_BLK = 128
ALIBI_MAX = 8.0
EPS = 1e-6
NEG_BIG = -1e30
LOG2E = math.log2(math.e)

Q_HEAD_PAD = 2 * V7X_LANES
A_IN_PAD = Q_LORA + KV_LORA + MLA_WIDTH + V7X_LANES


def _params(semantics):
    return pltpu.CompilerParams(dimension_semantics=semantics, vmem_limit_bytes=VMEM_LIMIT_BYTES)


def _inv_rms(x, n):
    return lax.rsqrt(jnp.sum(x * x, axis=-1, keepdims=True) * (1.0 / n) + EPS)


def _rope(rv, cos_t, s1_t, s2_t):
    return rv * cos_t + pltpu.roll(rv, 96, 1) * s1_t + pltpu.roll(rv, 32, 1) * s2_t


def _rope_table_kernel(inv_ref, cos_ref, s1_ref, s2_ref):
    ts = cos_ref.shape[0]
    row = lax.broadcasted_iota(jnp.int32, (ts, V7X_LANES), 0) + pl.program_id(0) * ts
    lane = lax.broadcasted_iota(jnp.int32, (ts, V7X_LANES), 1)
    ang = row.astype(F32) * inv_ref[...]
    c = jnp.cos(ang)
    s = jnp.sin(ang)
    half = QK_ROPE // 2
    cos_ref[...] = jnp.where(lane < QK_ROPE, c, 0.0)
    s1_ref[...] = jnp.where(lane < half, -s, 0.0)
    s2_ref[...] = jnp.where((lane >= half) & (lane < QK_ROPE), s, 0.0)


def _rope_tables(seq):
    half = QK_ROPE // 2
    inv = ROPE_THETA ** (-jnp.arange(half, dtype=F32) / half)
    inv_lanes = jnp.concatenate([inv, inv, jnp.zeros((V7X_LANES - QK_ROPE,), F32)])[None, :]
    ts = min(seq, 1024)
    tab = jax.ShapeDtypeStruct((seq, V7X_LANES), F32)
    spec = pl.BlockSpec((ts, V7X_LANES), lambda i: (i, 0))
    return pl.pallas_call(
        _rope_table_kernel,
        grid=(seq // ts,),
        in_specs=[pl.BlockSpec((1, V7X_LANES), lambda i: (0, 0))],
        out_specs=[spec, spec, spec],
        out_shape=[tab, tab, tab],
        name="rope_tables",
        compiler_params=_params(("arbitrary",)),
    )(inv_lanes)


def _a_in_kernel(x_ref, ln_ref, w_ref, lnq_ref, lnkv_ref, cq_ref, ckv_ref, sg_ref, kpe_ref):
    x = x_ref[...]
    xn = ((x * _inv_rms(x, D_MODEL)) * ln_ref[...]).astype(BF16)
    u = jnp.dot(xn, w_ref[...], preferred_element_type=F32)
    cq = u[:, :Q_LORA]
    cq_ref[...] = ((cq * _inv_rms(cq, Q_LORA)) * lnq_ref[...]).astype(BF16)
    ckv = u[:, Q_LORA:Q_LORA + KV_LORA]
    ckv_ref[...] = ((ckv * _inv_rms(ckv, KV_LORA)) * lnkv_ref[...]).astype(BF16)
    g0 = Q_LORA + KV_LORA
    gate = u[:, g0:g0 + MLA_WIDTH]
    sg_ref[...] = (gate * jax.nn.sigmoid(gate)).astype(BF16)
    kpe_ref[...] = u[:, g0 + MLA_WIDTH:]


def _a_in(x2, ln, w_pad, ln_q, ln_kv, tm):
    m = x2.shape[0]
    row = lambda n: pl.BlockSpec((tm, n), lambda i: (i, 0))
    full = lambda a: pl.BlockSpec(a.shape, lambda i: (0, 0))
    return pl.pallas_call(
        _a_in_kernel,
        grid=(m // tm,),
        in_specs=[row(D_MODEL), full(ln), full(w_pad), full(ln_q), full(ln_kv)],
        out_specs=[row(Q_LORA), row(KV_LORA), row(MLA_WIDTH), row(V7X_LANES)],
        out_shape=[jax.ShapeDtypeStruct((m, Q_LORA), BF16), jax.ShapeDtypeStruct((m, KV_LORA), BF16),
                   jax.ShapeDtypeStruct((m, MLA_WIDTH), BF16), jax.ShapeDtypeStruct((m, V7X_LANES), F32)],
        name="a_in_proj",
        compiler_params=_params(("arbitrary",)),
    )(x2, ln, w_pad, ln_q, ln_kv)


def _q_up_kernel(cq_ref, w_ref, g_ref, cos_ref, s1_ref, s2_ref, q_ref, *, scale):
    y = jnp.dot(cq_ref[...], w_ref[...], preferred_element_type=F32)
    g_n = g_ref[:, :QK_NOPE]
    g_r = g_ref[:, QK_NOPE:]
    cos_t, s1_t, s2_t = cos_ref[...], s1_ref[...], s2_ref[...]
    for h in range(MLA_HEADS):
        nope = y[:, h * Q_HEAD_PAD:h * Q_HEAD_PAD + QK_NOPE]
        rv = y[:, h * Q_HEAD_PAD + QK_NOPE:(h + 1) * Q_HEAD_PAD]
        ss = jnp.sum(nope * nope, axis=-1, keepdims=True) + jnp.sum(rv * rv, axis=-1, keepdims=True)
        r = lax.rsqrt(ss * (1.0 / QK_HEAD) + EPS)
        q_ref[h, :, :QK_NOPE] = (((nope * r) * g_n) * scale).astype(BF16)
        qr = _rope((rv * r) * g_r, cos_t, s1_t, s2_t) * scale
        q_ref[h, :, QK_NOPE:] = qr[:, :QK_ROPE].astype(BF16)


def _kv_up_kernel(ckv_ref, kpe_ref, w_ref, g_ref, cos_ref, s1_ref, s2_ref, k_ref, v_ref):
    y = jnp.dot(ckv_ref[...], w_ref[...], preferred_element_type=F32)
    g_n = g_ref[:, :QK_NOPE]
    g_r = g_ref[:, QK_NOPE:]
    kpe = kpe_ref[...]
    ss_pe = jnp.sum(kpe * kpe, axis=-1, keepdims=True)
    cos_t, s1_t, s2_t = cos_ref[...], s1_ref[...], s2_ref[...]
    hw = QK_NOPE + V_HEAD
    for h in range(MLA_HEADS):
        kn = y[:, h * hw:h * hw + QK_NOPE]
        r = lax.rsqrt((jnp.sum(kn * kn, axis=-1, keepdims=True) + ss_pe) * (1.0 / QK_HEAD) + EPS)
        k_ref[h, :, :QK_NOPE] = ((kn * r) * g_n).astype(BF16)
        kr = _rope((kpe * r) * g_r, cos_t, s1_t, s2_t)
        k_ref[h, :, QK_NOPE:] = kr[:, :QK_ROPE].astype(BF16)
        v_ref[h] = y[:, h * hw + QK_NOPE:(h + 1) * hw].astype(BF16)


def _seq_spec(tm, n):
    return pl.BlockSpec((None, tm, n), lambda b, i: (b, i, 0))


def _tab_spec(tm):
    return pl.BlockSpec((tm, V7X_LANES), lambda b, i: (i, 0))


def _head_spec(tm, n):
    return pl.BlockSpec((None, MLA_HEADS, tm, n), lambda b, i: (b, 0, i, 0))


def _q_up(cq3, w_pad, g_pad, tabs, tm):
    b, s, _ = cq3.shape
    full = lambda a: pl.BlockSpec(a.shape, lambda bb, i: (0, 0))
    scale = QK_HEAD ** -0.5 * LOG2E
    return pl.pallas_call(
        functools.partial(_q_up_kernel, scale=scale),
        grid=(b, s // tm),
        in_specs=[_seq_spec(tm, Q_LORA), full(w_pad), full(g_pad), _tab_spec(tm), _tab_spec(tm), _tab_spec(tm)],
        out_specs=_head_spec(tm, QK_HEAD),
        out_shape=jax.ShapeDtypeStruct((b, MLA_HEADS, s, QK_HEAD), BF16),
        name="a_q_up",
        compiler_params=_params(("arbitrary", "arbitrary")),
    )(cq3, w_pad, g_pad, *tabs)


def _kv_up(ckv3, kpe3, w, g_pad, tabs, tm):
    b, s, _ = ckv3.shape
    full = lambda a: pl.BlockSpec(a.shape, lambda bb, i: (0, 0))
    return pl.pallas_call(
        _kv_up_kernel,
        grid=(b, s // tm),
        in_specs=[_seq_spec(tm, KV_LORA), _seq_spec(tm, V7X_LANES), full(w), full(g_pad),
                  _tab_spec(tm), _tab_spec(tm), _tab_spec(tm)],
        out_specs=[_head_spec(tm, QK_HEAD), _head_spec(tm, V_HEAD)],
        out_shape=[jax.ShapeDtypeStruct((b, MLA_HEADS, s, QK_HEAD), BF16),
                   jax.ShapeDtypeStruct((b, MLA_HEADS, s, V_HEAD), BF16)],
        name="a_kv_up",
        compiler_params=_params(("arbitrary", "arbitrary")),
    )(ckv3, kpe3, w, g_pad, *tabs)


def _mla_attn_kernel(q_ref, k_ref, v_ref, o_ref, m_sc, l_sc, acc_sc, *, blk):
    qi = pl.program_id(2)
    q = q_ref[...]
    m_sc[...] = jnp.full(m_sc.shape, NEG_BIG, F32)
    l_sc[...] = jnp.zeros(l_sc.shape, F32)
    acc_sc[...] = jnp.zeros(acc_sc.shape, F32)

    def chunk(kc, diagonal):
        start = pl.multiple_of(kc * blk, blk)
        k = k_ref[pl.ds(start, blk), :]
        v = v_ref[pl.ds(start, blk), :]
        s = lax.dot_general(q, k, (((1,), (1,)), ((), ())), preferred_element_type=F32)
        if diagonal:
            row = lax.broadcasted_iota(jnp.int32, (blk, blk), 0)
            col = lax.broadcasted_iota(jnp.int32, (blk, blk), 1)
            s = jnp.where(col <= row, s, NEG_BIG)
        m_old = m_sc[...]
        m_new = jnp.maximum(m_old, jnp.max(s, axis=-1, keepdims=True))
        alpha = jnp.exp2(m_old - m_new)
        p = jnp.exp2(s - m_new)
        l_sc[...] = alpha * l_sc[...] + jnp.sum(p, axis=-1, keepdims=True)
        acc_sc[...] = alpha * acc_sc[...] + jnp.dot(p.astype(BF16), v, preferred_element_type=F32)
        m_sc[...] = m_new

    def body(kc, carry):
        chunk(kc, False)
        return carry

    lax.fori_loop(0, qi, body, 0)
    chunk(qi, True)
    o_ref[...] = (acc_sc[...] / l_sc[...]).astype(BF16)


def _mla_attention(q, k, v, blk):
    b, h, s, _ = q.shape
    return pl.pallas_call(
        functools.partial(_mla_attn_kernel, blk=blk),
        grid=(b, h, s // blk),
        in_specs=[pl.BlockSpec((None, None, blk, QK_HEAD), lambda bb, hh, i: (bb, hh, i, 0)),
                  pl.BlockSpec((None, None, s, QK_HEAD), lambda bb, hh, i: (bb, hh, 0, 0)),
                  pl.BlockSpec((None, None, s, V_HEAD), lambda bb, hh, i: (bb, hh, 0, 0))],
        out_specs=pl.BlockSpec((None, blk, V_HEAD), lambda bb, hh, i: (bb, i, hh)),
        out_shape=jax.ShapeDtypeStruct((b, s, MLA_WIDTH), BF16),
        scratch_shapes=[pltpu.VMEM((blk, 1), F32), pltpu.VMEM((blk, 1), F32), pltpu.VMEM((blk, V_HEAD), F32)],
        name="a_mla_attention",
        compiler_params=_params(("arbitrary", "arbitrary", "arbitrary")),
    )(q, k, v)


def _a_out_kernel(o_ref, sg_ref, x_ref, w_ref, lnkv_ref, lnb_ref, h_ref, xkv_ref, xb_ref):
    a = (o_ref[...].astype(F32) * sg_ref[...].astype(F32)).astype(BF16)
    h = x_ref[...] + jnp.dot(a, w_ref[...], preferred_element_type=F32)
    h_ref[...] = h
    hn = h * _inv_rms(h, D_MODEL)
    xkv_ref[...] = (hn * lnkv_ref[...]).astype(BF16)
    xb_ref[...] = (hn * lnb_ref[...]).astype(BF16)


def _a_out(o2, sg, x2, w, ln_kv, ln_b, tm):
    m = x2.shape[0]
    row = pl.BlockSpec((tm, D_MODEL), lambda i: (i, 0))
    full = lambda a: pl.BlockSpec(a.shape, lambda i: (0, 0))
    return pl.pallas_call(
        _a_out_kernel,
        grid=(m // tm,),
        in_specs=[row, row, row, full(w), full(ln_kv), full(ln_b)],
        out_specs=[row, row, row],
        out_shape=[jax.ShapeDtypeStruct((m, D_MODEL), F32), jax.ShapeDtypeStruct((m, D_MODEL), BF16),
                   jax.ShapeDtypeStruct((m, D_MODEL), BF16)],
        name="a_out_proj",
        compiler_params=_params(("arbitrary",)),
    )(o2, sg, x2, w, ln_kv, ln_b)


def _head_norm_proj_kernel(x_ref, w_ref, g_ref, o_ref, *, n_norm_tiles, tail_silu, scale):
    j = pl.program_id(1)
    y = jnp.dot(x_ref[...], w_ref[...], preferred_element_type=F32)
    tn = y.shape[1]

    @pl.when(j < n_norm_tiles)
    def _():
        for c in range(tn // DIL_HEAD_DIM):
            sl = slice(c * DIL_HEAD_DIM, (c + 1) * DIL_HEAD_DIM)
            yh = y[:, sl]
            out = (yh * _inv_rms(yh, DIL_HEAD_DIM)) * g_ref[:, sl]
            if scale != 1.0:
                out = out * scale
            o_ref[:, sl] = out.astype(BF16)

    @pl.when(j >= n_norm_tiles)
    def _():
        if tail_silu:
            o_ref[...] = (y * jax.nn.sigmoid(y)).astype(BF16)
        else:
            o_ref[...] = y.astype(BF16)


def _head_norm_proj(xn, w, g_cols, n_norm_cols, tail_silu, scale, tm, tn):
    m, kdim = xn.shape
    n = w.shape[1]
    return pl.pallas_call(
        functools.partial(_head_norm_proj_kernel, n_norm_tiles=n_norm_cols // tn, tail_silu=tail_silu, scale=scale),
        grid=(m // tm, n // tn),
        in_specs=[pl.BlockSpec((tm, kdim), lambda i, j: (i, 0)),
                  pl.BlockSpec((kdim, tn), lambda i, j: (0, j)),
                  pl.BlockSpec((1, tn), lambda i, j: (0, j))],
        out_specs=pl.BlockSpec((tm, tn), lambda i, j: (i, j)),
        out_shape=jax.ShapeDtypeStruct((m, n), BF16),
        name="b_wide_proj",
        compiler_params=_params(("arbitrary", "arbitrary")),
    )(xn, w, g_cols)


def _dilated_attn_kernel(q_ref, kp_ref, kc_ref, vp_ref, vc_ref, o_ref, lse_ref, kbuf, vbuf, *, dilation, tq):
    i = pl.program_id(2)
    kbuf[:BAND_BLK, :] = kp_ref[...]
    kbuf[BAND_BLK:, :] = kc_ref[...]
    vbuf[:BAND_BLK, :] = vp_ref[...]
    vbuf[BAND_BLK:, :] = vc_ref[...]
    row = lax.broadcasted_iota(jnp.int32, (BAND_BLK, 2 * BAND_BLK), 0)
    col = lax.broadcasted_iota(jnp.int32, (BAND_BLK, 2 * BAND_BLK), 1)
    rel = row + BAND_BLK - col
    band = (rel >= 0) & (rel <= LOOKBACK)
    first_band = band & (col >= jnp.where(i > 0, 0, BAND_BLK))
    dist = (-float(dilation) * LOG2E) * rel.astype(F32)
    lane = lax.broadcasted_iota(jnp.int32, (BAND_BLK, V7X_LANES), 1)
    for a in range(tq // BAND_BLK):
        valid = first_band if a == 0 else band
        lse_tile = jnp.zeros((BAND_BLK, V7X_LANES), F32)
        for h in range(DIL_HEADS):
            hs = slice(h * DIL_HEAD_DIM, (h + 1) * DIL_HEAD_DIM)
            slope = 2.0 ** (-ALIBI_MAX * (h + 1) / DIL_HEADS)
            q = q_ref[a * BAND_BLK:(a + 1) * BAND_BLK, hs]
            k = kbuf[a * BAND_BLK:(a + 2) * BAND_BLK, hs]
            v = vbuf[a * BAND_BLK:(a + 2) * BAND_BLK, hs]
            s = lax.dot_general(q, k, (((1,), (1,)), ((), ())), preferred_element_type=F32)
            s = jnp.where(valid, s + slope * dist, NEG_BIG)
            m = jnp.max(s, axis=-1, keepdims=True)
            p = jnp.exp2(s - m)
            den = jnp.sum(p, axis=-1, keepdims=True)
            o = jnp.dot(p.astype(BF16), v, preferred_element_type=F32) / den
            o_ref[a * BAND_BLK:(a + 1) * BAND_BLK, hs] = o.astype(BF16)
            lse_tile = jnp.where(lane == h, m + jnp.log2(den), lse_tile)
        lse_ref[a * BAND_BLK:(a + 1) * BAND_BLK, :] = lse_tile


def _dilated_attention(ub, kvs, batch, seq, group, tq):
    dilation = DIL_GROUPS[group][1]
    ls = seq // dilation
    q_cols = ub.shape[1] // DIL_WIDTH
    kv_cols = kvs.shape[1] // DIL_WIDTH
    qv = ub.reshape(batch, ls, dilation * ub.shape[1])
    kv = kvs.reshape(batch, ls, dilation * kvs.shape[1])
    sub = tq // BAND_BLK
    cur = lambda cb: pl.BlockSpec((None, tq, DIL_WIDTH), lambda b, r, i: (b, i, r * cb[0] + cb[1]))
    prev = lambda cb: pl.BlockSpec((None, BAND_BLK, DIL_WIDTH),
                                   lambda b, r, i: (b, jnp.maximum(i * sub - 1, 0), r * cb[0] + cb[1]))
    k_blk = (kv_cols, group)
    v_blk = (kv_cols, N_GROUPS + group)
    o, lse = pl.pallas_call(
        functools.partial(_dilated_attn_kernel, dilation=dilation, tq=tq),
        grid=(batch, dilation, ls // tq),
        in_specs=[cur((q_cols, group)), prev(k_blk), cur(k_blk), prev(v_blk), cur(v_blk)],
        out_specs=[pl.BlockSpec((None, tq, DIL_WIDTH), lambda b, r, i: (b, i, r)),
                   pl.BlockSpec((None, tq, V7X_LANES), lambda b, r, i: (b, i, r))],
        out_shape=[jax.ShapeDtypeStruct((batch, ls, dilation * DIL_WIDTH), BF16),
                   jax.ShapeDtypeStruct((batch, ls, dilation * V7X_LANES), F32)],
        scratch_shapes=[pltpu.VMEM((tq + BAND_BLK, DIL_WIDTH), BF16), pltpu.VMEM((tq + BAND_BLK, DIL_WIDTH), BF16)],
        name=f"b_dilated_attention_d{dilation}",
        compiler_params=_params(("arbitrary", "arbitrary", "arbitrary")),
    )(qv, kv, kv, kv, kv)
    return o.reshape(batch * seq, DIL_WIDTH), lse.reshape(batch * seq, V7X_LANES)


def _b_out_kernel(o0_ref, o1_ref, o2_ref, l0_ref, l1_ref, l2_ref, sg_ref, h_ref, w_ref, out_ref, a_sc):
    l0, l1, l2 = l0_ref[...], l1_ref[...], l2_ref[...]
    mx = jnp.maximum(jnp.maximum(l0, l1), l2)
    e0, e1, e2 = jnp.exp2(l0 - mx), jnp.exp2(l1 - mx), jnp.exp2(l2 - mx)
    inv = 1.0 / (e0 + e1 + e2)
    w0, w1, w2 = e0 * inv, e1 * inv, e2 * inv
    for h in range(DIL_HEADS):
        hs = slice(h * DIL_HEAD_DIM, (h + 1) * DIL_HEAD_DIM)
        o = (w0[:, h:h + 1] * o0_ref[:, hs].astype(F32) + w1[:, h:h + 1] * o1_ref[:, hs].astype(F32)
             + w2[:, h:h + 1] * o2_ref[:, hs].astype(F32))
        a_sc[:, hs] = (o * sg_ref[:, hs].astype(F32)).astype(BF16)
    out_ref[...] = h_ref[...] + jnp.dot(a_sc[...], w_ref[...], preferred_element_type=F32)


def _b_out(os_, lses, ub, h1, w, tm):
    m = h1.shape[0]
    row = pl.BlockSpec((tm, D_MODEL), lambda i: (i, 0))
    lrow = pl.BlockSpec((tm, V7X_LANES), lambda i: (i, 0))
    gate_block = ub.shape[1] // DIL_WIDTH - 1
    return pl.pallas_call(
        _b_out_kernel,
        grid=(m // tm,),
        in_specs=[row, row, row, lrow, lrow, lrow,
                  pl.BlockSpec((tm, DIL_WIDTH), lambda i: (i, gate_block)),
                  row, pl.BlockSpec(w.shape, lambda i: (0, 0))],
        out_specs=row,
        out_shape=jax.ShapeDtypeStruct((m, D_MODEL), F32),
        scratch_shapes=[pltpu.VMEM((tm, DIL_WIDTH), BF16)],
        name="b_merge_out_proj",
        compiler_params=_params(("arbitrary",)),
    )(*os_, *lses, ub, h1, w)


def _tile(n, want):
    t = min(n, want)
    assert n % t == 0, (n, t)
    return t


def kernel(x, a_ln, a_w_in, a_ln_q, a_w_q_up, a_ln_kv, a_w_kv_up, a_q_norm, a_k_norm, a_w_o,
           kv_ln, kv_w, kv_k_norm, b_ln, b_w_in, b_q_norm, b_w_o):
    batch, seq, d = x.shape
    assert d == D_MODEL and a_ln.shape[0] == 1 and b_ln.shape[0] == 1
    assert seq % (BAND_BLK * DIL_GROUPS[-1][1]) == 0
    m = batch * seq
    x2 = x.reshape(m, d)

    lat = Q_LORA + KV_LORA
    w_in = a_w_in[0]
    w_in_pad = jnp.concatenate(
        [w_in[:, :lat], w_in[:, lat + QK_ROPE:], w_in[:, lat:lat + QK_ROPE],
         jnp.zeros((d, V7X_LANES - QK_ROPE), F32)], axis=1).astype(BF16)
    w_q = a_w_q_up[0].reshape(Q_LORA, MLA_HEADS, QK_HEAD)
    w_q_pad = jnp.pad(w_q, ((0, 0), (0, 0), (0, Q_HEAD_PAD - QK_HEAD))).reshape(Q_LORA, MLA_HEADS * Q_HEAD_PAD)
    w_q_pad = w_q_pad.astype(BF16)
    w_kv = a_w_kv_up[0].astype(BF16)
    pad_gain = lambda g: jnp.pad(g, (0, Q_HEAD_PAD - QK_HEAD))[None, :]
    gq_pad, gk_pad = pad_gain(a_q_norm[0]), pad_gain(a_k_norm[0])
    w_o_a = a_w_o[0].astype(BF16)
    w_kvs = kv_w.astype(BF16)
    w_b_in = b_w_in[0].astype(BF16)
    w_o_b = b_w_o[0].astype(BF16)
    k_cols = N_GROUPS * DIL_WIDTH
    gk_cols = jnp.concatenate([jnp.tile(kv_k_norm, (1, DIL_HEADS)).reshape(1, k_cols),
                               jnp.ones((1, k_cols), F32)], axis=1)
    gq_cols = jnp.concatenate([jnp.tile(b_q_norm[0], (1, DIL_HEADS)).reshape(1, k_cols),
                               jnp.ones((1, DIL_WIDTH), F32)], axis=1)

    tabs = _rope_tables(seq)
    cq, ckv, sg_a, kpe = _a_in(x2, a_ln, w_in_pad, a_ln_q, a_ln_kv, _tile(m, 256))
    tm_up = _tile(seq, 256)
    q = _q_up(cq.reshape(batch, seq, Q_LORA), w_q_pad, gq_pad, tabs, tm_up)
    k, v = _kv_up(ckv.reshape(batch, seq, KV_LORA), kpe.reshape(batch, seq, V7X_LANES), w_kv, gk_pad, tabs, tm_up)
    o_a = _mla_attention(q, k, v, _tile(seq, 512))
    h1, xn_kv, xn_b = _a_out(o_a.reshape(m, MLA_WIDTH), sg_a, x2, w_o_a, kv_ln[None, :], b_ln, _tile(m, 256))

    tm_w, tn_w = _tile(m, 1024), 1024
    kvs = _head_norm_proj(xn_kv, w_kvs, gk_cols, k_cols, False, 1.0, tm_w, tn_w)
    ub = _head_norm_proj(xn_b, w_b_in, gq_cols, k_cols, True, DIL_HEAD_DIM ** -0.5 * LOG2E, tm_w, tn_w)
    outs, lses = [], []
    for g, (_, dil) in enumerate(DIL_GROUPS):
        o_g, lse_g = _dilated_attention(ub, kvs, batch, seq, g, _tile(seq // dil, 512))
        outs.append(o_g)
        lses.append(lse_g)
    out = _b_out(outs, lses, ub, h1, w_o_b, _tile(m, 256))
    return out.reshape(batch, seq, d)
```

```python
import functools
import math

import jax
import jax.numpy as jnp
from jax import lax
from jax.experimental import pallas as pl
from jax.experimental.pallas import tpu as pltpu

F32 = jnp.float32
BF16 = jnp.bfloat16

V7X_LANES = 128
V7X_VMEM_BYTES = 64 * 1024 * 1024
VMEM_LIMIT_BYTES = V7X_VMEM_BYTES * 7 // 8

D_MODEL = 2048
MLA_HEADS = 16
Q_LORA = 512
KV_LORA = 512
QK_NOPE = 128
QK_ROPE = 64
QK_HEAD = QK_NOPE + QK_ROPE
V_HEAD = 128
MLA_WIDTH = MLA_HEADS * V_HEAD
ROPE_THETA = 10000.0
DIL_GROUPS = ((128, 1), (512, 4), (2048, 16))
N_GROUPS = 3
DIL_HEADS = 16
DIL_HEAD_DIM = 128
DIL_WIDTH = DIL_HEADS * DIL_HEAD_DIM
LOOKBACK = 128
BAND_BLK = 128
ALIBI_MAX = 8.0
EPS = 1e-6
NEG_BIG = -1e30
LOG2E = math.log2(math.e)

QK_PAD = 2 * V7X_LANES
A_IN_PAD = Q_LORA + KV_LORA + MLA_WIDTH + V7X_LANES

DIL_TILE = BAND_BLK * DIL_GROUPS[-1][1]
DIL_HEAD_CHUNK = 2
DIL_COLS = DIL_HEAD_CHUNK * DIL_HEAD_DIM


def _params(semantics):
    return pltpu.CompilerParams(dimension_semantics=semantics, vmem_limit_bytes=VMEM_LIMIT_BYTES)


def _inv_rms(x, n):
    return lax.rsqrt(jnp.sum(x * x, axis=-1, keepdims=True) * (1.0 / n) + EPS)


def _rope(rv, cos_t, s1_t, s2_t):
    return rv * cos_t + pltpu.roll(rv, 96, 1) * s1_t + pltpu.roll(rv, 32, 1) * s2_t


def _rope_table_kernel(inv_ref, cos_ref, s1_ref, s2_ref):
    ts = cos_ref.shape[0]
    row = lax.broadcasted_iota(jnp.int32, (ts, V7X_LANES), 0) + pl.program_id(0) * ts
    lane = lax.broadcasted_iota(jnp.int32, (ts, V7X_LANES), 1)
    ang = row.astype(F32) * inv_ref[...]
    c = jnp.cos(ang)
    s = jnp.sin(ang)
    half = QK_ROPE // 2
    cos_ref[...] = jnp.where(lane < QK_ROPE, c, 0.0)
    s1_ref[...] = jnp.where(lane < half, -s, 0.0)
    s2_ref[...] = jnp.where((lane >= half) & (lane < QK_ROPE), s, 0.0)


def _rope_tables(seq):
    half = QK_ROPE // 2
    inv = ROPE_THETA ** (-jnp.arange(half, dtype=F32) / half)
    inv_lanes = jnp.concatenate([inv, inv, jnp.zeros((V7X_LANES - QK_ROPE,), F32)])[None, :]
    ts = min(seq, 1024)
    tab = jax.ShapeDtypeStruct((seq, V7X_LANES), F32)
    spec = pl.BlockSpec((ts, V7X_LANES), lambda i: (i, 0))
    return pl.pallas_call(
        _rope_table_kernel,
        grid=(seq // ts,),
        in_specs=[pl.BlockSpec((1, V7X_LANES), lambda i: (0, 0))],
        out_specs=[spec, spec, spec],
        out_shape=[tab, tab, tab],
        name="rope_tables",
        compiler_params=_params(("arbitrary",)),
    )(inv_lanes)


def _a_in_kernel(x_ref, ln_ref, w_ref, lnq_ref, lnkv_ref, cq_ref, ckv_ref, sg_ref, kpe_ref):
    x = x_ref[...]
    xn = ((x * _inv_rms(x, D_MODEL)) * ln_ref[...]).astype(BF16)
    u = jnp.dot(xn, w_ref[...], preferred_element_type=F32)
    cq = u[:, :Q_LORA]
    cq_ref[...] = ((cq * _inv_rms(cq, Q_LORA)) * lnq_ref[...]).astype(BF16)
    ckv = u[:, Q_LORA:Q_LORA + KV_LORA]
    ckv_ref[...] = ((ckv * _inv_rms(ckv, KV_LORA)) * lnkv_ref[...]).astype(BF16)
    g0 = Q_LORA + KV_LORA
    gate = u[:, g0:g0 + MLA_WIDTH]
    sg_ref[...] = (gate * jax.nn.sigmoid(gate)).astype(BF16)
    kpe_ref[...] = u[:, g0 + MLA_WIDTH:]


def _a_in(x2, ln, w_pad, ln_q, ln_kv, tm):
    m = x2.shape[0]
    row = lambda n: pl.BlockSpec((tm, n), lambda i: (i, 0))
    full = lambda a: pl.BlockSpec(a.shape, lambda i: (0, 0))
    return pl.pallas_call(
        _a_in_kernel,
        grid=(m // tm,),
        in_specs=[row(D_MODEL), full(ln), full(w_pad), full(ln_q), full(ln_kv)],
        out_specs=[row(Q_LORA), row(KV_LORA), row(MLA_WIDTH), row(V7X_LANES)],
        out_shape=[jax.ShapeDtypeStruct((m, Q_LORA), BF16), jax.ShapeDtypeStruct((m, KV_LORA), BF16),
                   jax.ShapeDtypeStruct((m, MLA_WIDTH), BF16), jax.ShapeDtypeStruct((m, V7X_LANES), F32)],
        name="a_in_proj",
        compiler_params=_params(("arbitrary",)),
    )(x2, ln, w_pad, ln_q, ln_kv)


def _q_up_kernel(cq_ref, w_ref, g_ref, cos_ref, s1_ref, s2_ref, q_ref, *, scale):
    y = jnp.dot(cq_ref[...], w_ref[...], preferred_element_type=F32)
    g_n = g_ref[:, :QK_NOPE]
    g_r = g_ref[:, QK_NOPE:]
    cos_t, s1_t, s2_t = cos_ref[...], s1_ref[...], s2_ref[...]
    for h in range(MLA_HEADS):
        nope = y[:, h * QK_PAD:h * QK_PAD + QK_NOPE]
        rv = y[:, h * QK_PAD + QK_NOPE:(h + 1) * QK_PAD]
        ss = jnp.sum(nope * nope, axis=-1, keepdims=True) + jnp.sum(rv * rv, axis=-1, keepdims=True)
        r = lax.rsqrt(ss * (1.0 / QK_HEAD) + EPS)
        q_ref[h, :, :QK_NOPE] = (((nope * r) * g_n) * scale).astype(BF16)
        q_ref[h, :, QK_NOPE:] = (_rope((rv * r) * g_r, cos_t, s1_t, s2_t) * scale).astype(BF16)


def _kv_up_kernel(ckv_ref, kpe_ref, w_ref, g_ref, cos_ref, s1_ref, s2_ref, k_ref, vt_ref):
    y = jnp.dot(ckv_ref[...], w_ref[...], preferred_element_type=F32)
    g_n = g_ref[:, :QK_NOPE]
    g_r = g_ref[:, QK_NOPE:]
    kpe = kpe_ref[...]
    ss_pe = jnp.sum(kpe * kpe, axis=-1, keepdims=True)
    cos_t, s1_t, s2_t = cos_ref[...], s1_ref[...], s2_ref[...]
    hw = QK_NOPE + V_HEAD
    for h in range(MLA_HEADS):
        kn = y[:, h * hw:h * hw + QK_NOPE]
        r = lax.rsqrt((jnp.sum(kn * kn, axis=-1, keepdims=True) + ss_pe) * (1.0 / QK_HEAD) + EPS)
        k_ref[h, :, :QK_NOPE] = ((kn * r) * g_n).astype(BF16)
        k_ref[h, :, QK_NOPE:] = _rope((kpe * r) * g_r, cos_t, s1_t, s2_t).astype(BF16)
        vt_ref[h] = y[:, h * hw + QK_NOPE:(h + 1) * hw].T.astype(BF16)


def _seq_spec(tm, n):
    return pl.BlockSpec((None, tm, n), lambda b, i: (b, i, 0))


def _tab_spec(tm):
    return pl.BlockSpec((tm, V7X_LANES), lambda b, i: (i, 0))


def _head_spec(tm, n):
    return pl.BlockSpec((None, MLA_HEADS, tm, n), lambda b, i: (b, 0, i, 0))


def _q_up(cq3, w_pad, g_pad, tabs, tm):
    b, s, _ = cq3.shape
    full = lambda a: pl.BlockSpec(a.shape, lambda bb, i: (0, 0))
    scale = QK_HEAD ** -0.5 * LOG2E
    return pl.pallas_call(
        functools.partial(_q_up_kernel, scale=scale),
        grid=(b, s // tm),
        in_specs=[_seq_spec(tm, Q_LORA), full(w_pad), full(g_pad), _tab_spec(tm), _tab_spec(tm), _tab_spec(tm)],
        out_specs=_head_spec(tm, QK_PAD),
        out_shape=jax.ShapeDtypeStruct((b, MLA_HEADS, s, QK_PAD), BF16),
        name="a_q_up",
        compiler_params=_params(("arbitrary", "arbitrary")),
    )(cq3, w_pad, g_pad, *tabs)


def _kv_up(ckv3, kpe3, w, g_pad, tabs, tm):
    b, s, _ = ckv3.shape
    full = lambda a: pl.BlockSpec(a.shape, lambda bb, i: (0, 0))
    return pl.pallas_call(
        _kv_up_kernel,
        grid=(b, s // tm),
        in_specs=[_seq_spec(tm, KV_LORA), _seq_spec(tm, V7X_LANES), full(w), full(g_pad),
                  _tab_spec(tm), _tab_spec(tm), _tab_spec(tm)],
        out_specs=[_head_spec(tm, QK_PAD),
                   pl.BlockSpec((None, MLA_HEADS, None, V_HEAD, tm), lambda bb, i: (bb, 0, i, 0, 0))],
        out_shape=[jax.ShapeDtypeStruct((b, MLA_HEADS, s, QK_PAD), BF16),
                   jax.ShapeDtypeStruct((b, MLA_HEADS, s // tm, V_HEAD, tm), BF16)],
        name="a_kv_up",
        compiler_params=_params(("arbitrary", "arbitrary")),
    )(ckv3, kpe3, w, g_pad, *tabs)


def _mla_attn_kernel(q_ref, k_ref, vt_ref, o_ref, sa_sc, sb_sc, m_sc, l_sc, acc_sc, *, blk):
    qi = pl.program_id(2)
    q = q_ref[...]
    m_sc[...] = jnp.full(m_sc.shape, NEG_BIG, F32)
    l_sc[...] = jnp.zeros(l_sc.shape, F32)
    acc_sc[...] = jnp.zeros(acc_sc.shape, F32)

    def scores(kc, s_ref):
        k = k_ref[pl.ds(pl.multiple_of(kc * blk, blk), blk), :]
        s_ref[...] = lax.dot_general(k, q, (((1,), (1,)), ((), ())), preferred_element_type=F32)

    def absorb(kc, s_ref, diagonal):
        s = s_ref[...]
        if diagonal:
            key = lax.broadcasted_iota(jnp.int32, (blk, blk), 0)
            qry = lax.broadcasted_iota(jnp.int32, (blk, blk), 1)
            s = jnp.where(key <= qry, s, NEG_BIG)
        m_old = m_sc[...]
        m_new = jnp.maximum(m_old, jnp.max(s, axis=0, keepdims=True))
        alpha = jnp.exp2(m_old - m_new)
        p = jnp.exp2(s - m_new)
        l_sc[...] = alpha * l_sc[...] + jnp.sum(p, axis=0, keepdims=True)
        acc_sc[...] = alpha * acc_sc[...] + jnp.dot(vt_ref[kc], p.astype(BF16), preferred_element_type=F32)
        m_sc[...] = m_new

    scores(0, sa_sc)

    def pair(pi, carry):
        kc = 2 * pi
        scores(kc + 1, sb_sc)
        absorb(kc, sa_sc, False)
        scores(kc + 2, sa_sc)
        absorb(kc + 1, sb_sc, False)
        return carry

    lax.fori_loop(0, lax.shift_right_logical(qi, 1), pair, 0)

    @pl.when(qi % 2 == 0)
    def _():
        absorb(qi, sa_sc, True)

    @pl.when(qi % 2 == 1)
    def _():
        scores(qi, sb_sc)
        absorb(qi - 1, sa_sc, False)
        absorb(qi, sb_sc, True)

    o_ref[...] = (acc_sc[...] / l_sc[...]).T.astype(BF16)


def _mla_attention(q, k, vt, blk):
    b, h, s, _ = q.shape
    assert vt.shape[-1] == blk
    return pl.pallas_call(
        functools.partial(_mla_attn_kernel, blk=blk),
        grid=(b, h, s // blk),
        in_specs=[pl.BlockSpec((None, None, blk, QK_PAD), lambda bb, hh, i: (bb, hh, i, 0)),
                  pl.BlockSpec((None, None, s, QK_PAD), lambda bb, hh, i: (bb, hh, 0, 0)),
                  pl.BlockSpec((None, None, s // blk, V_HEAD, blk), lambda bb, hh, i: (bb, hh, 0, 0, 0))],
        out_specs=pl.BlockSpec((None, blk, V_HEAD), lambda bb, hh, i: (bb, i, hh)),
        out_shape=jax.ShapeDtypeStruct((b, s, MLA_WIDTH), BF16),
        scratch_shapes=[pltpu.VMEM((blk, blk), F32), pltpu.VMEM((blk, blk), F32),
                        pltpu.VMEM((1, blk), F32), pltpu.VMEM((1, blk), F32),
                        pltpu.VMEM((V_HEAD, blk), F32)],
        name="a_mla_attention",
        compiler_params=_params(("arbitrary", "arbitrary", "arbitrary")),
    )(q, k, vt)


def _a_out_kernel(o_ref, sg_ref, x_ref, w_ref, lnkv_ref, lnb_ref, h_ref, xkv_ref, xb_ref):
    a = (o_ref[...].astype(F32) * sg_ref[...].astype(F32)).astype(BF16)
    h = x_ref[...] + jnp.dot(a, w_ref[...], preferred_element_type=F32)
    h_ref[...] = h
    hn = h * _inv_rms(h, D_MODEL)
    xkv_ref[...] = (hn * lnkv_ref[...]).astype(BF16)
    xb_ref[...] = (hn * lnb_ref[...]).astype(BF16)


def _a_out(o2, sg, x2, w, ln_kv, ln_b, tm):
    m = x2.shape[0]
    row = pl.BlockSpec((tm, D_MODEL), lambda i: (i, 0))
    full = lambda a: pl.BlockSpec(a.shape, lambda i: (0, 0))
    return pl.pallas_call(
        _a_out_kernel,
        grid=(m // tm,),
        in_specs=[row, row, row, full(w), full(ln_kv), full(ln_b)],
        out_specs=[row, row, row],
        out_shape=[jax.ShapeDtypeStruct((m, D_MODEL), F32), jax.ShapeDtypeStruct((m, D_MODEL), BF16),
                   jax.ShapeDtypeStruct((m, D_MODEL), BF16)],
        name="a_out_proj",
        compiler_params=_params(("arbitrary",)),
    )(o2, sg, x2, w, ln_kv, ln_b)


def _wide_proj_kernel(x_ref, w_ref, g_ref, o_ref, y_sc, *, segments, scale):
    j = pl.program_id(1)
    y = jnp.dot(x_ref[...], w_ref[...], preferred_element_type=F32)
    tm, tn = y.shape

    def emit(mode, dilation):
        def value(sl):
            if mode == "norm":
                yh = y[:, sl]
                out = (yh * _inv_rms(yh, DIL_HEAD_DIM)) * g_ref[:, sl]
                return out * scale if scale != 1.0 else out
            if mode == "silu":
                return y[:, sl] * jax.nn.sigmoid(y[:, sl])
            return y[:, sl]

        for c in range(tn // DIL_HEAD_DIM):
            sl = slice(c * DIL_HEAD_DIM, (c + 1) * DIL_HEAD_DIM)
            if dilation == 1:
                o_ref[:, sl] = value(sl).astype(BF16)
            else:
                y_sc[c] = value(sl)
                rows = tm // dilation
                for r in range(dilation):
                    o_ref[r * rows:(r + 1) * rows, sl] = y_sc[c, pl.ds(r, rows, stride=dilation), :].astype(BF16)

    for lo, hi, mode, dilation in segments:
        pl.when((j >= lo) & (j < hi))(functools.partial(emit, mode, dilation))


def _wide_proj(xn, w, g_cols, segments, scale, tn):
    m, kdim = xn.shape
    n = w.shape[1]
    tm = DIL_TILE
    return pl.pallas_call(
        functools.partial(_wide_proj_kernel, segments=segments, scale=scale),
        grid=(m // tm, n // tn),
        in_specs=[pl.BlockSpec((tm, kdim), lambda i, j: (i, 0)),
                  pl.BlockSpec((kdim, tn), lambda i, j: (0, j)),
                  pl.BlockSpec((1, tn), lambda i, j: (0, j))],
        out_specs=pl.BlockSpec((tm, tn), lambda i, j: (i, j)),
        out_shape=jax.ShapeDtypeStruct((m, n), BF16),
        scratch_shapes=[pltpu.VMEM((tn // DIL_HEAD_DIM, tm, DIL_HEAD_DIM), F32)],
        name="b_wide_proj",
        compiler_params=_params(("arbitrary", "arbitrary")),
    )(xn, w, g_cols)


def _segments(tn, parts):
    segs, tile = [], 0
    for cols, mode in parts:
        if mode == "silu":
            segs.append((tile, tile + cols // tn, mode, 1))
            tile += cols // tn
        else:
            per = cols // N_GROUPS // tn
            for _, dil in DIL_GROUPS:
                segs.append((tile, tile + per, mode, dil))
                tile += per
    return tuple(segs)


def _dilated_attn_kernel(*refs):
    qs, kps, kcs, vps, vcs = (refs[i * N_GROUPS:(i + 1) * N_GROUPS] for i in range(5))
    sg_ref, out_ref, o_sc, lse_sc = refs[5 * N_GROUPS:]
    t = pl.program_id(1)
    c = pl.program_id(2)
    row = lax.broadcasted_iota(jnp.int32, (BAND_BLK, 2 * BAND_BLK), 0)
    col = lax.broadcasted_iota(jnp.int32, (BAND_BLK, 2 * BAND_BLK), 1)
    rel = row + BAND_BLK - col
    band = (rel >= 0) & (rel <= LOOKBACK)
    first_band = band & (col >= jnp.where(t > 0, 0, BAND_BLK))
    relf = rel.astype(F32)
    for g, (_, dil) in enumerate(DIL_GROUPS):
        nblk = DIL_TILE // dil // BAND_BLK
        q_ref, kp_ref, kc_ref, vp_ref, vc_ref = qs[g], kps[g], kcs[g], vps[g], vcs[g]
        for hh in range(DIL_HEAD_CHUNK):
            hs = slice(hh * DIL_HEAD_DIM, (hh + 1) * DIL_HEAD_DIM)
            head = (c * DIL_HEAD_CHUNK + hh + 1).astype(F32)
            slope = jnp.exp2(jnp.full((1, 2 * BAND_BLK), -ALIBI_MAX / DIL_HEADS, F32) * head)
            bias = (slope * (-float(dil) * LOG2E)) * relf
            for r in range(dil):
                for a in range(nblk):
                    q = q_ref[r, a, :, hs]
                    k_lo = kp_ref[r, :, hs] if a == 0 else kc_ref[r, a - 1, :, hs]
                    v_lo = vp_ref[r, :, hs] if a == 0 else vc_ref[r, a - 1, :, hs]
                    k = jnp.concatenate([k_lo, kc_ref[r, a, :, hs]], axis=0)
                    v = jnp.concatenate([v_lo, vc_ref[r, a, :, hs]], axis=0)
                    s = lax.dot_general(q, k, (((1,), (1,)), ((), ())), preferred_element_type=F32)
                    s = jnp.where(first_band if a == 0 else band, s + bias, NEG_BIG)
                    m = jnp.max(s, axis=-1, keepdims=True)
                    p = jnp.exp2(s - m)
                    den = jnp.sum(p, axis=-1, keepdims=True)
                    o = jnp.dot(p.astype(BF16), v, preferred_element_type=F32) / den
                    tok = pl.ds(a * BAND_BLK * dil + r, BAND_BLK, stride=dil) if dil > 1 else pl.ds(a * BAND_BLK, BAND_BLK)
                    o_sc[g, hh, tok, :] = o
                    lse_sc[g, hh, tok, :] = jnp.broadcast_to(m + jnp.log2(den), (BAND_BLK, V7X_LANES))
    for hh in range(DIL_HEAD_CHUNK):
        hs = slice(hh * DIL_HEAD_DIM, (hh + 1) * DIL_HEAD_DIM)
        l0, l1, l2 = lse_sc[0, hh], lse_sc[1, hh], lse_sc[2, hh]
        mx = jnp.maximum(jnp.maximum(l0, l1), l2)
        e0, e1, e2 = jnp.exp2(l0 - mx), jnp.exp2(l1 - mx), jnp.exp2(l2 - mx)
        o = (e0 * o_sc[0, hh] + e1 * o_sc[1, hh] + e2 * o_sc[2, hh]) / (e0 + e1 + e2)
        out_ref[:, hs] = (o * sg_ref[:, hs].astype(F32)).astype(BF16)


def _dilated_attention(ub, kvs, batch, seq):
    nt = seq // DIL_TILE
    q_blocks = ub.shape[1] // DIL_COLS
    kv_blocks = kvs.shape[1] // DIL_COLS
    grp_blocks = DIL_WIDTH // DIL_COLS
    ins, specs = [], []

    def add(arr, nblocks, first_block, prev):
        for g, (_, dil) in enumerate(DIL_GROUPS):
            nblk = DIL_TILE // dil // BAND_BLK
            view = arr.reshape(batch, nt, dil, nblk, BAND_BLK, nblocks * DIL_COLS)
            blk0 = first_block + g * grp_blocks
            if prev:
                spec = pl.BlockSpec((None, None, dil, None, BAND_BLK, DIL_COLS),
                                    lambda b, t, c, blk0=blk0, nblk=nblk: (b, jnp.maximum(t - 1, 0), 0, nblk - 1, 0, blk0 + c))
            else:
                spec = pl.BlockSpec((None, None, dil, nblk, BAND_BLK, DIL_COLS),
                                    lambda b, t, c, blk0=blk0: (b, t, 0, 0, 0, blk0 + c))
            ins.append(view)
            specs.append(spec)

    add(ub, q_blocks, 0, False)
    add(kvs, kv_blocks, 0, True)
    add(kvs, kv_blocks, 0, False)
    add(kvs, kv_blocks, N_GROUPS * grp_blocks, True)
    add(kvs, kv_blocks, N_GROUPS * grp_blocks, False)
    ins.append(ub.reshape(batch, nt, DIL_TILE, ub.shape[1]))
    specs.append(pl.BlockSpec((None, None, DIL_TILE, DIL_COLS),
                              lambda b, t, c: (b, t, 0, N_GROUPS * grp_blocks + c)))
    out = pl.pallas_call(
        _dilated_attn_kernel,
        grid=(batch, nt, DIL_HEADS // DIL_HEAD_CHUNK),
        in_specs=specs,
        out_specs=pl.BlockSpec((None, None, DIL_TILE, DIL_COLS), lambda b, t, c: (b, t, 0, c)),
        out_shape=jax.ShapeDtypeStruct((batch, nt, DIL_TILE, DIL_WIDTH), BF16),
        scratch_shapes=[pltpu.VMEM((N_GROUPS, DIL_HEAD_CHUNK, DIL_TILE, DIL_HEAD_DIM), F32),
                        pltpu.VMEM((N_GROUPS, DIL_HEAD_CHUNK, DIL_TILE, V7X_LANES), F32)],
        name="b_dilated_attention",
        compiler_params=_params(("arbitrary", "arbitrary", "arbitrary")),
    )(*ins)
    return out.reshape(batch * seq, DIL_WIDTH)


def _b_out_kernel(a_ref, h_ref, w_ref, out_ref):
    out_ref[...] = h_ref[...] + jnp.dot(a_ref[...], w_ref[...], preferred_element_type=F32)


def _b_out(a, h1, w, tm):
    m = h1.shape[0]
    row = pl.BlockSpec((tm, D_MODEL), lambda i: (i, 0))
    return pl.pallas_call(
        _b_out_kernel,
        grid=(m // tm,),
        in_specs=[row, row, pl.BlockSpec(w.shape, lambda i: (0, 0))],
        out_specs=row,
        out_shape=jax.ShapeDtypeStruct((m, D_MODEL), F32),
        name="b_out_proj",
        compiler_params=_params(("arbitrary",)),
    )(a, h1, w)


def _tile(n, want):
    t = min(n, want)
    assert n % t == 0, (n, t)
    return t


def kernel(x, a_ln, a_w_in, a_ln_q, a_w_q_up, a_ln_kv, a_w_kv_up, a_q_norm, a_k_norm, a_w_o,
           kv_ln, kv_w, kv_k_norm, b_ln, b_w_in, b_q_norm, b_w_o):
    batch, seq, d = x.shape
    assert d == D_MODEL and a_ln.shape[0] == 1 and b_ln.shape[0] == 1
    assert seq % DIL_TILE == 0
    m = batch * seq
    x2 = x.reshape(m, d)

    lat = Q_LORA + KV_LORA
    w_in = a_w_in[0]
    w_in_pad = jnp.concatenate(
        [w_in[:, :lat], w_in[:, lat + QK_ROPE:], w_in[:, lat:lat + QK_ROPE],
         jnp.zeros((d, V7X_LANES - QK_ROPE), F32)], axis=1).astype(BF16)
    w_q = a_w_q_up[0].reshape(Q_LORA, MLA_HEADS, QK_HEAD)
    w_q_pad = jnp.pad(w_q, ((0, 0), (0, 0), (0, QK_PAD - QK_HEAD))).reshape(Q_LORA, MLA_HEADS * QK_PAD)
    w_q_pad = w_q_pad.astype(BF16)
    w_kv = a_w_kv_up[0].astype(BF16)
    pad_gain = lambda g: jnp.pad(g, (0, QK_PAD - QK_HEAD))[None, :]
    gq_pad, gk_pad = pad_gain(a_q_norm[0]), pad_gain(a_k_norm[0])
    w_o_a = a_w_o[0].astype(BF16)
    w_kvs = kv_w.astype(BF16)
    w_b_in = b_w_in[0].astype(BF16)
    w_o_b = b_w_o[0].astype(BF16)
    k_cols = N_GROUPS * DIL_WIDTH
    gk_cols = jnp.concatenate([jnp.tile(kv_k_norm, (1, DIL_HEADS)).reshape(1, k_cols),
                               jnp.ones((1, k_cols), F32)], axis=1)
    gq_cols = jnp.concatenate([jnp.tile(b_q_norm[0], (1, DIL_HEADS)).reshape(1, k_cols),
                               jnp.ones((1, DIL_WIDTH), F32)], axis=1)

    tabs = _rope_tables(seq)
    cq, ckv, sg_a, kpe = _a_in(x2, a_ln, w_in_pad, a_ln_q, a_ln_kv, _tile(m, 256))
    blk = _tile(seq, 512)
    q = _q_up(cq.reshape(batch, seq, Q_LORA), w_q_pad, gq_pad, tabs, _tile(seq, 256))
    k, vt = _kv_up(ckv.reshape(batch, seq, KV_LORA), kpe.reshape(batch, seq, V7X_LANES), w_kv, gk_pad, tabs, blk)
    o_a = _mla_attention(q, k, vt, blk)
    h1, xn_kv, xn_b = _a_out(o_a.reshape(m, MLA_WIDTH), sg_a, x2, w_o_a, kv_ln[None, :], b_ln, _tile(m, 256))

    tn = 512
    kvs = _wide_proj(xn_kv, w_kvs, gk_cols, _segments(tn, ((k_cols, "norm"), (k_cols, "copy"))), 1.0, tn)
    ub = _wide_proj(xn_b, w_b_in, gq_cols, _segments(tn, ((k_cols, "norm"), (DIL_WIDTH, "silu"))),
                    DIL_HEAD_DIM ** -0.5 * LOG2E, tn)
    a_b = _dilated_attention(ub, kvs, batch, seq)
    out = _b_out(a_b, h1, w_o_b, _tile(m, 256))
    return out.reshape(batch, seq, d)
```

```python
import functools
import math

import jax
import jax.numpy as jnp
from jax import lax
from jax.experimental import pallas as pl
from jax.experimental.pallas import tpu as pltpu

F32 = jnp.float32
BF16 = jnp.bfloat16

V7X_LANES = 128
V7X_VMEM_BYTES = 64 * 1024 * 1024
VMEM_LIMIT_BYTES = V7X_VMEM_BYTES * 7 // 8

D_MODEL = 2048
MLA_HEADS = 16
Q_LORA = 512
KV_LORA = 512
QK_NOPE = 128
QK_ROPE = 64
QK_HEAD = QK_NOPE + QK_ROPE
V_HEAD = 128
MLA_WIDTH = MLA_HEADS * V_HEAD
ROPE_THETA = 10000.0
DIL_GROUPS = ((128, 1), (512, 4), (2048, 16))
N_GROUPS = 3
DIL_HEADS = 16
DIL_HEAD_DIM = 128
DIL_WIDTH = DIL_HEADS * DIL_HEAD_DIM
LOOKBACK = 128
BAND_BLK = 128
ALIBI_MAX = 8.0
EPS = 1e-6
NEG_BIG = -1e30
LOG2E = math.log2(math.e)

QK_PAD = 2 * V7X_LANES
A_IN_PAD = Q_LORA + KV_LORA + MLA_WIDTH + V7X_LANES

DIL_TILE = BAND_BLK * DIL_GROUPS[-1][1]
DIL_HEAD_CHUNK = 2
DIL_COLS = DIL_HEAD_CHUNK * DIL_HEAD_DIM
PROJ_ROW_CHUNK = 256
PROJ_COL_TILE = 512


def _params(semantics):
    return pltpu.CompilerParams(dimension_semantics=semantics, vmem_limit_bytes=VMEM_LIMIT_BYTES)


def _inv_rms(x, n):
    return lax.rsqrt(jnp.sum(x * x, axis=-1, keepdims=True) * (1.0 / n) + EPS)


def _rope(rv, cos_t, s1_t, s2_t):
    return rv * cos_t + pltpu.roll(rv, 96, 1) * s1_t + pltpu.roll(rv, 32, 1) * s2_t


def _rope_table_kernel(inv_ref, cos_ref, s1_ref, s2_ref):
    ts = cos_ref.shape[0]
    row = lax.broadcasted_iota(jnp.int32, (ts, V7X_LANES), 0) + pl.program_id(0) * ts
    lane = lax.broadcasted_iota(jnp.int32, (ts, V7X_LANES), 1)
    ang = row.astype(F32) * inv_ref[...]
    c = jnp.cos(ang)
    s = jnp.sin(ang)
    half = QK_ROPE // 2
    cos_ref[...] = jnp.where(lane < QK_ROPE, c, 0.0)
    s1_ref[...] = jnp.where(lane < half, -s, 0.0)
    s2_ref[...] = jnp.where((lane >= half) & (lane < QK_ROPE), s, 0.0)


def _rope_tables(seq):
    half = QK_ROPE // 2
    inv = ROPE_THETA ** (-jnp.arange(half, dtype=F32) / half)
    inv_lanes = jnp.concatenate([inv, inv, jnp.zeros((V7X_LANES - QK_ROPE,), F32)])[None, :]
    ts = min(seq, 1024)
    tab = jax.ShapeDtypeStruct((seq, V7X_LANES), F32)
    spec = pl.BlockSpec((ts, V7X_LANES), lambda i: (i, 0))
    return pl.pallas_call(
        _rope_table_kernel,
        grid=(seq // ts,),
        in_specs=[pl.BlockSpec((1, V7X_LANES), lambda i: (0, 0))],
        out_specs=[spec, spec, spec],
        out_shape=[tab, tab, tab],
        name="rope_tables",
        compiler_params=_params(("arbitrary",)),
    )(inv_lanes)


def _a_in_kernel(x_ref, ln_ref, w_ref, lnq_ref, lnkv_ref, cq_ref, ckv_ref, sg_ref, kpe_ref):
    x = x_ref[...]
    xn = ((x * _inv_rms(x, D_MODEL)) * ln_ref[...]).astype(BF16)
    u = jnp.dot(xn, w_ref[...], preferred_element_type=F32)
    cq = u[:, :Q_LORA]
    cq_ref[...] = ((cq * _inv_rms(cq, Q_LORA)) * lnq_ref[...]).astype(BF16)
    ckv = u[:, Q_LORA:Q_LORA + KV_LORA]
    ckv_ref[...] = ((ckv * _inv_rms(ckv, KV_LORA)) * lnkv_ref[...]).astype(BF16)
    g0 = Q_LORA + KV_LORA
    gate = u[:, g0:g0 + MLA_WIDTH]
    sg_ref[...] = (gate * jax.nn.sigmoid(gate)).astype(BF16)
    kpe_ref[...] = u[:, g0 + MLA_WIDTH:]


def _a_in(x2, ln, w_pad, ln_q, ln_kv, tm):
    m = x2.shape[0]
    row = lambda n: pl.BlockSpec((tm, n), lambda i: (i, 0))
    full = lambda a: pl.BlockSpec(a.shape, lambda i: (0, 0))
    return pl.pallas_call(
        _a_in_kernel,
        grid=(m // tm,),
        in_specs=[row(D_MODEL), full(ln), full(w_pad), full(ln_q), full(ln_kv)],
        out_specs=[row(Q_LORA), row(KV_LORA), row(MLA_WIDTH), row(V7X_LANES)],
        out_shape=[jax.ShapeDtypeStruct((m, Q_LORA), BF16), jax.ShapeDtypeStruct((m, KV_LORA), BF16),
                   jax.ShapeDtypeStruct((m, MLA_WIDTH), BF16), jax.ShapeDtypeStruct((m, V7X_LANES), F32)],
        name="a_in_proj",
        compiler_params=_params(("arbitrary",)),
    )(x2, ln, w_pad, ln_q, ln_kv)


def _q_up_kernel(cq_ref, w_ref, g_ref, cos_ref, s1_ref, s2_ref, q_ref, *, scale):
    y = jnp.dot(cq_ref[...], w_ref[...], preferred_element_type=F32)
    g_n = g_ref[:, :QK_NOPE]
    g_r = g_ref[:, QK_NOPE:]
    cos_t, s1_t, s2_t = cos_ref[...], s1_ref[...], s2_ref[...]
    for h in range(MLA_HEADS):
        nope = y[:, h * QK_PAD:h * QK_PAD + QK_NOPE]
        rv = y[:, h * QK_PAD + QK_NOPE:(h + 1) * QK_PAD]
        ss = jnp.sum(nope * nope, axis=-1, keepdims=True) + jnp.sum(rv * rv, axis=-1, keepdims=True)
        r = lax.rsqrt(ss * (1.0 / QK_HEAD) + EPS)
        q_ref[h, :, :QK_NOPE] = (((nope * r) * g_n) * scale).astype(BF16)
        q_ref[h, :, QK_NOPE:] = (_rope((rv * r) * g_r, cos_t, s1_t, s2_t) * scale).astype(BF16)


def _kv_up_kernel(ckv_ref, kpe_ref, w_ref, g_ref, cos_ref, s1_ref, s2_ref, k_ref, vt_ref):
    y = jnp.dot(ckv_ref[...], w_ref[...], preferred_element_type=F32)
    g_n = g_ref[:, :QK_NOPE]
    g_r = g_ref[:, QK_NOPE:]
    kpe = kpe_ref[...]
    ss_pe = jnp.sum(kpe * kpe, axis=-1, keepdims=True)
    cos_t, s1_t, s2_t = cos_ref[...], s1_ref[...], s2_ref[...]
    hw = QK_NOPE + V_HEAD
    for h in range(MLA_HEADS):
        kn = y[:, h * hw:h * hw + QK_NOPE]
        r = lax.rsqrt((jnp.sum(kn * kn, axis=-1, keepdims=True) + ss_pe) * (1.0 / QK_HEAD) + EPS)
        k_ref[h, :, :QK_NOPE] = ((kn * r) * g_n).astype(BF16)
        k_ref[h, :, QK_NOPE:] = _rope((kpe * r) * g_r, cos_t, s1_t, s2_t).astype(BF16)
        vt_ref[h] = y[:, h * hw + QK_NOPE:(h + 1) * hw].T.astype(BF16)


def _seq_spec(tm, n):
    return pl.BlockSpec((None, tm, n), lambda b, i: (b, i, 0))


def _tab_spec(tm):
    return pl.BlockSpec((tm, V7X_LANES), lambda b, i: (i, 0))


def _head_spec(tm, n):
    return pl.BlockSpec((None, MLA_HEADS, tm, n), lambda b, i: (b, 0, i, 0))


def _q_up(cq3, w_pad, g_pad, tabs, tm):
    b, s, _ = cq3.shape
    full = lambda a: pl.BlockSpec(a.shape, lambda bb, i: (0, 0))
    scale = QK_HEAD ** -0.5 * LOG2E
    return pl.pallas_call(
        functools.partial(_q_up_kernel, scale=scale),
        grid=(b, s // tm),
        in_specs=[_seq_spec(tm, Q_LORA), full(w_pad), full(g_pad), _tab_spec(tm), _tab_spec(tm), _tab_spec(tm)],
        out_specs=_head_spec(tm, QK_PAD),
        out_shape=jax.ShapeDtypeStruct((b, MLA_HEADS, s, QK_PAD), BF16),
        name="a_q_up",
        compiler_params=_params(("arbitrary", "arbitrary")),
    )(cq3, w_pad, g_pad, *tabs)


def _kv_up(ckv3, kpe3, w, g_pad, tabs, tm):
    b, s, _ = ckv3.shape
    full = lambda a: pl.BlockSpec(a.shape, lambda bb, i: (0, 0))
    return pl.pallas_call(
        _kv_up_kernel,
        grid=(b, s // tm),
        in_specs=[_seq_spec(tm, KV_LORA), _seq_spec(tm, V7X_LANES), full(w), full(g_pad),
                  _tab_spec(tm), _tab_spec(tm), _tab_spec(tm)],
        out_specs=[_head_spec(tm, QK_PAD),
                   pl.BlockSpec((None, MLA_HEADS, None, V_HEAD, tm), lambda bb, i: (bb, 0, i, 0, 0))],
        out_shape=[jax.ShapeDtypeStruct((b, MLA_HEADS, s, QK_PAD), BF16),
                   jax.ShapeDtypeStruct((b, MLA_HEADS, s // tm, V_HEAD, tm), BF16)],
        name="a_kv_up",
        compiler_params=_params(("arbitrary", "arbitrary")),
    )(ckv3, kpe3, w, g_pad, *tabs)


def _mla_attn_kernel(q_ref, k_ref, vt_ref, o_ref, sa_sc, sb_sc, m_sc, l_sc, acc_sc, *, tq, tk):
    qi = pl.program_id(2)
    q = q_ref[...]
    m_sc[...] = jnp.full(m_sc.shape, NEG_BIG, F32)
    l_sc[...] = jnp.zeros(l_sc.shape, F32)
    acc_sc[...] = jnp.zeros(acc_sc.shape, F32)

    def scores(kc, s_ref):
        k = k_ref[pl.ds(pl.multiple_of(kc * tk, tk), tk), :]
        s_ref[...] = lax.dot_general(k, q, (((1,), (1,)), ((), ())), preferred_element_type=F32)

    def absorb(kc, s_ref, key_offset):
        s = s_ref[...]
        if key_offset is not None:
            key = lax.broadcasted_iota(jnp.int32, (tk, tq), 0) + key_offset
            qry = lax.broadcasted_iota(jnp.int32, (tk, tq), 1)
            s = jnp.where(key <= qry, s, NEG_BIG)
        m_old = m_sc[...]
        m_new = jnp.maximum(m_old, jnp.max(s, axis=0, keepdims=True))
        alpha = jnp.exp2(m_old - m_new)
        p = jnp.exp2(s - m_new)
        l_sc[...] = alpha * l_sc[...] + jnp.sum(p, axis=0, keepdims=True)
        acc_sc[...] = alpha * acc_sc[...] + jnp.dot(vt_ref[kc], p.astype(BF16), preferred_element_type=F32)
        m_sc[...] = m_new

    scores(0, sa_sc)

    def pair(pi, carry):
        kc = 2 * pi
        scores(kc + 1, sb_sc)
        absorb(kc, sa_sc, None)
        scores(kc + 2, sa_sc)
        absorb(kc + 1, sb_sc, None)
        return carry

    lax.fori_loop(0, qi, pair, 0)
    scores(2 * qi + 1, sb_sc)
    absorb(2 * qi, sa_sc, 0)
    absorb(2 * qi + 1, sb_sc, tk)
    o_ref[...] = (acc_sc[...] / l_sc[...]).T.astype(BF16)


def _mla_attention(q, k, vt, tk):
    b, h, s, _ = q.shape
    tq = 2 * tk
    assert vt.shape[-1] == tk and s % tq == 0
    return pl.pallas_call(
        functools.partial(_mla_attn_kernel, tq=tq, tk=tk),
        grid=(b, h, s // tq),
        in_specs=[pl.BlockSpec((None, None, tq, QK_PAD), lambda bb, hh, i: (bb, hh, i, 0)),
                  pl.BlockSpec((None, None, s, QK_PAD), lambda bb, hh, i: (bb, hh, 0, 0)),
                  pl.BlockSpec((None, None, s // tk, V_HEAD, tk), lambda bb, hh, i: (bb, hh, 0, 0, 0))],
        out_specs=pl.BlockSpec((None, tq, V_HEAD), lambda bb, hh, i: (bb, i, hh)),
        out_shape=jax.ShapeDtypeStruct((b, s, MLA_WIDTH), BF16),
        scratch_shapes=[pltpu.VMEM((tk, tq), F32), pltpu.VMEM((tk, tq), F32),
                        pltpu.VMEM((1, tq), F32), pltpu.VMEM((1, tq), F32),
                        pltpu.VMEM((V_HEAD, tq), F32)],
        name="a_mla_attention",
        compiler_params=_params(("arbitrary", "arbitrary", "arbitrary")),
    )(q, k, vt)


def _a_out_kernel(o_ref, sg_ref, x_ref, w_ref, lnkv_ref, lnb_ref, h_ref, xkv_ref, xb_ref):
    a = (o_ref[...].astype(F32) * sg_ref[...].astype(F32)).astype(BF16)
    h = x_ref[...] + jnp.dot(a, w_ref[...], preferred_element_type=F32)
    h_ref[...] = h
    hn = h * _inv_rms(h, D_MODEL)
    xkv_ref[...] = (hn * lnkv_ref[...]).astype(BF16)
    xb_ref[...] = (hn * lnb_ref[...]).astype(BF16)


def _a_out(o2, sg, x2, w, ln_kv, ln_b, tm):
    m = x2.shape[0]
    row = pl.BlockSpec((tm, D_MODEL), lambda i: (i, 0))
    full = lambda a: pl.BlockSpec(a.shape, lambda i: (0, 0))
    return pl.pallas_call(
        _a_out_kernel,
        grid=(m // tm,),
        in_specs=[row, row, row, full(w), full(ln_kv), full(ln_b)],
        out_specs=[row, row, row],
        out_shape=[jax.ShapeDtypeStruct((m, D_MODEL), F32), jax.ShapeDtypeStruct((m, D_MODEL), BF16),
                   jax.ShapeDtypeStruct((m, D_MODEL), BF16)],
        name="a_out_proj",
        compiler_params=_params(("arbitrary",)),
    )(o2, sg, x2, w, ln_kv, ln_b)


def _branch_proj_kernel(x_ref, w_ref, g_ref, o_ref, y_sc, *, mode, dilation, scale):
    tm, tn = o_ref.shape
    rows = PROJ_ROW_CHUNK // dilation
    for c in range(tm // PROJ_ROW_CHUNK):
        y = jnp.dot(x_ref[c * PROJ_ROW_CHUNK:(c + 1) * PROJ_ROW_CHUNK, :], w_ref[...], preferred_element_type=F32)
        for hc in range(tn // DIL_HEAD_DIM):
            sl = slice(hc * DIL_HEAD_DIM, (hc + 1) * DIL_HEAD_DIM)
            yh = y[:, sl]
            if mode == "norm":
                yh = (yh * _inv_rms(yh, DIL_HEAD_DIM)) * g_ref[:, sl]
                if scale != 1.0:
                    yh = yh * scale
            elif mode == "silu":
                yh = yh * jax.nn.sigmoid(yh)
            if dilation == 1:
                o_ref[c * PROJ_ROW_CHUNK:(c + 1) * PROJ_ROW_CHUNK, sl] = yh.astype(BF16)
            else:
                y_sc[hc, c * PROJ_ROW_CHUNK:(c + 1) * PROJ_ROW_CHUNK, :] = yh
                for r in range(dilation):
                    dst = r * (tm // dilation) + c * rows
                    o_ref[dst:dst + rows, sl] = y_sc[hc, pl.ds(c * PROJ_ROW_CHUNK + r, rows, stride=dilation), :].astype(BF16)


def _branch_proj(xn, w, g_cols, col0, mode, dilation, scale, name):
    m, kdim = xn.shape
    tm, tn = DIL_TILE, PROJ_COL_TILE
    blk0 = col0 // tn
    return pl.pallas_call(
        functools.partial(_branch_proj_kernel, mode=mode, dilation=dilation, scale=scale),
        grid=(m // tm, DIL_WIDTH // tn),
        in_specs=[pl.BlockSpec((tm, kdim), lambda i, j: (i, 0)),
                  pl.BlockSpec((kdim, tn), lambda i, j: (0, blk0 + j)),
                  pl.BlockSpec((1, tn), lambda i, j: (0, blk0 + j))],
        out_specs=pl.BlockSpec((tm, tn), lambda i, j: (i, j)),
        out_shape=jax.ShapeDtypeStruct((m, DIL_WIDTH), BF16),
        scratch_shapes=[pltpu.VMEM((tn // DIL_HEAD_DIM, tm, DIL_HEAD_DIM), F32)],
        name=name,
        compiler_params=_params(("arbitrary", "arbitrary")),
    )(xn, w, g_cols)


def _dilated_attn_kernel(*refs):
    qs, kps, kcs, vps, vcs = (refs[i * N_GROUPS:(i + 1) * N_GROUPS] for i in range(5))
    sg_ref, out_ref, o_sc, lse_sc = refs[5 * N_GROUPS:]
    t = pl.program_id(1)
    c = pl.program_id(2)
    row = lax.broadcasted_iota(jnp.int32, (BAND_BLK, 2 * BAND_BLK), 0)
    col = lax.broadcasted_iota(jnp.int32, (BAND_BLK, 2 * BAND_BLK), 1)
    rel = row + BAND_BLK - col
    band = (rel >= 0) & (rel <= LOOKBACK)
    first_band = band & (col >= jnp.where(t > 0, 0, BAND_BLK))
    relf = rel.astype(F32)
    for g, (_, dil) in enumerate(DIL_GROUPS):
        nblk = DIL_TILE // dil // BAND_BLK
        q_ref, kp_ref, kc_ref, vp_ref, vc_ref = qs[g], kps[g], kcs[g], vps[g], vcs[g]
        for hh in range(DIL_HEAD_CHUNK):
            hs = slice(hh * DIL_HEAD_DIM, (hh + 1) * DIL_HEAD_DIM)
            head = (c * DIL_HEAD_CHUNK + hh + 1).astype(F32)
            slope = jnp.exp2(jnp.full((1, 2 * BAND_BLK), -ALIBI_MAX / DIL_HEADS, F32) * head)
            bias = (slope * (-float(dil) * LOG2E)) * relf
            for r in range(dil):
                for a in range(nblk):
                    q = q_ref[r, a, :, hs]
                    k_lo = kp_ref[r, :, hs] if a == 0 else kc_ref[r, a - 1, :, hs]
                    v_lo = vp_ref[r, :, hs] if a == 0 else vc_ref[r, a - 1, :, hs]
                    k = jnp.concatenate([k_lo, kc_ref[r, a, :, hs]], axis=0)
                    v = jnp.concatenate([v_lo, vc_ref[r, a, :, hs]], axis=0)
                    s = lax.dot_general(q, k, (((1,), (1,)), ((), ())), preferred_element_type=F32)
                    s = jnp.where(first_band if a == 0 else band, s + bias, NEG_BIG)
                    m = jnp.max(s, axis=-1, keepdims=True)
                    p = jnp.exp2(s - m)
                    den = jnp.sum(p, axis=-1, keepdims=True)
                    o = jnp.dot(p.astype(BF16), v, preferred_element_type=F32) / den
                    tok = pl.ds(a * BAND_BLK * dil + r, BAND_BLK, stride=dil) if dil > 1 else pl.ds(a * BAND_BLK, BAND_BLK)
                    o_sc[g, hh, tok, :] = o
                    lse_sc[g, hh, tok, :] = jnp.broadcast_to(m + jnp.log2(den), (BAND_BLK, V7X_LANES))
    for hh in range(DIL_HEAD_CHUNK):
        hs = slice(hh * DIL_HEAD_DIM, (hh + 1) * DIL_HEAD_DIM)
        l0, l1, l2 = lse_sc[0, hh], lse_sc[1, hh], lse_sc[2, hh]
        mx = jnp.maximum(jnp.maximum(l0, l1), l2)
        e0, e1, e2 = jnp.exp2(l0 - mx), jnp.exp2(l1 - mx), jnp.exp2(l2 - mx)
        o = (e0 * o_sc[0, hh] + e1 * o_sc[1, hh] + e2 * o_sc[2, hh]) / (e0 + e1 + e2)
        out_ref[:, hs] = (o * sg_ref[:, hs].astype(F32)).astype(BF16)


def _dilated_attention(qs, ks, vs, gate, batch, seq):
    nt = seq // DIL_TILE
    ins, specs = [], []

    def add(arrs, prev):
        for arr, (_, dil) in zip(arrs, DIL_GROUPS):
            nblk = DIL_TILE // dil // BAND_BLK
            ins.append(arr.reshape(batch, nt, dil, nblk, BAND_BLK, DIL_WIDTH))
            if prev:
                specs.append(pl.BlockSpec((None, None, dil, None, BAND_BLK, DIL_COLS),
                                          lambda b, t, c, nblk=nblk: (b, jnp.maximum(t - 1, 0), 0, nblk - 1, 0, c)))
            else:
                specs.append(pl.BlockSpec((None, None, dil, nblk, BAND_BLK, DIL_COLS),
                                          lambda b, t, c: (b, t, 0, 0, 0, c)))

    add(qs, False)
    add(ks, True)
    add(ks, False)
    add(vs, True)
    add(vs, False)
    ins.append(gate.reshape(batch, nt, DIL_TILE, DIL_WIDTH))
    specs.append(pl.BlockSpec((None, None, DIL_TILE, DIL_COLS), lambda b, t, c: (b, t, 0, c)))
    out = pl.pallas_call(
        _dilated_attn_kernel,
        grid=(batch, nt, DIL_HEADS // DIL_HEAD_CHUNK),
        in_specs=specs,
        out_specs=pl.BlockSpec((None, None, DIL_TILE, DIL_COLS), lambda b, t, c: (b, t, 0, c)),
        out_shape=jax.ShapeDtypeStruct((batch, nt, DIL_TILE, DIL_WIDTH), BF16),
        scratch_shapes=[pltpu.VMEM((N_GROUPS, DIL_HEAD_CHUNK, DIL_TILE, DIL_HEAD_DIM), F32),
                        pltpu.VMEM((N_GROUPS, DIL_HEAD_CHUNK, DIL_TILE, V7X_LANES), F32)],
        name="b_dilated_attention",
        compiler_params=_params(("arbitrary", "arbitrary", "arbitrary")),
    )(*ins)
    return out.reshape(batch * seq, DIL_WIDTH)


def _b_out_kernel(a_ref, h_ref, w_ref, out_ref):
    out_ref[...] = h_ref[...] + jnp.dot(a_ref[...], w_ref[...], preferred_element_type=F32)


def _b_out(a, h1, w, tm):
    m = h1.shape[0]
    row = pl.BlockSpec((tm, D_MODEL), lambda i: (i, 0))
    return pl.pallas_call(
        _b_out_kernel,
        grid=(m // tm,),
        in_specs=[row, row, pl.BlockSpec(w.shape, lambda i: (0, 0))],
        out_specs=row,
        out_shape=jax.ShapeDtypeStruct((m, D_MODEL), F32),
        name="b_out_proj",
        compiler_params=_params(("arbitrary",)),
    )(a, h1, w)


def _tile(n, want):
    t = min(n, want)
    assert n % t == 0, (n, t)
    return t


def kernel(x, a_ln, a_w_in, a_ln_q, a_w_q_up, a_ln_kv, a_w_kv_up, a_q_norm, a_k_norm, a_w_o,
           kv_ln, kv_w, kv_k_norm, b_ln, b_w_in, b_q_norm, b_w_o):
    batch, seq, d = x.shape
    assert d == D_MODEL and a_ln.shape[0] == 1 and b_ln.shape[0] == 1
    assert seq % DIL_TILE == 0
    m = batch * seq
    x2 = x.reshape(m, d)

    lat = Q_LORA + KV_LORA
    w_in = a_w_in[0]
    w_in_pad = jnp.concatenate(
        [w_in[:, :lat], w_in[:, lat + QK_ROPE:], w_in[:, lat:lat + QK_ROPE],
         jnp.zeros((d, V7X_LANES - QK_ROPE), F32)], axis=1).astype(BF16)
    w_q = a_w_q_up[0].reshape(Q_LORA, MLA_HEADS, QK_HEAD)
    w_q_pad = jnp.pad(w_q, ((0, 0), (0, 0), (0, QK_PAD - QK_HEAD))).reshape(Q_LORA, MLA_HEADS * QK_PAD)
    w_q_pad = w_q_pad.astype(BF16)
    w_kv = a_w_kv_up[0].astype(BF16)
    pad_gain = lambda g: jnp.pad(g, (0, QK_PAD - QK_HEAD))[None, :]
    gq_pad, gk_pad = pad_gain(a_q_norm[0]), pad_gain(a_k_norm[0])
    w_o_a = a_w_o[0].astype(BF16)
    w_kvs = kv_w.astype(BF16)
    w_b_in = b_w_in[0].astype(BF16)
    w_o_b = b_w_o[0].astype(BF16)
    k_cols = N_GROUPS * DIL_WIDTH
    gk_cols = jnp.concatenate([jnp.tile(kv_k_norm, (1, DIL_HEADS)).reshape(1, k_cols),
                               jnp.ones((1, k_cols), F32)], axis=1)
    gq_cols = jnp.concatenate([jnp.tile(b_q_norm[0], (1, DIL_HEADS)).reshape(1, k_cols),
                               jnp.ones((1, DIL_WIDTH), F32)], axis=1)

    tabs = _rope_tables(seq)
    cq, ckv, sg_a, kpe = _a_in(x2, a_ln, w_in_pad, a_ln_q, a_ln_kv, _tile(m, 256))
    blk = _tile(seq, 512)
    q = _q_up(cq.reshape(batch, seq, Q_LORA), w_q_pad, gq_pad, tabs, _tile(seq, 256))
    k, vt = _kv_up(ckv.reshape(batch, seq, KV_LORA), kpe.reshape(batch, seq, V7X_LANES), w_kv, gk_pad, tabs, blk)
    o_a = _mla_attention(q, k, vt, blk)
    h1, xn_kv, xn_b = _a_out(o_a.reshape(m, MLA_WIDTH), sg_a, x2, w_o_a, kv_ln[None, :], b_ln, _tile(m, 256))

    q_scale = DIL_HEAD_DIM ** -0.5 * LOG2E
    qs, ks, vs = [], [], []
    for g, (_, dil) in enumerate(DIL_GROUPS):
        ks.append(_branch_proj(xn_kv, w_kvs, gk_cols, g * DIL_WIDTH, "norm", dil, 1.0, f"b_k_proj_d{dil}"))
        vs.append(_branch_proj(xn_kv, w_kvs, gk_cols, k_cols + g * DIL_WIDTH, "copy", dil, 1.0, f"b_v_proj_d{dil}"))
        qs.append(_branch_proj(xn_b, w_b_in, gq_cols, g * DIL_WIDTH, "norm", dil, q_scale, f"b_q_proj_d{dil}"))
    sg_b = _branch_proj(xn_b, w_b_in, gq_cols, k_cols, "silu", 1, 1.0, "b_gate_proj")
    a_b = _dilated_attention(qs, ks, vs, sg_b, batch, seq)
    out = _b_out(a_b, h1, w_o_b, _tile(m, 256))
    return out.reshape(batch, seq, d)
```

```python
import functools
import math

import jax
import jax.numpy as jnp
from jax import lax
from jax.experimental import pallas as pl
from jax.experimental.pallas import tpu as pltpu

F32 = jnp.float32
BF16 = jnp.bfloat16

V7X_LANES = 128
V7X_VMEM_BYTES = 64 * 1024 * 1024
VMEM_LIMIT_BYTES = V7X_VMEM_BYTES * 7 // 8

D_MODEL = 2048
MLA_HEADS = 16
Q_LORA = 512
KV_LORA = 512
QK_NOPE = 128
QK_ROPE = 64
QK_HEAD = QK_NOPE + QK_ROPE
V_HEAD = 128
MLA_WIDTH = MLA_HEADS * V_HEAD
ROPE_THETA = 10000.0
DIL_GROUPS = ((128, 1), (512, 4), (2048, 16))
N_GROUPS = 3
DIL_HEADS = 16
DIL_HEAD_DIM = 128
DIL_WIDTH = DIL_HEADS * DIL_HEAD_DIM
LOOKBACK = 128
BAND_BLK = 128
ALIBI_MAX = 8.0
EPS = 1e-6
NEG_BIG = -1e30
LOG2E = math.log2(math.e)

QK_PAD = 2 * V7X_LANES
A_IN_PAD = Q_LORA + KV_LORA + MLA_WIDTH + V7X_LANES

DIL_TILE = BAND_BLK * DIL_GROUPS[-1][1]
DIL_HEAD_CHUNK = 2
DIL_COLS = DIL_HEAD_CHUNK * DIL_HEAD_DIM
VT_ROWS = V_HEAD + 16
STRIDE_STEP = 4
PROJ_ROW_CHUNK = 256
PROJ_COL_TILE = 512


def _params(semantics):
    return pltpu.CompilerParams(dimension_semantics=semantics, vmem_limit_bytes=VMEM_LIMIT_BYTES)


def _inv_rms(x, n):
    return lax.rsqrt(jnp.sum(x * x, axis=-1, keepdims=True) * (1.0 / n) + EPS)


def _rope(rv, cos_t, s1_t, s2_t):
    return rv * cos_t + pltpu.roll(rv, 96, 1) * s1_t + pltpu.roll(rv, 32, 1) * s2_t


def _rope_table_kernel(inv_ref, cos_ref, s1_ref, s2_ref):
    ts = cos_ref.shape[0]
    row = lax.broadcasted_iota(jnp.int32, (ts, V7X_LANES), 0) + pl.program_id(0) * ts
    lane = lax.broadcasted_iota(jnp.int32, (ts, V7X_LANES), 1)
    ang = row.astype(F32) * inv_ref[...]
    c = jnp.cos(ang)
    s = jnp.sin(ang)
    half = QK_ROPE // 2
    cos_ref[...] = jnp.where(lane < QK_ROPE, c, 0.0)
    s1_ref[...] = jnp.where(lane < half, -s, 0.0)
    s2_ref[...] = jnp.where((lane >= half) & (lane < QK_ROPE), s, 0.0)


def _rope_tables(seq):
    half = QK_ROPE // 2
    inv = ROPE_THETA ** (-jnp.arange(half, dtype=F32) / half)
    inv_lanes = jnp.concatenate([inv, inv, jnp.zeros((V7X_LANES - QK_ROPE,), F32)])[None, :]
    ts = min(seq, 1024)
    tab = jax.ShapeDtypeStruct((seq, V7X_LANES), F32)
    spec = pl.BlockSpec((ts, V7X_LANES), lambda i: (i, 0))
    return pl.pallas_call(
        _rope_table_kernel,
        grid=(seq // ts,),
        in_specs=[pl.BlockSpec((1, V7X_LANES), lambda i: (0, 0))],
        out_specs=[spec, spec, spec],
        out_shape=[tab, tab, tab],
        name="rope_tables",
        compiler_params=_params(("arbitrary",)),
    )(inv_lanes)


def _a_in_kernel(x_ref, ln_ref, w_ref, lnq_ref, lnkv_ref, cq_ref, ckv_ref, sg_ref, kpe_ref):
    x = x_ref[...]
    xn = ((x * _inv_rms(x, D_MODEL)) * ln_ref[...]).astype(BF16)
    u = jnp.dot(xn, w_ref[...], preferred_element_type=F32)
    cq = u[:, :Q_LORA]
    cq_ref[...] = ((cq * _inv_rms(cq, Q_LORA)) * lnq_ref[...]).astype(BF16)
    ckv = u[:, Q_LORA:Q_LORA + KV_LORA]
    ckv_ref[...] = ((ckv * _inv_rms(ckv, KV_LORA)) * lnkv_ref[...]).astype(BF16)
    g0 = Q_LORA + KV_LORA
    gate = u[:, g0:g0 + MLA_WIDTH]
    sg_ref[...] = (gate * jax.nn.sigmoid(gate)).astype(BF16)
    kpe_ref[...] = u[:, g0 + MLA_WIDTH:]


def _a_in(x2, ln, w_pad, ln_q, ln_kv, tm):
    m = x2.shape[0]
    row = lambda n: pl.BlockSpec((tm, n), lambda i: (i, 0))
    full = lambda a: pl.BlockSpec(a.shape, lambda i: (0, 0))
    return pl.pallas_call(
        _a_in_kernel,
        grid=(m // tm,),
        in_specs=[row(D_MODEL), full(ln), full(w_pad), full(ln_q), full(ln_kv)],
        out_specs=[row(Q_LORA), row(KV_LORA), row(MLA_WIDTH), row(V7X_LANES)],
        out_shape=[jax.ShapeDtypeStruct((m, Q_LORA), BF16), jax.ShapeDtypeStruct((m, KV_LORA), BF16),
                   jax.ShapeDtypeStruct((m, MLA_WIDTH), BF16), jax.ShapeDtypeStruct((m, V7X_LANES), F32)],
        name="a_in_proj",
        compiler_params=_params(("arbitrary",)),
    )(x2, ln, w_pad, ln_q, ln_kv)


def _q_up_kernel(cq_ref, w_ref, g_ref, cos_ref, s1_ref, s2_ref, q_ref, *, scale):
    y = jnp.dot(cq_ref[...], w_ref[...], preferred_element_type=F32)
    g_n = g_ref[:, :QK_NOPE]
    g_r = g_ref[:, QK_NOPE:]
    cos_t, s1_t, s2_t = cos_ref[...], s1_ref[...], s2_ref[...]
    for h in range(MLA_HEADS):
        nope = y[:, h * QK_PAD:h * QK_PAD + QK_NOPE]
        rv = y[:, h * QK_PAD + QK_NOPE:(h + 1) * QK_PAD]
        ss = jnp.sum(nope * nope, axis=-1, keepdims=True) + jnp.sum(rv * rv, axis=-1, keepdims=True)
        r = lax.rsqrt(ss * (1.0 / QK_HEAD) + EPS)
        q_ref[h, :QK_NOPE, :] = (((nope * r) * g_n) * scale).T.astype(BF16)
        q_ref[h, QK_NOPE:, :] = (_rope((rv * r) * g_r, cos_t, s1_t, s2_t) * scale).T.astype(BF16)


def _kv_up_kernel(ckv_ref, kpe_ref, w_ref, g_ref, cos_ref, s1_ref, s2_ref, k_ref, vt_ref):
    y = jnp.dot(ckv_ref[...], w_ref[...], preferred_element_type=F32)
    g_n = g_ref[:, :QK_NOPE]
    g_r = g_ref[:, QK_NOPE:]
    kpe = kpe_ref[...]
    ss_pe = jnp.sum(kpe * kpe, axis=-1, keepdims=True)
    cos_t, s1_t, s2_t = cos_ref[...], s1_ref[...], s2_ref[...]
    hw = QK_NOPE + V_HEAD
    pad_row = lax.broadcasted_iota(jnp.int32, (VT_ROWS - V_HEAD, y.shape[0]), 0)
    ones_rows = jnp.where(pad_row == 0, 1.0, 0.0).astype(BF16)
    for h in range(MLA_HEADS):
        kn = y[:, h * hw:h * hw + QK_NOPE]
        r = lax.rsqrt((jnp.sum(kn * kn, axis=-1, keepdims=True) + ss_pe) * (1.0 / QK_HEAD) + EPS)
        k_ref[h, :, :QK_NOPE] = ((kn * r) * g_n).astype(BF16)
        k_ref[h, :, QK_NOPE:] = _rope((kpe * r) * g_r, cos_t, s1_t, s2_t).astype(BF16)
        vt_ref[h, :V_HEAD, :] = y[:, h * hw + QK_NOPE:(h + 1) * hw].T.astype(BF16)
        vt_ref[h, V_HEAD:, :] = ones_rows


def _seq_spec(tm, n):
    return pl.BlockSpec((None, tm, n), lambda b, i: (b, i, 0))


def _tab_spec(tm):
    return pl.BlockSpec((tm, V7X_LANES), lambda b, i: (i, 0))


def _head_spec(tm, n):
    return pl.BlockSpec((None, MLA_HEADS, tm, n), lambda b, i: (b, 0, i, 0))


def _q_up(cq3, w_pad, g_pad, tabs, tm):
    b, s, _ = cq3.shape
    full = lambda a: pl.BlockSpec(a.shape, lambda bb, i: (0, 0))
    scale = QK_HEAD ** -0.5 * LOG2E
    return pl.pallas_call(
        functools.partial(_q_up_kernel, scale=scale),
        grid=(b, s // tm),
        in_specs=[_seq_spec(tm, Q_LORA), full(w_pad), full(g_pad), _tab_spec(tm), _tab_spec(tm), _tab_spec(tm)],
        out_specs=pl.BlockSpec((None, MLA_HEADS, QK_PAD, tm), lambda bb, i: (bb, 0, 0, i)),
        out_shape=jax.ShapeDtypeStruct((b, MLA_HEADS, QK_PAD, s), BF16),
        name="a_q_up",
        compiler_params=_params(("arbitrary", "arbitrary")),
    )(cq3, w_pad, g_pad, *tabs)


def _kv_up(ckv3, kpe3, w, g_pad, tabs, tm):
    b, s, _ = ckv3.shape
    full = lambda a: pl.BlockSpec(a.shape, lambda bb, i: (0, 0))
    return pl.pallas_call(
        _kv_up_kernel,
        grid=(b, s // tm),
        in_specs=[_seq_spec(tm, KV_LORA), _seq_spec(tm, V7X_LANES), full(w), full(g_pad),
                  _tab_spec(tm), _tab_spec(tm), _tab_spec(tm)],
        out_specs=[_head_spec(tm, QK_PAD),
                   pl.BlockSpec((None, MLA_HEADS, None, VT_ROWS, tm), lambda bb, i: (bb, 0, i, 0, 0))],
        out_shape=[jax.ShapeDtypeStruct((b, MLA_HEADS, s, QK_PAD), BF16),
                   jax.ShapeDtypeStruct((b, MLA_HEADS, s // tm, VT_ROWS, tm), BF16)],
        name="a_kv_up",
        compiler_params=_params(("arbitrary", "arbitrary")),
    )(ckv3, kpe3, w, g_pad, *tabs)


def _mla_attn_kernel(q_ref, k_ref, vt_ref, o_ref, sa_sc, sb_sc, mxa_sc, mxb_sc, m_sc, acc_sc, *, tq, tk):
    qi = pl.program_id(2)
    qt = q_ref[...]
    m_sc[...] = jnp.full(m_sc.shape, NEG_BIG, F32)
    acc_sc[...] = jnp.zeros(acc_sc.shape, F32)
    bufs = ((sa_sc, mxa_sc), (sb_sc, mxb_sc))

    def scores(kc, par):
        k = k_ref[pl.ds(pl.multiple_of(kc * tk, tk), tk), :]
        s = jnp.dot(k, qt, preferred_element_type=F32)
        bufs[par][0][...] = s
        bufs[par][1][...] = jnp.max(s, axis=0, keepdims=True)

    def absorb(kc, par, key_offset):
        s_ref, mx_ref = bufs[par]
        s = s_ref[...]
        if key_offset is None:
            mx = mx_ref[...]
        else:
            key = lax.broadcasted_iota(jnp.int32, (tk, tq), 0) + key_offset
            qry = lax.broadcasted_iota(jnp.int32, (tk, tq), 1)
            s = jnp.where(key <= qry, s, NEG_BIG)
            mx = jnp.max(s, axis=0, keepdims=True)
        m_old = m_sc[...]
        m_new = jnp.maximum(m_old, mx)
        alpha = jnp.exp2(m_old - m_new)
        p = jnp.exp2(s - m_new).astype(BF16)
        acc_sc[...] = alpha * acc_sc[...] + jnp.dot(vt_ref[kc], p, preferred_element_type=F32)
        m_sc[...] = m_new

    scores(0, 0)

    def pair(pi, carry):
        kc = 2 * pi
        scores(kc + 1, 1)
        absorb(kc, 0, None)
        scores(kc + 2, 0)
        absorb(kc + 1, 1, None)
        return carry

    lax.fori_loop(0, qi, pair, 0)
    scores(2 * qi + 1, 1)
    absorb(2 * qi, 0, 0)
    absorb(2 * qi + 1, 1, tk)
    o_ref[...] = (acc_sc[:V_HEAD, :] / acc_sc[V_HEAD:V_HEAD + 1, :]).T.astype(BF16)


def _mla_attention(qt, k, vt, tk):
    b, h, s, _ = k.shape
    tq = 2 * tk
    assert vt.shape[-1] == tk and s % tq == 0
    return pl.pallas_call(
        functools.partial(_mla_attn_kernel, tq=tq, tk=tk),
        grid=(b, h, s // tq),
        in_specs=[pl.BlockSpec((None, None, QK_PAD, tq), lambda bb, hh, i: (bb, hh, 0, i)),
                  pl.BlockSpec((None, None, s, QK_PAD), lambda bb, hh, i: (bb, hh, 0, 0)),
                  pl.BlockSpec((None, None, s // tk, VT_ROWS, tk), lambda bb, hh, i: (bb, hh, 0, 0, 0))],
        out_specs=pl.BlockSpec((None, tq, V_HEAD), lambda bb, hh, i: (bb, i, hh)),
        out_shape=jax.ShapeDtypeStruct((b, s, MLA_WIDTH), BF16),
        scratch_shapes=[pltpu.VMEM((tk, tq), F32), pltpu.VMEM((tk, tq), F32),
                        pltpu.VMEM((1, tq), F32), pltpu.VMEM((1, tq), F32),
                        pltpu.VMEM((1, tq), F32), pltpu.VMEM((VT_ROWS, tq), F32)],
        name="a_mla_attention",
        compiler_params=_params(("arbitrary", "arbitrary", "arbitrary")),
    )(qt, k, vt)


def _a_out_kernel(o_ref, sg_ref, x_ref, w_ref, lnkv_ref, lnb_ref, h_ref, xkv_ref, xb_ref):
    a = (o_ref[...].astype(F32) * sg_ref[...].astype(F32)).astype(BF16)
    h = x_ref[...] + jnp.dot(a, w_ref[...], preferred_element_type=F32)
    h_ref[...] = h
    hn = h * _inv_rms(h, D_MODEL)
    xkv_ref[...] = (hn * lnkv_ref[...]).astype(BF16)
    xb_ref[...] = (hn * lnb_ref[...]).astype(BF16)


def _a_out(o2, sg, x2, w, ln_kv, ln_b, tm):
    m = x2.shape[0]
    row = pl.BlockSpec((tm, D_MODEL), lambda i: (i, 0))
    full = lambda a: pl.BlockSpec(a.shape, lambda i: (0, 0))
    return pl.pallas_call(
        _a_out_kernel,
        grid=(m // tm,),
        in_specs=[row, row, row, full(w), full(ln_kv), full(ln_b)],
        out_specs=[row, row, row],
        out_shape=[jax.ShapeDtypeStruct((m, D_MODEL), F32), jax.ShapeDtypeStruct((m, D_MODEL), BF16),
                   jax.ShapeDtypeStruct((m, D_MODEL), BF16)],
        name="a_out_proj",
        compiler_params=_params(("arbitrary",)),
    )(o2, sg, x2, w, ln_kv, ln_b)


def _branch_proj_kernel(x_ref, w_ref, g_ref, o_ref, y_sc, y2_sc, *, mode, dilation, scale):
    tm, tn = o_ref.shape
    rows = PROJ_ROW_CHUNK // dilation
    for c in range(tm // PROJ_ROW_CHUNK):
        y = jnp.dot(x_ref[c * PROJ_ROW_CHUNK:(c + 1) * PROJ_ROW_CHUNK, :], w_ref[...], preferred_element_type=F32)
        for hc in range(tn // DIL_HEAD_DIM):
            sl = slice(hc * DIL_HEAD_DIM, (hc + 1) * DIL_HEAD_DIM)
            yh = y[:, sl]
            if mode == "norm":
                yh = (yh * _inv_rms(yh, DIL_HEAD_DIM)) * g_ref[:, sl]
                if scale != 1.0:
                    yh = yh * scale
            elif mode == "silu":
                yh = yh * jax.nn.sigmoid(yh)
            if dilation == 1:
                o_ref[c * PROJ_ROW_CHUNK:(c + 1) * PROJ_ROW_CHUNK, sl] = yh.astype(BF16)
            else:
                base = c * PROJ_ROW_CHUNK
                y_sc[hc, base:base + PROJ_ROW_CHUNK, :] = yh
                if dilation == STRIDE_STEP * STRIDE_STEP:
                    quarter = PROJ_ROW_CHUNK // STRIDE_STEP
                    for a in range(STRIDE_STEP):
                        y2_sc[hc, base + a * quarter:base + (a + 1) * quarter, :] = (
                            y_sc[hc, pl.ds(base + a, quarter, stride=STRIDE_STEP), :])
                    for r in range(dilation):
                        a, b = r % STRIDE_STEP, r // STRIDE_STEP
                        dst = r * (tm // dilation) + c * rows
                        o_ref[dst:dst + rows, sl] = (
                            y2_sc[hc, pl.ds(base + a * quarter + b, rows, stride=STRIDE_STEP), :].astype(BF16))
                else:
                    for r in range(dilation):
                        dst = r * (tm // dilation) + c * rows
                        o_ref[dst:dst + rows, sl] = y_sc[hc, pl.ds(base + r, rows, stride=dilation), :].astype(BF16)


def _branch_proj(xn, w, g_cols, col0, mode, dilation, scale, name):
    m, kdim = xn.shape
    tm, tn = DIL_TILE, PROJ_COL_TILE
    blk0 = col0 // tn
    return pl.pallas_call(
        functools.partial(_branch_proj_kernel, mode=mode, dilation=dilation, scale=scale),
        grid=(m // tm, DIL_WIDTH // tn),
        in_specs=[pl.BlockSpec((tm, kdim), lambda i, j: (i, 0)),
                  pl.BlockSpec((kdim, tn), lambda i, j: (0, blk0 + j)),
                  pl.BlockSpec((1, tn), lambda i, j: (0, blk0 + j))],
        out_specs=pl.BlockSpec((tm, tn), lambda i, j: (i, j)),
        out_shape=jax.ShapeDtypeStruct((m, DIL_WIDTH), BF16),
        scratch_shapes=[pltpu.VMEM((tn // DIL_HEAD_DIM, tm, DIL_HEAD_DIM), F32)] * 2,
        name=name,
        compiler_params=_params(("arbitrary", "arbitrary")),
    )(xn, w, g_cols)


def _dilated_attn_kernel(*refs):
    qs, kps, kcs, vps, vcs = (refs[i * N_GROUPS:(i + 1) * N_GROUPS] for i in range(5))
    sg_ref, out_ref, o_sc, lse_sc = refs[5 * N_GROUPS:]
    t = pl.program_id(1)
    c = pl.program_id(2)
    row = lax.broadcasted_iota(jnp.int32, (BAND_BLK, 2 * BAND_BLK), 0)
    col = lax.broadcasted_iota(jnp.int32, (BAND_BLK, 2 * BAND_BLK), 1)
    rel = row + BAND_BLK - col
    band = (rel >= 0) & (rel <= LOOKBACK)
    first_band = band & (col >= jnp.where(t > 0, 0, BAND_BLK))
    relf = rel.astype(F32)
    for g, (_, dil) in enumerate(DIL_GROUPS):
        nblk = DIL_TILE // dil // BAND_BLK
        q_ref, kp_ref, kc_ref, vp_ref, vc_ref = qs[g], kps[g], kcs[g], vps[g], vcs[g]
        for hh in range(DIL_HEAD_CHUNK):
            hs = slice(hh * DIL_HEAD_DIM, (hh + 1) * DIL_HEAD_DIM)
            head = (c * DIL_HEAD_CHUNK + hh + 1).astype(F32)
            slope = jnp.exp2(jnp.full((1, 2 * BAND_BLK), -ALIBI_MAX / DIL_HEADS, F32) * head)
            bias = (slope * (-float(dil) * LOG2E)) * relf
            for r in range(dil):
                for a in range(nblk):
                    q = q_ref[r, a, :, hs]
                    k_lo = kp_ref[r, :, hs] if a == 0 else kc_ref[r, a - 1, :, hs]
                    v_lo = vp_ref[r, :, hs] if a == 0 else vc_ref[r, a - 1, :, hs]
                    k = jnp.concatenate([k_lo, kc_ref[r, a, :, hs]], axis=0)
                    v = jnp.concatenate([v_lo, vc_ref[r, a, :, hs]], axis=0)
                    s = lax.dot_general(q, k, (((1,), (1,)), ((), ())), preferred_element_type=F32)
                    s = jnp.where(first_band if a == 0 else band, s + bias, NEG_BIG)
                    m = jnp.max(s, axis=-1, keepdims=True)
                    p = jnp.exp2(s - m)
                    den = jnp.sum(p, axis=-1, keepdims=True)
                    o = jnp.dot(p.astype(BF16), v, preferred_element_type=F32) / den
                    tok = pl.ds(a * BAND_BLK * dil + r, BAND_BLK, stride=dil) if dil > 1 else pl.ds(a * BAND_BLK, BAND_BLK)
                    o_sc[g, hh, tok, :] = o
                    lse_sc[g, hh, tok, :] = jnp.broadcast_to(m + jnp.log2(den), (BAND_BLK, V7X_LANES))
    for hh in range(DIL_HEAD_CHUNK):
        hs = slice(hh * DIL_HEAD_DIM, (hh + 1) * DIL_HEAD_DIM)
        l0, l1, l2 = lse_sc[0, hh], lse_sc[1, hh], lse_sc[2, hh]
        mx = jnp.maximum(jnp.maximum(l0, l1), l2)
        e0, e1, e2 = jnp.exp2(l0 - mx), jnp.exp2(l1 - mx), jnp.exp2(l2 - mx)
        o = (e0 * o_sc[0, hh] + e1 * o_sc[1, hh] + e2 * o_sc[2, hh]) / (e0 + e1 + e2)
        out_ref[:, hs] = (o * sg_ref[:, hs].astype(F32)).astype(BF16)


def _dilated_attention(qs, ks, vs, gate, batch, seq):
    nt = seq // DIL_TILE
    ins, specs = [], []

    def add(arrs, prev):
        for arr, (_, dil) in zip(arrs, DIL_GROUPS):
            nblk = DIL_TILE // dil // BAND_BLK
            ins.append(arr.reshape(batch, nt, dil, nblk, BAND_BLK, DIL_WIDTH))
            if prev:
                specs.append(pl.BlockSpec((None, None, dil, None, BAND_BLK, DIL_COLS),
                                          lambda b, t, c, nblk=nblk: (b, jnp.maximum(t - 1, 0), 0, nblk - 1, 0, c)))
            else:
                specs.append(pl.BlockSpec((None, None, dil, nblk, BAND_BLK, DIL_COLS),
                                          lambda b, t, c: (b, t, 0, 0, 0, c)))

    add(qs, False)
    add(ks, True)
    add(ks, False)
    add(vs, True)
    add(vs, False)
    ins.append(gate.reshape(batch, nt, DIL_TILE, DIL_WIDTH))
    specs.append(pl.BlockSpec((None, None, DIL_TILE, DIL_COLS), lambda b, t, c: (b, t, 0, c)))
    out = pl.pallas_call(
        _dilated_attn_kernel,
        grid=(batch, nt, DIL_HEADS // DIL_HEAD_CHUNK),
        in_specs=specs,
        out_specs=pl.BlockSpec((None, None, DIL_TILE, DIL_COLS), lambda b, t, c: (b, t, 0, c)),
        out_shape=jax.ShapeDtypeStruct((batch, nt, DIL_TILE, DIL_WIDTH), BF16),
        scratch_shapes=[pltpu.VMEM((N_GROUPS, DIL_HEAD_CHUNK, DIL_TILE, DIL_HEAD_DIM), F32),
                        pltpu.VMEM((N_GROUPS, DIL_HEAD_CHUNK, DIL_TILE, V7X_LANES), F32)],
        name="b_dilated_attention",
        compiler_params=_params(("arbitrary", "arbitrary", "arbitrary")),
    )(*ins)
    return out.reshape(batch * seq, DIL_WIDTH)


def _b_out_kernel(a_ref, h_ref, w_ref, out_ref):
    out_ref[...] = h_ref[...] + jnp.dot(a_ref[...], w_ref[...], preferred_element_type=F32)


def _b_out(a, h1, w, tm):
    m = h1.shape[0]
    row = pl.BlockSpec((tm, D_MODEL), lambda i: (i, 0))
    return pl.pallas_call(
        _b_out_kernel,
        grid=(m // tm,),
        in_specs=[row, row, pl.BlockSpec(w.shape, lambda i: (0, 0))],
        out_specs=row,
        out_shape=jax.ShapeDtypeStruct((m, D_MODEL), F32),
        name="b_out_proj",
        compiler_params=_params(("arbitrary",)),
    )(a, h1, w)


def _tile(n, want):
    t = min(n, want)
    assert n % t == 0, (n, t)
    return t


def kernel(x, a_ln, a_w_in, a_ln_q, a_w_q_up, a_ln_kv, a_w_kv_up, a_q_norm, a_k_norm, a_w_o,
           kv_ln, kv_w, kv_k_norm, b_ln, b_w_in, b_q_norm, b_w_o):
    batch, seq, d = x.shape
    assert d == D_MODEL and a_ln.shape[0] == 1 and b_ln.shape[0] == 1
    assert seq % DIL_TILE == 0
    m = batch * seq
    x2 = x.reshape(m, d)

    lat = Q_LORA + KV_LORA
    w_in = a_w_in[0]
    w_in_pad = jnp.concatenate(
        [w_in[:, :lat], w_in[:, lat + QK_ROPE:], w_in[:, lat:lat + QK_ROPE],
         jnp.zeros((d, V7X_LANES - QK_ROPE), F32)], axis=1).astype(BF16)
    w_q = a_w_q_up[0].reshape(Q_LORA, MLA_HEADS, QK_HEAD)
    w_q_pad = jnp.pad(w_q, ((0, 0), (0, 0), (0, QK_PAD - QK_HEAD))).reshape(Q_LORA, MLA_HEADS * QK_PAD)
    w_q_pad = w_q_pad.astype(BF16)
    w_kv = a_w_kv_up[0].astype(BF16)
    pad_gain = lambda g: jnp.pad(g, (0, QK_PAD - QK_HEAD))[None, :]
    gq_pad, gk_pad = pad_gain(a_q_norm[0]), pad_gain(a_k_norm[0])
    w_o_a = a_w_o[0].astype(BF16)
    w_kvs = kv_w.astype(BF16)
    w_b_in = b_w_in[0].astype(BF16)
    w_o_b = b_w_o[0].astype(BF16)
    k_cols = N_GROUPS * DIL_WIDTH
    gk_cols = jnp.concatenate([jnp.tile(kv_k_norm, (1, DIL_HEADS)).reshape(1, k_cols),
                               jnp.ones((1, k_cols), F32)], axis=1)
    gq_cols = jnp.concatenate([jnp.tile(b_q_norm[0], (1, DIL_HEADS)).reshape(1, k_cols),
                               jnp.ones((1, DIL_WIDTH), F32)], axis=1)

    tabs = _rope_tables(seq)
    cq, ckv, sg_a, kpe = _a_in(x2, a_ln, w_in_pad, a_ln_q, a_ln_kv, _tile(m, 256))
    blk = _tile(seq, 512)
    q = _q_up(cq.reshape(batch, seq, Q_LORA), w_q_pad, gq_pad, tabs, _tile(seq, 256))
    k, vt = _kv_up(ckv.reshape(batch, seq, KV_LORA), kpe.reshape(batch, seq, V7X_LANES), w_kv, gk_pad, tabs, blk)
    o_a = _mla_attention(q, k, vt, blk)
    h1, xn_kv, xn_b = _a_out(o_a.reshape(m, MLA_WIDTH), sg_a, x2, w_o_a, kv_ln[None, :], b_ln, _tile(m, 256))

    q_scale = DIL_HEAD_DIM ** -0.5 * LOG2E
    qs, ks, vs = [], [], []
    for g, (_, dil) in enumerate(DIL_GROUPS):
        ks.append(_branch_proj(xn_kv, w_kvs, gk_cols, g * DIL_WIDTH, "norm", dil, 1.0, f"b_k_proj_d{dil}"))
        vs.append(_branch_proj(xn_kv, w_kvs, gk_cols, k_cols + g * DIL_WIDTH, "copy", dil, 1.0, f"b_v_proj_d{dil}"))
        qs.append(_branch_proj(xn_b, w_b_in, gq_cols, g * DIL_WIDTH, "norm", dil, q_scale, f"b_q_proj_d{dil}"))
    sg_b = _branch_proj(xn_b, w_b_in, gq_cols, k_cols, "silu", 1, 1.0, "b_gate_proj")
    a_b = _dilated_attention(qs, ks, vs, sg_b, batch, seq)
    out = _b_out(a_b, h1, w_o_b, _tile(m, 256))
    return out.reshape(batch, seq, d)
```

```python
import functools
import math

import jax
import jax.numpy as jnp
from jax import lax
from jax.experimental import pallas as pl
from jax.experimental.pallas import tpu as pltpu

F32 = jnp.float32
BF16 = jnp.bfloat16

V7X_LANES = 128
V7X_VMEM_BYTES = 64 * 1024 * 1024
VMEM_LIMIT_BYTES = V7X_VMEM_BYTES * 7 // 8

D_MODEL = 2048
MLA_HEADS = 16
Q_LORA = 512
KV_LORA = 512
QK_NOPE = 128
QK_ROPE = 64
QK_HEAD = QK_NOPE + QK_ROPE
V_HEAD = 128
MLA_WIDTH = MLA_HEADS * V_HEAD
ROPE_THETA = 10000.0
DIL_GROUPS = ((128, 1), (512, 4), (2048, 16))
N_GROUPS = 3
DIL_HEADS = 16
DIL_HEAD_DIM = 128
DIL_WIDTH = DIL_HEADS * DIL_HEAD_DIM
LOOKBACK = 128
BAND_BLK = 128
ALIBI_MAX = 8.0
EPS = 1e-6
NEG_BIG = -1e30
LOG2E = math.log2(math.e)

QK_PAD = 2 * V7X_LANES
A_IN_PAD = Q_LORA + KV_LORA + MLA_WIDTH + V7X_LANES

DIL_TILE = BAND_BLK * DIL_GROUPS[-1][1]
DIL_HEAD_CHUNK = 2
DIL_COLS = DIL_HEAD_CHUNK * DIL_HEAD_DIM
VT_ROWS = V_HEAD + 16
LAG_LIMIT = 64.0
STRIDE_STEP = 4
PROJ_ROW_CHUNK = 256
PROJ_COL_TILE = 512


def _params(semantics):
    return pltpu.CompilerParams(dimension_semantics=semantics, vmem_limit_bytes=VMEM_LIMIT_BYTES)


def _inv_rms(x, n):
    return lax.rsqrt(jnp.sum(x * x, axis=-1, keepdims=True) * (1.0 / n) + EPS)


def _rope(rv, cos_t, s1_t, s2_t):
    return rv * cos_t + pltpu.roll(rv, 96, 1) * s1_t + pltpu.roll(rv, 32, 1) * s2_t


def _rope_table_kernel(inv_ref, cos_ref, s1_ref, s2_ref):
    ts = cos_ref.shape[0]
    row = lax.broadcasted_iota(jnp.int32, (ts, V7X_LANES), 0) + pl.program_id(0) * ts
    lane = lax.broadcasted_iota(jnp.int32, (ts, V7X_LANES), 1)
    ang = row.astype(F32) * inv_ref[...]
    c = jnp.cos(ang)
    s = jnp.sin(ang)
    half = QK_ROPE // 2
    cos_ref[...] = jnp.where(lane < QK_ROPE, c, 0.0)
    s1_ref[...] = jnp.where(lane < half, -s, 0.0)
    s2_ref[...] = jnp.where((lane >= half) & (lane < QK_ROPE), s, 0.0)


def _rope_tables(seq):
    half = QK_ROPE // 2
    inv = ROPE_THETA ** (-jnp.arange(half, dtype=F32) / half)
    inv_lanes = jnp.concatenate([inv, inv, jnp.zeros((V7X_LANES - QK_ROPE,), F32)])[None, :]
    ts = min(seq, 1024)
    tab = jax.ShapeDtypeStruct((seq, V7X_LANES), F32)
    spec = pl.BlockSpec((ts, V7X_LANES), lambda i: (i, 0))
    return pl.pallas_call(
        _rope_table_kernel,
        grid=(seq // ts,),
        in_specs=[pl.BlockSpec((1, V7X_LANES), lambda i: (0, 0))],
        out_specs=[spec, spec, spec],
        out_shape=[tab, tab, tab],
        name="rope_tables",
        compiler_params=_params(("arbitrary",)),
    )(inv_lanes)


def _a_in_kernel(x_ref, ln_ref, w_ref, lnq_ref, lnkv_ref, cq_ref, ckv_ref, sg_ref, kpe_ref):
    x = x_ref[...]
    xn = ((x * _inv_rms(x, D_MODEL)) * ln_ref[...]).astype(BF16)
    u = jnp.dot(xn, w_ref[...], preferred_element_type=F32)
    cq = u[:, :Q_LORA]
    cq_ref[...] = ((cq * _inv_rms(cq, Q_LORA)) * lnq_ref[...]).astype(BF16)
    ckv = u[:, Q_LORA:Q_LORA + KV_LORA]
    ckv_ref[...] = ((ckv * _inv_rms(ckv, KV_LORA)) * lnkv_ref[...]).astype(BF16)
    g0 = Q_LORA + KV_LORA
    gate = u[:, g0:g0 + MLA_WIDTH]
    sg_ref[...] = (gate * jax.nn.sigmoid(gate)).astype(BF16)
    kpe_ref[...] = u[:, g0 + MLA_WIDTH:]


def _a_in(x2, ln, w_pad, ln_q, ln_kv, tm):
    m = x2.shape[0]
    row = lambda n: pl.BlockSpec((tm, n), lambda i: (i, 0))
    full = lambda a: pl.BlockSpec(a.shape, lambda i: (0, 0))
    return pl.pallas_call(
        _a_in_kernel,
        grid=(m // tm,),
        in_specs=[row(D_MODEL), full(ln), full(w_pad), full(ln_q), full(ln_kv)],
        out_specs=[row(Q_LORA), row(KV_LORA), row(MLA_WIDTH), row(V7X_LANES)],
        out_shape=[jax.ShapeDtypeStruct((m, Q_LORA), BF16), jax.ShapeDtypeStruct((m, KV_LORA), BF16),
                   jax.ShapeDtypeStruct((m, MLA_WIDTH), BF16), jax.ShapeDtypeStruct((m, V7X_LANES), F32)],
        name="a_in_proj",
        compiler_params=_params(("arbitrary",)),
    )(x2, ln, w_pad, ln_q, ln_kv)


def _q_up_kernel(cq_ref, w_ref, g_ref, cos_ref, s1_ref, s2_ref, q_ref, *, scale):
    y = jnp.dot(cq_ref[...], w_ref[...], preferred_element_type=F32)
    g_n = g_ref[:, :QK_NOPE]
    g_r = g_ref[:, QK_NOPE:]
    cos_t, s1_t, s2_t = cos_ref[...], s1_ref[...], s2_ref[...]
    for h in range(MLA_HEADS):
        nope = y[:, h * QK_PAD:h * QK_PAD + QK_NOPE]
        rv = y[:, h * QK_PAD + QK_NOPE:(h + 1) * QK_PAD]
        ss = jnp.sum(nope * nope, axis=-1, keepdims=True) + jnp.sum(rv * rv, axis=-1, keepdims=True)
        r = lax.rsqrt(ss * (1.0 / QK_HEAD) + EPS)
        q_ref[h, :QK_NOPE, :] = (((nope * r) * g_n) * scale).T.astype(BF16)
        q_ref[h, QK_NOPE:, :] = (_rope((rv * r) * g_r, cos_t, s1_t, s2_t) * scale).T.astype(BF16)


def _kv_up_kernel(ckv_ref, kpe_ref, w_ref, g_ref, cos_ref, s1_ref, s2_ref, k_ref, vt_ref):
    y = jnp.dot(ckv_ref[...], w_ref[...], preferred_element_type=F32)
    g_n = g_ref[:, :QK_NOPE]
    g_r = g_ref[:, QK_NOPE:]
    kpe = kpe_ref[...]
    ss_pe = jnp.sum(kpe * kpe, axis=-1, keepdims=True)
    cos_t, s1_t, s2_t = cos_ref[...], s1_ref[...], s2_ref[...]
    hw = QK_NOPE + V_HEAD
    pad_row = lax.broadcasted_iota(jnp.int32, (VT_ROWS - V_HEAD, y.shape[0]), 0)
    ones_rows = jnp.where(pad_row == 0, 1.0, 0.0).astype(BF16)
    for h in range(MLA_HEADS):
        kn = y[:, h * hw:h * hw + QK_NOPE]
        r = lax.rsqrt((jnp.sum(kn * kn, axis=-1, keepdims=True) + ss_pe) * (1.0 / QK_HEAD) + EPS)
        k_ref[h, :, :QK_NOPE] = ((kn * r) * g_n).astype(BF16)
        k_ref[h, :, QK_NOPE:] = _rope((kpe * r) * g_r, cos_t, s1_t, s2_t).astype(BF16)
        vt_ref[h, :V_HEAD, :] = y[:, h * hw + QK_NOPE:(h + 1) * hw].T.astype(BF16)
        vt_ref[h, V_HEAD:, :] = ones_rows


def _seq_spec(tm, n):
    return pl.BlockSpec((None, tm, n), lambda b, i: (b, i, 0))


def _tab_spec(tm):
    return pl.BlockSpec((tm, V7X_LANES), lambda b, i: (i, 0))


def _head_spec(tm, n):
    return pl.BlockSpec((None, MLA_HEADS, tm, n), lambda b, i: (b, 0, i, 0))


def _q_up(cq3, w_pad, g_pad, tabs, tm):
    b, s, _ = cq3.shape
    full = lambda a: pl.BlockSpec(a.shape, lambda bb, i: (0, 0))
    scale = QK_HEAD ** -0.5 * LOG2E
    return pl.pallas_call(
        functools.partial(_q_up_kernel, scale=scale),
        grid=(b, s // tm),
        in_specs=[_seq_spec(tm, Q_LORA), full(w_pad), full(g_pad), _tab_spec(tm), _tab_spec(tm), _tab_spec(tm)],
        out_specs=pl.BlockSpec((None, MLA_HEADS, QK_PAD, tm), lambda bb, i: (bb, 0, 0, i)),
        out_shape=jax.ShapeDtypeStruct((b, MLA_HEADS, QK_PAD, s), BF16),
        name="a_q_up",
        compiler_params=_params(("arbitrary", "arbitrary")),
    )(cq3, w_pad, g_pad, *tabs)


def _kv_up(ckv3, kpe3, w, g_pad, tabs, tm):
    b, s, _ = ckv3.shape
    full = lambda a: pl.BlockSpec(a.shape, lambda bb, i: (0, 0))
    return pl.pallas_call(
        _kv_up_kernel,
        grid=(b, s // tm),
        in_specs=[_seq_spec(tm, KV_LORA), _seq_spec(tm, V7X_LANES), full(w), full(g_pad),
                  _tab_spec(tm), _tab_spec(tm), _tab_spec(tm)],
        out_specs=[_head_spec(tm, QK_PAD),
                   pl.BlockSpec((None, MLA_HEADS, None, VT_ROWS, tm), lambda bb, i: (bb, 0, i, 0, 0))],
        out_shape=[jax.ShapeDtypeStruct((b, MLA_HEADS, s, QK_PAD), BF16),
                   jax.ShapeDtypeStruct((b, MLA_HEADS, s // tm, VT_ROWS, tm), BF16)],
        name="a_kv_up",
        compiler_params=_params(("arbitrary", "arbitrary")),
    )(ckv3, kpe3, w, g_pad, *tabs)


def _mla_attn_kernel(q_ref, k_ref, vt_ref, o_ref, sig_sc, viol_sc, acc_sc, *, tq, tk):
    qi = pl.program_id(2)
    qt = q_ref[...]
    n_chunks = 2 * qi + 2

    def qk(kc):
        k = k_ref[pl.ds(pl.multiple_of(kc * tk, tk), tk), :]
        return jnp.dot(k, qt, preferred_element_type=F32)

    def causal(s, key_offset):
        key = lax.broadcasted_iota(jnp.int32, (tk, tq), 0) + key_offset
        qry = lax.broadcasted_iota(jnp.int32, (tk, tq), 1)
        return jnp.where(key <= qry, s, NEG_BIG)

    def finish():
        o_ref[...] = (acc_sc[:V_HEAD, :] / acc_sc[V_HEAD:V_HEAD + 1, :]).T.astype(BF16)

    def lagged_chunk(kc, key_offset):
        s = qk(kc)
        if key_offset is not None:
            s = causal(s, key_offset)
        shift = sig_sc[...]
        mx = jnp.max(s, axis=0, keepdims=True)
        p = jnp.exp2(s - shift).astype(BF16)
        new_shift = jnp.maximum(shift, mx)
        pv = jnp.dot(vt_ref[kc], p, preferred_element_type=F32)
        acc_sc[...] = (acc_sc[...] + pv) * jnp.exp2(shift - new_shift)
        sig_sc[...] = new_shift
        viol_sc[...] = jnp.maximum(viol_sc[...], mx - shift)

    sig_sc[...] = jnp.max(causal(qk(0), -qi * tq), axis=0, keepdims=True)
    viol_sc[...] = jnp.zeros(viol_sc.shape, F32)
    acc_sc[...] = jnp.zeros(acc_sc.shape, F32)

    def quad(qd, carry):
        for u in range(4):
            lagged_chunk(4 * qd + u, None)
        return carry

    lax.fori_loop(0, lax.shift_right_logical(qi, 1), quad, 0)

    @pl.when(qi % 2 == 1)
    def _():
        lagged_chunk(2 * qi - 2, None)
        lagged_chunk(2 * qi - 1, None)

    lagged_chunk(2 * qi, 0)
    lagged_chunk(2 * qi + 1, tk)
    finish()

    @pl.when(jnp.max(viol_sc[...]) > LAG_LIMIT)
    def _():
        sig_sc[...] = jnp.full(sig_sc.shape, NEG_BIG, F32)
        acc_sc[...] = jnp.zeros(acc_sc.shape, F32)

        def exact_chunk(kc, carry):
            s = causal(qk(kc), kc * tk - qi * tq)
            m_old = sig_sc[...]
            m_new = jnp.maximum(m_old, jnp.max(s, axis=0, keepdims=True))
            p = jnp.exp2(s - m_new).astype(BF16)
            acc_sc[...] = jnp.exp2(m_old - m_new) * acc_sc[...] + jnp.dot(vt_ref[kc], p, preferred_element_type=F32)
            sig_sc[...] = m_new
            return carry

        lax.fori_loop(0, n_chunks, exact_chunk, 0)
        finish()


def _mla_attention(qt, k, vt, tk):
    b, h, s, _ = k.shape
    tq = 2 * tk
    assert vt.shape[-1] == tk and s % tq == 0
    return pl.pallas_call(
        functools.partial(_mla_attn_kernel, tq=tq, tk=tk),
        grid=(b, h, s // tq),
        in_specs=[pl.BlockSpec((None, None, QK_PAD, tq), lambda bb, hh, i: (bb, hh, 0, i)),
                  pl.BlockSpec((None, None, s, QK_PAD), lambda bb, hh, i: (bb, hh, 0, 0)),
                  pl.BlockSpec((None, None, s // tk, VT_ROWS, tk), lambda bb, hh, i: (bb, hh, 0, 0, 0))],
        out_specs=pl.BlockSpec((None, tq, V_HEAD), lambda bb, hh, i: (bb, i, hh)),
        out_shape=jax.ShapeDtypeStruct((b, s, MLA_WIDTH), BF16),
        scratch_shapes=[pltpu.VMEM((1, tq), F32), pltpu.VMEM((1, tq), F32), pltpu.VMEM((VT_ROWS, tq), F32)],
        name="a_mla_attention",
        compiler_params=_params(("arbitrary", "arbitrary", "arbitrary")),
    )(qt, k, vt)


def _a_out_kernel(o_ref, sg_ref, x_ref, w_ref, lnkv_ref, lnb_ref, h_ref, xkv_ref, xb_ref):
    a = (o_ref[...].astype(F32) * sg_ref[...].astype(F32)).astype(BF16)
    h = x_ref[...] + jnp.dot(a, w_ref[...], preferred_element_type=F32)
    h_ref[...] = h
    hn = h * _inv_rms(h, D_MODEL)
    xkv_ref[...] = (hn * lnkv_ref[...]).astype(BF16)
    xb_ref[...] = (hn * lnb_ref[...]).astype(BF16)


def _a_out(o2, sg, x2, w, ln_kv, ln_b, tm):
    m = x2.shape[0]
    row = pl.BlockSpec((tm, D_MODEL), lambda i: (i, 0))
    full = lambda a: pl.BlockSpec(a.shape, lambda i: (0, 0))
    return pl.pallas_call(
        _a_out_kernel,
        grid=(m // tm,),
        in_specs=[row, row, row, full(w), full(ln_kv), full(ln_b)],
        out_specs=[row, row, row],
        out_shape=[jax.ShapeDtypeStruct((m, D_MODEL), F32), jax.ShapeDtypeStruct((m, D_MODEL), BF16),
                   jax.ShapeDtypeStruct((m, D_MODEL), BF16)],
        name="a_out_proj",
        compiler_params=_params(("arbitrary",)),
    )(o2, sg, x2, w, ln_kv, ln_b)


def _branch_proj_kernel(x_ref, w_ref, g_ref, o_ref, y_sc, y2_sc, *, mode, dilation, scale):
    tm, tn = o_ref.shape
    rows = PROJ_ROW_CHUNK // dilation
    for c in range(tm // PROJ_ROW_CHUNK):
        y = jnp.dot(x_ref[c * PROJ_ROW_CHUNK:(c + 1) * PROJ_ROW_CHUNK, :], w_ref[...], preferred_element_type=F32)
        for hc in range(tn // DIL_HEAD_DIM):
            sl = slice(hc * DIL_HEAD_DIM, (hc + 1) * DIL_HEAD_DIM)
            yh = y[:, sl]
            if mode == "norm":
                yh = (yh * _inv_rms(yh, DIL_HEAD_DIM)) * g_ref[:, sl]
                if scale != 1.0:
                    yh = yh * scale
            elif mode == "silu":
                yh = yh * jax.nn.sigmoid(yh)
            if dilation == 1:
                o_ref[c * PROJ_ROW_CHUNK:(c + 1) * PROJ_ROW_CHUNK, sl] = yh.astype(BF16)
            else:
                base = c * PROJ_ROW_CHUNK
                y_sc[hc, base:base + PROJ_ROW_CHUNK, :] = yh
                if dilation == STRIDE_STEP * STRIDE_STEP:
                    quarter = PROJ_ROW_CHUNK // STRIDE_STEP
                    for a in range(STRIDE_STEP):
                        y2_sc[hc, base + a * quarter:base + (a + 1) * quarter, :] = (
                            y_sc[hc, pl.ds(base + a, quarter, stride=STRIDE_STEP), :])
                    for r in range(dilation):
                        a, b = r % STRIDE_STEP, r // STRIDE_STEP
                        dst = r * (tm // dilation) + c * rows
                        o_ref[dst:dst + rows, sl] = (
                            y2_sc[hc, pl.ds(base + a * quarter + b, rows, stride=STRIDE_STEP), :].astype(BF16))
                else:
                    for r in range(dilation):
                        dst = r * (tm // dilation) + c * rows
                        o_ref[dst:dst + rows, sl] = y_sc[hc, pl.ds(base + r, rows, stride=dilation), :].astype(BF16)


def _branch_proj(xn, w, g_cols, col0, mode, dilation, scale, name):
    m, kdim = xn.shape
    tm, tn = DIL_TILE, PROJ_COL_TILE
    blk0 = col0 // tn
    return pl.pallas_call(
        functools.partial(_branch_proj_kernel, mode=mode, dilation=dilation, scale=scale),
        grid=(m // tm, DIL_WIDTH // tn),
        in_specs=[pl.BlockSpec((tm, kdim), lambda i, j: (i, 0)),
                  pl.BlockSpec((kdim, tn), lambda i, j: (0, blk0 + j)),
                  pl.BlockSpec((1, tn), lambda i, j: (0, blk0 + j))],
        out_specs=pl.BlockSpec((tm, tn), lambda i, j: (i, j)),
        out_shape=jax.ShapeDtypeStruct((m, DIL_WIDTH), BF16),
        scratch_shapes=[pltpu.VMEM((tn // DIL_HEAD_DIM, tm, DIL_HEAD_DIM), F32)] * 2,
        name=name,
        compiler_params=_params(("arbitrary", "arbitrary")),
    )(xn, w, g_cols)


def _dilated_attn_kernel(*refs):
    qs, kps, kcs, vps, vcs = (refs[i * N_GROUPS:(i + 1) * N_GROUPS] for i in range(5))
    sg_ref, out_ref, o_sc, lse_sc = refs[5 * N_GROUPS:]
    t = pl.program_id(1)
    c = pl.program_id(2)
    row = lax.broadcasted_iota(jnp.int32, (BAND_BLK, 2 * BAND_BLK), 0)
    col = lax.broadcasted_iota(jnp.int32, (BAND_BLK, 2 * BAND_BLK), 1)
    rel = row + BAND_BLK - col
    band = (rel >= 0) & (rel <= LOOKBACK)
    first_band = band & (col >= jnp.where(t > 0, 0, BAND_BLK))
    relf = rel.astype(F32)
    for g, (_, dil) in enumerate(DIL_GROUPS):
        nblk = DIL_TILE // dil // BAND_BLK
        q_ref, kp_ref, kc_ref, vp_ref, vc_ref = qs[g], kps[g], kcs[g], vps[g], vcs[g]
        for hh in range(DIL_HEAD_CHUNK):
            hs = slice(hh * DIL_HEAD_DIM, (hh + 1) * DIL_HEAD_DIM)
            head = (c * DIL_HEAD_CHUNK + hh + 1).astype(F32)
            slope = jnp.exp2(jnp.full((1, 2 * BAND_BLK), -ALIBI_MAX / DIL_HEADS, F32) * head)
            bias = (slope * (-float(dil) * LOG2E)) * relf
            for r in range(dil):
                for a in range(nblk):
                    q = q_ref[r, a, :, hs]
                    k_lo = kp_ref[r, :, hs] if a == 0 else kc_ref[r, a - 1, :, hs]
                    v_lo = vp_ref[r, :, hs] if a == 0 else vc_ref[r, a - 1, :, hs]
                    k = jnp.concatenate([k_lo, kc_ref[r, a, :, hs]], axis=0)
                    v = jnp.concatenate([v_lo, vc_ref[r, a, :, hs]], axis=0)
                    s = lax.dot_general(q, k, (((1,), (1,)), ((), ())), preferred_element_type=F32)
                    s = jnp.where(first_band if a == 0 else band, s + bias, NEG_BIG)
                    m = jnp.max(s, axis=-1, keepdims=True)
                    p = jnp.exp2(s - m)
                    den = jnp.sum(p, axis=-1, keepdims=True)
                    o = jnp.dot(p.astype(BF16), v, preferred_element_type=F32) / den
                    tok = pl.ds(a * BAND_BLK * dil + r, BAND_BLK, stride=dil) if dil > 1 else pl.ds(a * BAND_BLK, BAND_BLK)
                    o_sc[g, hh, tok, :] = o
                    lse_sc[g, hh, tok, :] = jnp.broadcast_to(m + jnp.log2(den), (BAND_BLK, V7X_LANES))
    for hh in range(DIL_HEAD_CHUNK):
        hs = slice(hh * DIL_HEAD_DIM, (hh + 1) * DIL_HEAD_DIM)
        l0, l1, l2 = lse_sc[0, hh], lse_sc[1, hh], lse_sc[2, hh]
        mx = jnp.maximum(jnp.maximum(l0, l1), l2)
        e0, e1, e2 = jnp.exp2(l0 - mx), jnp.exp2(l1 - mx), jnp.exp2(l2 - mx)
        o = (e0 * o_sc[0, hh] + e1 * o_sc[1, hh] + e2 * o_sc[2, hh]) / (e0 + e1 + e2)
        out_ref[:, hs] = (o * sg_ref[:, hs].astype(F32)).astype(BF16)


def _dilated_attention(qs, ks, vs, gate, batch, seq):
    nt = seq // DIL_TILE
    ins, specs = [], []

    def add(arrs, prev):
        for arr, (_, dil) in zip(arrs, DIL_GROUPS):
            nblk = DIL_TILE // dil // BAND_BLK
            ins.append(arr.reshape(batch, nt, dil, nblk, BAND_BLK, DIL_WIDTH))
            if prev:
                specs.append(pl.BlockSpec((None, None, dil, None, BAND_BLK, DIL_COLS),
                                          lambda b, t, c, nblk=nblk: (b, jnp.maximum(t - 1, 0), 0, nblk - 1, 0, c)))
            else:
                specs.append(pl.BlockSpec((None, None, dil, nblk, BAND_BLK, DIL_COLS),
                                          lambda b, t, c: (b, t, 0, 0, 0, c)))

    add(qs, False)
    add(ks, True)
    add(ks, False)
    add(vs, True)
    add(vs, False)
    ins.append(gate.reshape(batch, nt, DIL_TILE, DIL_WIDTH))
    specs.append(pl.BlockSpec((None, None, DIL_TILE, DIL_COLS), lambda b, t, c: (b, t, 0, c)))
    out = pl.pallas_call(
        _dilated_attn_kernel,
        grid=(batch, nt, DIL_HEADS // DIL_HEAD_CHUNK),
        in_specs=specs,
        out_specs=pl.BlockSpec((None, None, DIL_TILE, DIL_COLS), lambda b, t, c: (b, t, 0, c)),
        out_shape=jax.ShapeDtypeStruct((batch, nt, DIL_TILE, DIL_WIDTH), BF16),
        scratch_shapes=[pltpu.VMEM((N_GROUPS, DIL_HEAD_CHUNK, DIL_TILE, DIL_HEAD_DIM), F32),
                        pltpu.VMEM((N_GROUPS, DIL_HEAD_CHUNK, DIL_TILE, V7X_LANES), F32)],
        name="b_dilated_attention",
        compiler_params=_params(("arbitrary", "arbitrary", "arbitrary")),
    )(*ins)
    return out.reshape(batch * seq, DIL_WIDTH)


def _b_out_kernel(a_ref, h_ref, w_ref, out_ref):
    out_ref[...] = h_ref[...] + jnp.dot(a_ref[...], w_ref[...], preferred_element_type=F32)


def _b_out(a, h1, w, tm):
    m = h1.shape[0]
    row = pl.BlockSpec((tm, D_MODEL), lambda i: (i, 0))
    return pl.pallas_call(
        _b_out_kernel,
        grid=(m // tm,),
        in_specs=[row, row, pl.BlockSpec(w.shape, lambda i: (0, 0))],
        out_specs=row,
        out_shape=jax.ShapeDtypeStruct((m, D_MODEL), F32),
        name="b_out_proj",
        compiler_params=_params(("arbitrary",)),
    )(a, h1, w)


def _tile(n, want):
    t = min(n, want)
    assert n % t == 0, (n, t)
    return t


def kernel(x, a_ln, a_w_in, a_ln_q, a_w_q_up, a_ln_kv, a_w_kv_up, a_q_norm, a_k_norm, a_w_o,
           kv_ln, kv_w, kv_k_norm, b_ln, b_w_in, b_q_norm, b_w_o):
    batch, seq, d = x.shape
    assert d == D_MODEL and a_ln.shape[0] == 1 and b_ln.shape[0] == 1
    assert seq % DIL_TILE == 0
    m = batch * seq
    x2 = x.reshape(m, d)

    lat = Q_LORA + KV_LORA
    w_in = a_w_in[0]
    w_in_pad = jnp.concatenate(
        [w_in[:, :lat], w_in[:, lat + QK_ROPE:], w_in[:, lat:lat + QK_ROPE],
         jnp.zeros((d, V7X_LANES - QK_ROPE), F32)], axis=1).astype(BF16)
    w_q = a_w_q_up[0].reshape(Q_LORA, MLA_HEADS, QK_HEAD)
    w_q_pad = jnp.pad(w_q, ((0, 0), (0, 0), (0, QK_PAD - QK_HEAD))).reshape(Q_LORA, MLA_HEADS * QK_PAD)
    w_q_pad = w_q_pad.astype(BF16)
    w_kv = a_w_kv_up[0].astype(BF16)
    pad_gain = lambda g: jnp.pad(g, (0, QK_PAD - QK_HEAD))[None, :]
    gq_pad, gk_pad = pad_gain(a_q_norm[0]), pad_gain(a_k_norm[0])
    w_o_a = a_w_o[0].astype(BF16)
    w_kvs = kv_w.astype(BF16)
    w_b_in = b_w_in[0].astype(BF16)
    w_o_b = b_w_o[0].astype(BF16)
    k_cols = N_GROUPS * DIL_WIDTH
    gk_cols = jnp.concatenate([jnp.tile(kv_k_norm, (1, DIL_HEADS)).reshape(1, k_cols),
                               jnp.ones((1, k_cols), F32)], axis=1)
    gq_cols = jnp.concatenate([jnp.tile(b_q_norm[0], (1, DIL_HEADS)).reshape(1, k_cols),
                               jnp.ones((1, DIL_WIDTH), F32)], axis=1)

    tabs = _rope_tables(seq)
    cq, ckv, sg_a, kpe = _a_in(x2, a_ln, w_in_pad, a_ln_q, a_ln_kv, _tile(m, 256))
    blk = _tile(seq, 512)
    q = _q_up(cq.reshape(batch, seq, Q_LORA), w_q_pad, gq_pad, tabs, _tile(seq, 256))
    k, vt = _kv_up(ckv.reshape(batch, seq, KV_LORA), kpe.reshape(batch, seq, V7X_LANES), w_kv, gk_pad, tabs, blk)
    o_a = _mla_attention(q, k, vt, blk)
    h1, xn_kv, xn_b = _a_out(o_a.reshape(m, MLA_WIDTH), sg_a, x2, w_o_a, kv_ln[None, :], b_ln, _tile(m, 256))

    q_scale = DIL_HEAD_DIM ** -0.5 * LOG2E
    qs, ks, vs = [], [], []
    for g, (_, dil) in enumerate(DIL_GROUPS):
        ks.append(_branch_proj(xn_kv, w_kvs, gk_cols, g * DIL_WIDTH, "norm", dil, 1.0, f"b_k_proj_d{dil}"))
        vs.append(_branch_proj(xn_kv, w_kvs, gk_cols, k_cols + g * DIL_WIDTH, "copy", dil, 1.0, f"b_v_proj_d{dil}"))
        qs.append(_branch_proj(xn_b, w_b_in, gq_cols, g * DIL_WIDTH, "norm", dil, q_scale, f"b_q_proj_d{dil}"))
    sg_b = _branch_proj(xn_b, w_b_in, gq_cols, k_cols, "silu", 1, 1.0, "b_gate_proj")
    a_b = _dilated_attention(qs, ks, vs, sg_b, batch, seq)
    out = _b_out(a_b, h1, w_o_b, _tile(m, 256))
    return out.reshape(batch, seq, d)
```

```python
import functools
import math

import jax
import jax.numpy as jnp
from jax import lax
from jax.experimental import pallas as pl
from jax.experimental.pallas import tpu as pltpu

F32 = jnp.float32
BF16 = jnp.bfloat16

V7X_LANES = 128
V7X_VMEM_BYTES = 64 * 1024 * 1024
VMEM_LIMIT_BYTES = V7X_VMEM_BYTES * 7 // 8

D_MODEL = 2048
MLA_HEADS = 16
Q_LORA = 512
KV_LORA = 512
QK_NOPE = 128
QK_ROPE = 64
QK_HEAD = QK_NOPE + QK_ROPE
V_HEAD = 128
MLA_WIDTH = MLA_HEADS * V_HEAD
ROPE_THETA = 10000.0
DIL_GROUPS = ((128, 1), (512, 4), (2048, 16))
N_GROUPS = 3
DIL_HEADS = 16
DIL_HEAD_DIM = 128
DIL_WIDTH = DIL_HEADS * DIL_HEAD_DIM
LOOKBACK = 128
BAND_BLK = 128
ALIBI_MAX = 8.0
EPS = 1e-6
NEG_BIG = -1e30
LOG2E = math.log2(math.e)

QK_PAD = 2 * V7X_LANES
A_IN_PAD = Q_LORA + KV_LORA + MLA_WIDTH + V7X_LANES

DIL_TILE = BAND_BLK * DIL_GROUPS[-1][1]
DIL_HEAD_CHUNK = 2
DIL_COLS = DIL_HEAD_CHUNK * DIL_HEAD_DIM
VT_ROWS = V_HEAD + 16
FIRST_SHIFT_KEYS = 16
LAG_LIMIT = 64.0
STRIDE_STEP = 4
PROJ_ROW_CHUNK = 256
PROJ_COL_TILE = 512


def _params(semantics):
    return pltpu.CompilerParams(dimension_semantics=semantics, vmem_limit_bytes=VMEM_LIMIT_BYTES)


def _inv_rms(x, n):
    return lax.rsqrt(jnp.sum(x * x, axis=-1, keepdims=True) * (1.0 / n) + EPS)


def _rope(rv, cos_t, s1_t, s2_t):
    return rv * cos_t + pltpu.roll(rv, 96, 1) * s1_t + pltpu.roll(rv, 32, 1) * s2_t


def _rope_table_kernel(inv_ref, invt_ref, cos_ref, s1_ref, s2_ref, cost_ref, sint_ref):
    ts = cos_ref.shape[0]
    row = lax.broadcasted_iota(jnp.int32, (ts, V7X_LANES), 0) + pl.program_id(0) * ts
    lane = lax.broadcasted_iota(jnp.int32, (ts, V7X_LANES), 1)
    ang = row.astype(F32) * inv_ref[...]
    c = jnp.cos(ang)
    s = jnp.sin(ang)
    half = QK_ROPE // 2
    cos_ref[...] = jnp.where(lane < QK_ROPE, c, 0.0)
    s1_ref[...] = jnp.where(lane < half, -s, 0.0)
    s2_ref[...] = jnp.where((lane >= half) & (lane < QK_ROPE), s, 0.0)
    pos = lax.broadcasted_iota(jnp.int32, (half, ts), 1) + pl.program_id(0) * ts
    ang_t = pos.astype(F32) * jnp.tile(invt_ref[...], (1, ts // V7X_LANES))
    cost_ref[...] = jnp.cos(ang_t)
    sint_ref[...] = jnp.sin(ang_t)


def _rope_tables(seq):
    half = QK_ROPE // 2
    inv = ROPE_THETA ** (-jnp.arange(half, dtype=F32) / half)
    inv_lanes = jnp.concatenate([inv, inv, jnp.zeros((V7X_LANES - QK_ROPE,), F32)])[None, :]
    inv_rows = jnp.broadcast_to(inv[:, None], (half, V7X_LANES))
    ts = min(seq, 1024)
    tab = jax.ShapeDtypeStruct((seq, V7X_LANES), F32)
    tab_t = jax.ShapeDtypeStruct((half, seq), F32)
    spec = pl.BlockSpec((ts, V7X_LANES), lambda i: (i, 0))
    spec_t = pl.BlockSpec((half, ts), lambda i: (0, i))
    outs = pl.pallas_call(
        _rope_table_kernel,
        grid=(seq // ts,),
        in_specs=[pl.BlockSpec((1, V7X_LANES), lambda i: (0, 0)), pl.BlockSpec((half, V7X_LANES), lambda i: (0, 0))],
        out_specs=[spec, spec, spec, spec_t, spec_t],
        out_shape=[tab, tab, tab, tab_t, tab_t],
        name="rope_tables",
        compiler_params=_params(("arbitrary",)),
    )(inv_lanes, inv_rows)
    return outs[:3], outs[3:]


def _a_in_kernel(x_ref, ln_ref, w_ref, lnq_ref, lnkv_ref, cq_ref, ckv_ref, sg_ref, kpe_ref):
    x = x_ref[...]
    xn = ((x * _inv_rms(x, D_MODEL)) * ln_ref[...]).astype(BF16)
    u = jnp.dot(xn, w_ref[...], preferred_element_type=F32)
    cq = u[:, :Q_LORA]
    cq_ref[...] = ((cq * _inv_rms(cq, Q_LORA)) * lnq_ref[...]).T.astype(BF16)
    ckv = u[:, Q_LORA:Q_LORA + KV_LORA]
    ckv_ref[...] = ((ckv * _inv_rms(ckv, KV_LORA)) * lnkv_ref[...]).astype(BF16)
    g0 = Q_LORA + KV_LORA
    gate = u[:, g0:g0 + MLA_WIDTH]
    sg_ref[...] = (gate * jax.nn.sigmoid(gate)).astype(BF16)
    kpe_ref[...] = u[:, g0 + MLA_WIDTH:]


def _a_in(x2, ln, w_pad, ln_q, ln_kv, tm):
    m = x2.shape[0]
    row = lambda n: pl.BlockSpec((tm, n), lambda i: (i, 0))
    full = lambda a: pl.BlockSpec(a.shape, lambda i: (0, 0))
    return pl.pallas_call(
        _a_in_kernel,
        grid=(m // tm,),
        in_specs=[row(D_MODEL), full(ln), full(w_pad), full(ln_q), full(ln_kv)],
        out_specs=[pl.BlockSpec((Q_LORA, tm), lambda i: (0, i)), row(KV_LORA), row(MLA_WIDTH), row(V7X_LANES)],
        out_shape=[jax.ShapeDtypeStruct((Q_LORA, m), BF16), jax.ShapeDtypeStruct((m, KV_LORA), BF16),
                   jax.ShapeDtypeStruct((m, MLA_WIDTH), BF16), jax.ShapeDtypeStruct((m, V7X_LANES), F32)],
        name="a_in_proj",
        compiler_params=_params(("arbitrary",)),
    )(x2, ln, w_pad, ln_q, ln_kv)


def _q_up_kernel(cqt_ref, wt_ref, g_ref, cos_ref, sin_ref, q_ref, *, scale):
    yt = jnp.dot(wt_ref[...], cqt_ref[...], preferred_element_type=F32)
    tm = yt.shape[1]
    g = jnp.tile(g_ref[...], (1, tm // V7X_LANES))
    cos_t, sin_t = cos_ref[...], sin_ref[...]
    half = QK_ROPE // 2
    zeros = jnp.zeros((QK_PAD - QK_HEAD, tm), BF16)
    for h in range(MLA_HEADS):
        blk = yt[h * QK_HEAD:(h + 1) * QK_HEAD, :]
        r = lax.rsqrt(jnp.sum(blk * blk, axis=0, keepdims=True) * (1.0 / QK_HEAD) + EPS)
        z = (blk * r) * g
        x1, x2 = z[QK_NOPE:QK_NOPE + half, :], z[QK_NOPE + half:, :]
        q_ref[h, :QK_NOPE, :] = (z[:QK_NOPE, :] * scale).astype(BF16)
        q_ref[h, QK_NOPE:QK_NOPE + half, :] = ((x1 * cos_t - x2 * sin_t) * scale).astype(BF16)
        q_ref[h, QK_NOPE + half:QK_HEAD, :] = ((x2 * cos_t + x1 * sin_t) * scale).astype(BF16)
        q_ref[h, QK_HEAD:, :] = zeros


def _kv_up_kernel(ckv_ref, kpe_ref, w_ref, g_ref, cos_ref, s1_ref, s2_ref, k_ref, vt_ref):
    y = jnp.dot(ckv_ref[...], w_ref[...], preferred_element_type=F32)
    g_n = g_ref[:, :QK_NOPE]
    g_r = g_ref[:, QK_NOPE:]
    kpe = kpe_ref[...]
    ss_pe = jnp.sum(kpe * kpe, axis=-1, keepdims=True)
    cos_t, s1_t, s2_t = cos_ref[...], s1_ref[...], s2_ref[...]
    hw = QK_NOPE + V_HEAD
    pad_row = lax.broadcasted_iota(jnp.int32, (VT_ROWS - V_HEAD, y.shape[0]), 0)
    ones_rows = jnp.where(pad_row == 0, 1.0, 0.0).astype(BF16)
    kr = _rope(kpe * g_r, cos_t, s1_t, s2_t)
    for h in range(MLA_HEADS):
        kn = y[:, h * hw:h * hw + QK_NOPE]
        r = lax.rsqrt((jnp.sum(kn * kn, axis=-1, keepdims=True) + ss_pe) * (1.0 / QK_HEAD) + EPS)
        k_ref[h, :, :QK_NOPE] = ((kn * r) * g_n).astype(BF16)
        k_ref[h, :, QK_NOPE:] = (kr * r).astype(BF16)
        vt_ref[h, :V_HEAD, :] = y[:, h * hw + QK_NOPE:(h + 1) * hw].T.astype(BF16)
        vt_ref[h, V_HEAD:, :] = ones_rows


def _seq_spec(tm, n):
    return pl.BlockSpec((None, tm, n), lambda b, i: (b, i, 0))


def _tab_spec(tm):
    return pl.BlockSpec((tm, V7X_LANES), lambda b, i: (i, 0))


def _head_spec(tm, n):
    return pl.BlockSpec((None, MLA_HEADS, tm, n), lambda b, i: (b, 0, i, 0))


def _q_up(cqt, wt, g_rows, tabs_t, batch, seq, tm):
    full = lambda a: pl.BlockSpec(a.shape, lambda bb, i: (0, 0))
    scale = QK_HEAD ** -0.5 * LOG2E
    nblk = seq // tm
    tab = pl.BlockSpec((QK_ROPE // 2, tm), lambda bb, i: (0, i))
    return pl.pallas_call(
        functools.partial(_q_up_kernel, scale=scale),
        grid=(batch, nblk),
        in_specs=[pl.BlockSpec((Q_LORA, tm), lambda bb, i: (0, bb * nblk + i)), full(wt), full(g_rows), tab, tab],
        out_specs=pl.BlockSpec((None, MLA_HEADS, QK_PAD, tm), lambda bb, i: (bb, 0, 0, i)),
        out_shape=jax.ShapeDtypeStruct((batch, MLA_HEADS, QK_PAD, seq), BF16),
        name="a_q_up",
        compiler_params=_params(("arbitrary", "arbitrary")),
    )(cqt, wt, g_rows, *tabs_t)


def _kv_up(ckv3, kpe3, w, g_pad, tabs, tm):
    b, s, _ = ckv3.shape
    full = lambda a: pl.BlockSpec(a.shape, lambda bb, i: (0, 0))
    return pl.pallas_call(
        _kv_up_kernel,
        grid=(b, s // tm),
        in_specs=[_seq_spec(tm, KV_LORA), _seq_spec(tm, V7X_LANES), full(w), full(g_pad),
                  _tab_spec(tm), _tab_spec(tm), _tab_spec(tm)],
        out_specs=[_head_spec(tm, QK_PAD),
                   pl.BlockSpec((None, MLA_HEADS, None, VT_ROWS, tm), lambda bb, i: (bb, 0, i, 0, 0))],
        out_shape=[jax.ShapeDtypeStruct((b, MLA_HEADS, s, QK_PAD), BF16),
                   jax.ShapeDtypeStruct((b, MLA_HEADS, s // tm, VT_ROWS, tm), BF16)],
        name="a_kv_up",
        compiler_params=_params(("arbitrary", "arbitrary")),
    )(ckv3, kpe3, w, g_pad, *tabs)


def _mla_attn_kernel(q_ref, k_ref, vt_ref, o_ref, sig_sc, viol_sc, acc_sc, *, tq, tk):
    qi = pl.program_id(2)
    qt = q_ref[...]
    n_chunks = 2 * qi + 2

    def qk(kc):
        k = k_ref[pl.ds(pl.multiple_of(kc * tk, tk), tk), :]
        return jnp.dot(k, qt, preferred_element_type=F32)

    def causal(s, key_offset):
        key = lax.broadcasted_iota(jnp.int32, (tk, tq), 0) + key_offset
        qry = lax.broadcasted_iota(jnp.int32, (tk, tq), 1)
        return jnp.where(key <= qry, s, NEG_BIG)

    def finish():
        o_ref[...] = (acc_sc[:V_HEAD, :] / acc_sc[V_HEAD:V_HEAD + 1, :]).T.astype(BF16)

    def lagged_chunk(kc, key_offset):
        s = qk(kc)
        if key_offset is not None:
            s = causal(s, key_offset)
        shift = sig_sc[...]
        mx = jnp.max(s, axis=0, keepdims=True)
        p = jnp.exp2(s - shift).astype(BF16)
        new_shift = jnp.maximum(shift, mx)
        pv = jnp.dot(vt_ref[kc], p, preferred_element_type=F32)
        acc_sc[...] = (acc_sc[...] + pv) * jnp.exp2(shift - new_shift)
        sig_sc[...] = new_shift
        viol_sc[...] = jnp.maximum(viol_sc[...], mx - shift)

    s0 = jnp.dot(k_ref[:FIRST_SHIFT_KEYS, :], qt, preferred_element_type=F32)
    key0 = lax.broadcasted_iota(jnp.int32, s0.shape, 0) - qi * tq
    s0 = jnp.where(key0 <= lax.broadcasted_iota(jnp.int32, s0.shape, 1), s0, NEG_BIG)
    sig_sc[...] = jnp.max(s0, axis=0, keepdims=True)
    viol_sc[...] = jnp.zeros(viol_sc.shape, F32)
    acc_sc[...] = jnp.zeros(acc_sc.shape, F32)

    def quad(qd, carry):
        for u in range(4):
            lagged_chunk(4 * qd + u, None)
        return carry

    lax.fori_loop(0, lax.shift_right_logical(qi, 1), quad, 0)

    @pl.when(qi % 2 == 1)
    def _():
        lagged_chunk(2 * qi - 2, None)
        lagged_chunk(2 * qi - 1, None)

    lagged_chunk(2 * qi, 0)
    lagged_chunk(2 * qi + 1, tk)
    finish()

    @pl.when(jnp.max(viol_sc[...]) > LAG_LIMIT)
    def _():
        sig_sc[...] = jnp.full(sig_sc.shape, NEG_BIG, F32)
        acc_sc[...] = jnp.zeros(acc_sc.shape, F32)

        def exact_chunk(kc, carry):
            s = causal(qk(kc), kc * tk - qi * tq)
            m_old = sig_sc[...]
            m_new = jnp.maximum(m_old, jnp.max(s, axis=0, keepdims=True))
            p = jnp.exp2(s - m_new).astype(BF16)
            acc_sc[...] = jnp.exp2(m_old - m_new) * acc_sc[...] + jnp.dot(vt_ref[kc], p, preferred_element_type=F32)
            sig_sc[...] = m_new
            return carry

        lax.fori_loop(0, n_chunks, exact_chunk, 0)
        finish()


def _mla_attention(qt, k, vt, tk):
    b, h, s, _ = k.shape
    tq = 2 * tk
    assert vt.shape[-1] == tk and s % tq == 0
    return pl.pallas_call(
        functools.partial(_mla_attn_kernel, tq=tq, tk=tk),
        grid=(b, h, s // tq),
        in_specs=[pl.BlockSpec((None, None, QK_PAD, tq), lambda bb, hh, i: (bb, hh, 0, i)),
                  pl.BlockSpec((None, None, s, QK_PAD), lambda bb, hh, i: (bb, hh, 0, 0)),
                  pl.BlockSpec((None, None, s // tk, VT_ROWS, tk), lambda bb, hh, i: (bb, hh, 0, 0, 0))],
        out_specs=pl.BlockSpec((None, tq, V_HEAD), lambda bb, hh, i: (bb, i, hh)),
        out_shape=jax.ShapeDtypeStruct((b, s, MLA_WIDTH), BF16),
        scratch_shapes=[pltpu.VMEM((1, tq), F32), pltpu.VMEM((1, tq), F32), pltpu.VMEM((VT_ROWS, tq), F32)],
        name="a_mla_attention",
        compiler_params=_params(("arbitrary", "arbitrary", "arbitrary")),
    )(qt, k, vt)


def _a_out_kernel(o_ref, sg_ref, x_ref, w_ref, lnkv_ref, lnb_ref, h_ref, xkv_ref, xb_ref):
    a = (o_ref[...].astype(F32) * sg_ref[...].astype(F32)).astype(BF16)
    h = x_ref[...] + jnp.dot(a, w_ref[...], preferred_element_type=F32)
    h_ref[...] = h
    hn = h * _inv_rms(h, D_MODEL)
    xkv_ref[...] = (hn * lnkv_ref[...]).astype(BF16)
    xb_ref[...] = (hn * lnb_ref[...]).astype(BF16)


def _a_out(o2, sg, x2, w, ln_kv, ln_b, tm):
    m = x2.shape[0]
    row = pl.BlockSpec((tm, D_MODEL), lambda i: (i, 0))
    full = lambda a: pl.BlockSpec(a.shape, lambda i: (0, 0))
    return pl.pallas_call(
        _a_out_kernel,
        grid=(m // tm,),
        in_specs=[row, row, row, full(w), full(ln_kv), full(ln_b)],
        out_specs=[row, row, row],
        out_shape=[jax.ShapeDtypeStruct((m, D_MODEL), F32), jax.ShapeDtypeStruct((m, D_MODEL), BF16),
                   jax.ShapeDtypeStruct((m, D_MODEL), BF16)],
        name="a_out_proj",
        compiler_params=_params(("arbitrary",)),
    )(o2, sg, x2, w, ln_kv, ln_b)


def _branch_proj_kernel(x_ref, w_ref, g_ref, o_ref, y_sc, y2_sc, *, mode, dilation, scale):
    tm, tn = o_ref.shape
    rows = PROJ_ROW_CHUNK // dilation
    for c in range(tm // PROJ_ROW_CHUNK):
        y = jnp.dot(x_ref[c * PROJ_ROW_CHUNK:(c + 1) * PROJ_ROW_CHUNK, :], w_ref[...], preferred_element_type=F32)
        for hc in range(tn // DIL_HEAD_DIM):
            sl = slice(hc * DIL_HEAD_DIM, (hc + 1) * DIL_HEAD_DIM)
            yh = y[:, sl]
            if mode == "norm":
                yh = (yh * _inv_rms(yh, DIL_HEAD_DIM)) * g_ref[:, sl]
                if scale != 1.0:
                    yh = yh * scale
            elif mode == "silu":
                yh = yh * jax.nn.sigmoid(yh)
            if dilation == 1:
                o_ref[c * PROJ_ROW_CHUNK:(c + 1) * PROJ_ROW_CHUNK, sl] = yh.astype(BF16)
            else:
                base = c * PROJ_ROW_CHUNK
                y_sc[hc, base:base + PROJ_ROW_CHUNK, :] = yh
                if dilation == STRIDE_STEP * STRIDE_STEP:
                    quarter = PROJ_ROW_CHUNK // STRIDE_STEP
                    for a in range(STRIDE_STEP):
                        y2_sc[hc, base + a * quarter:base + (a + 1) * quarter, :] = (
                            y_sc[hc, pl.ds(base + a, quarter, stride=STRIDE_STEP), :])
                    for r in range(dilation):
                        a, b = r % STRIDE_STEP, r // STRIDE_STEP
                        dst = r * (tm // dilation) + c * rows
                        o_ref[dst:dst + rows, sl] = (
                            y2_sc[hc, pl.ds(base + a * quarter + b, rows, stride=STRIDE_STEP), :].astype(BF16))
                else:
                    for r in range(dilation):
                        dst = r * (tm // dilation) + c * rows
                        o_ref[dst:dst + rows, sl] = y_sc[hc, pl.ds(base + r, rows, stride=dilation), :].astype(BF16)


def _branch_proj(xn, w, g_cols, col0, mode, dilation, scale, name):
    m, kdim = xn.shape
    tm, tn = DIL_TILE, PROJ_COL_TILE
    blk0 = col0 // tn
    return pl.pallas_call(
        functools.partial(_branch_proj_kernel, mode=mode, dilation=dilation, scale=scale),
        grid=(m // tm, DIL_WIDTH // tn),
        in_specs=[pl.BlockSpec((tm, kdim), lambda i, j: (i, 0)),
                  pl.BlockSpec((kdim, tn), lambda i, j: (0, blk0 + j)),
                  pl.BlockSpec((1, tn), lambda i, j: (0, blk0 + j))],
        out_specs=pl.BlockSpec((tm, tn), lambda i, j: (i, j)),
        out_shape=jax.ShapeDtypeStruct((m, DIL_WIDTH), BF16),
        scratch_shapes=[pltpu.VMEM((tn // DIL_HEAD_DIM, tm, DIL_HEAD_DIM), F32)] * 2,
        name=name,
        compiler_params=_params(("arbitrary", "arbitrary")),
    )(xn, w, g_cols)


def _dilated_attn_kernel(*refs):
    qs, kps, kcs, vps, vcs = (refs[i * N_GROUPS:(i + 1) * N_GROUPS] for i in range(5))
    sg_ref, out_ref, o_sc, lse_sc = refs[5 * N_GROUPS:]
    t = pl.program_id(1)
    c = pl.program_id(2)
    row = lax.broadcasted_iota(jnp.int32, (BAND_BLK, 2 * BAND_BLK), 0)
    col = lax.broadcasted_iota(jnp.int32, (BAND_BLK, 2 * BAND_BLK), 1)
    rel = row + BAND_BLK - col
    band = (rel >= 0) & (rel <= LOOKBACK)
    first_band = band & (col >= jnp.where(t > 0, 0, BAND_BLK))
    relf = rel.astype(F32)
    for g, (_, dil) in enumerate(DIL_GROUPS):
        nblk = DIL_TILE // dil // BAND_BLK
        q_ref, kp_ref, kc_ref, vp_ref, vc_ref = qs[g], kps[g], kcs[g], vps[g], vcs[g]
        for hh in range(DIL_HEAD_CHUNK):
            hs = slice(hh * DIL_HEAD_DIM, (hh + 1) * DIL_HEAD_DIM)
            head = (c * DIL_HEAD_CHUNK + hh + 1).astype(F32)
            slope = jnp.exp2(jnp.full((1, 2 * BAND_BLK), -ALIBI_MAX / DIL_HEADS, F32) * head)
            bias = (slope * (-float(dil) * LOG2E)) * relf
            for r in range(dil):
                for a in range(nblk):
                    q = q_ref[r, a, :, hs]
                    k_lo = kp_ref[r, :, hs] if a == 0 else kc_ref[r, a - 1, :, hs]
                    v_lo = vp_ref[r, :, hs] if a == 0 else vc_ref[r, a - 1, :, hs]
                    k = jnp.concatenate([k_lo, kc_ref[r, a, :, hs]], axis=0)
                    v = jnp.concatenate([v_lo, vc_ref[r, a, :, hs]], axis=0)
                    s = lax.dot_general(q, k, (((1,), (1,)), ((), ())), preferred_element_type=F32)
                    s = jnp.where(first_band if a == 0 else band, s + bias, NEG_BIG)
                    m = jnp.max(s, axis=-1, keepdims=True)
                    p = jnp.exp2(s - m)
                    den = jnp.sum(p, axis=-1, keepdims=True)
                    o = jnp.dot(p.astype(BF16), v, preferred_element_type=F32) / den
                    tok = pl.ds(a * BAND_BLK * dil + r, BAND_BLK, stride=dil) if dil > 1 else pl.ds(a * BAND_BLK, BAND_BLK)
                    o_sc[g, hh, tok, :] = o
                    lse_sc[g, hh, tok, :] = jnp.broadcast_to(m + jnp.log2(den), (BAND_BLK, V7X_LANES))
    for hh in range(DIL_HEAD_CHUNK):
        hs = slice(hh * DIL_HEAD_DIM, (hh + 1) * DIL_HEAD_DIM)
        l0, l1, l2 = lse_sc[0, hh], lse_sc[1, hh], lse_sc[2, hh]
        mx = jnp.maximum(jnp.maximum(l0, l1), l2)
        e0, e1, e2 = jnp.exp2(l0 - mx), jnp.exp2(l1 - mx), jnp.exp2(l2 - mx)
        o = (e0 * o_sc[0, hh] + e1 * o_sc[1, hh] + e2 * o_sc[2, hh]) / (e0 + e1 + e2)
        out_ref[:, hs] = (o * sg_ref[:, hs].astype(F32)).astype(BF16)


def _dilated_attention(qs, ks, vs, gate, batch, seq):
    nt = seq // DIL_TILE
    ins, specs = [], []

    def add(arrs, prev):
        for arr, (_, dil) in zip(arrs, DIL_GROUPS):
            nblk = DIL_TILE // dil // BAND_BLK
            ins.append(arr.reshape(batch, nt, dil, nblk, BAND_BLK, DIL_WIDTH))
            if prev:
                specs.append(pl.BlockSpec((None, None, dil, None, BAND_BLK, DIL_COLS),
                                          lambda b, t, c, nblk=nblk: (b, jnp.maximum(t - 1, 0), 0, nblk - 1, 0, c)))
            else:
                specs.append(pl.BlockSpec((None, None, dil, nblk, BAND_BLK, DIL_COLS),
                                          lambda b, t, c: (b, t, 0, 0, 0, c)))

    add(qs, False)
    add(ks, True)
    add(ks, False)
    add(vs, True)
    add(vs, False)
    ins.append(gate.reshape(batch, nt, DIL_TILE, DIL_WIDTH))
    specs.append(pl.BlockSpec((None, None, DIL_TILE, DIL_COLS), lambda b, t, c: (b, t, 0, c)))
    out = pl.pallas_call(
        _dilated_attn_kernel,
        grid=(batch, nt, DIL_HEADS // DIL_HEAD_CHUNK),
        in_specs=specs,
        out_specs=pl.BlockSpec((None, None, DIL_TILE, DIL_COLS), lambda b, t, c: (b, t, 0, c)),
        out_shape=jax.ShapeDtypeStruct((batch, nt, DIL_TILE, DIL_WIDTH), BF16),
        scratch_shapes=[pltpu.VMEM((N_GROUPS, DIL_HEAD_CHUNK, DIL_TILE, DIL_HEAD_DIM), F32),
                        pltpu.VMEM((N_GROUPS, DIL_HEAD_CHUNK, DIL_TILE, V7X_LANES), F32)],
        name="b_dilated_attention",
        compiler_params=_params(("arbitrary", "arbitrary", "arbitrary")),
    )(*ins)
    return out.reshape(batch * seq, DIL_WIDTH)


def _b_out_kernel(a_ref, h_ref, w_ref, out_ref):
    out_ref[...] = h_ref[...] + jnp.dot(a_ref[...], w_ref[...], preferred_element_type=F32)


def _b_out(a, h1, w, tm):
    m = h1.shape[0]
    row = pl.BlockSpec((tm, D_MODEL), lambda i: (i, 0))
    return pl.pallas_call(
        _b_out_kernel,
        grid=(m // tm,),
        in_specs=[row, row, pl.BlockSpec(w.shape, lambda i: (0, 0))],
        out_specs=row,
        out_shape=jax.ShapeDtypeStruct((m, D_MODEL), F32),
        name="b_out_proj",
        compiler_params=_params(("arbitrary",)),
    )(a, h1, w)


def _tile(n, want):
    t = min(n, want)
    assert n % t == 0, (n, t)
    return t


def kernel(x, a_ln, a_w_in, a_ln_q, a_w_q_up, a_ln_kv, a_w_kv_up, a_q_norm, a_k_norm, a_w_o,
           kv_ln, kv_w, kv_k_norm, b_ln, b_w_in, b_q_norm, b_w_o):
    batch, seq, d = x.shape
    assert d == D_MODEL and a_ln.shape[0] == 1 and b_ln.shape[0] == 1
    assert seq % DIL_TILE == 0
    m = batch * seq
    x2 = x.reshape(m, d)

    lat = Q_LORA + KV_LORA
    w_in = a_w_in[0]
    w_in_pad = jnp.concatenate(
        [w_in[:, :lat], w_in[:, lat + QK_ROPE:], w_in[:, lat:lat + QK_ROPE],
         jnp.zeros((d, V7X_LANES - QK_ROPE), F32)], axis=1).astype(BF16)
    w_q_t = a_w_q_up[0].T.astype(BF16)
    w_kv = a_w_kv_up[0].astype(BF16)
    gq_rows = jnp.broadcast_to(a_q_norm[0][:, None], (QK_HEAD, V7X_LANES))
    gk_pad = jnp.pad(a_k_norm[0], (0, QK_PAD - QK_HEAD))[None, :]
    w_o_a = a_w_o[0].astype(BF16)
    w_kvs = kv_w.astype(BF16)
    w_b_in = b_w_in[0].astype(BF16)
    w_o_b = b_w_o[0].astype(BF16)
    k_cols = N_GROUPS * DIL_WIDTH
    gk_cols = jnp.concatenate([jnp.tile(kv_k_norm, (1, DIL_HEADS)).reshape(1, k_cols),
                               jnp.ones((1, k_cols), F32)], axis=1)
    gq_cols = jnp.concatenate([jnp.tile(b_q_norm[0], (1, DIL_HEADS)).reshape(1, k_cols),
                               jnp.ones((1, DIL_WIDTH), F32)], axis=1)

    tabs, tabs_t = _rope_tables(seq)
    cqt, ckv, sg_a, kpe = _a_in(x2, a_ln, w_in_pad, a_ln_q, a_ln_kv, _tile(m, 256))
    blk = _tile(seq, 512)
    q = _q_up(cqt, w_q_t, gq_rows, tabs_t, batch, seq, blk)
    k, vt = _kv_up(ckv.reshape(batch, seq, KV_LORA), kpe.reshape(batch, seq, V7X_LANES), w_kv, gk_pad, tabs, blk)
    o_a = _mla_attention(q, k, vt, blk)
    h1, xn_kv, xn_b = _a_out(o_a.reshape(m, MLA_WIDTH), sg_a, x2, w_o_a, kv_ln[None, :], b_ln, _tile(m, 256))

    q_scale = DIL_HEAD_DIM ** -0.5 * LOG2E
    qs, ks, vs = [], [], []
    for g, (_, dil) in enumerate(DIL_GROUPS):
        ks.append(_branch_proj(xn_kv, w_kvs, gk_cols, g * DIL_WIDTH, "norm", dil, 1.0, f"b_k_proj_d{dil}"))
        vs.append(_branch_proj(xn_kv, w_kvs, gk_cols, k_cols + g * DIL_WIDTH, "copy", dil, 1.0, f"b_v_proj_d{dil}"))
        qs.append(_branch_proj(xn_b, w_b_in, gq_cols, g * DIL_WIDTH, "norm", dil, q_scale, f"b_q_proj_d{dil}"))
    sg_b = _branch_proj(xn_b, w_b_in, gq_cols, k_cols, "silu", 1, 1.0, "b_gate_proj")
    a_b = _dilated_attention(qs, ks, vs, sg_b, batch, seq)
    out = _b_out(a_b, h1, w_o_b, _tile(m, 256))
    return out.reshape(batch, seq, d)
```

```python
import functools
import math

import jax
import jax.numpy as jnp
from jax import lax
from jax.experimental import pallas as pl
from jax.experimental.pallas import tpu as pltpu

F32 = jnp.float32
BF16 = jnp.bfloat16

V7X_LANES = 128
V7X_VMEM_BYTES = 64 * 1024 * 1024
VMEM_LIMIT_BYTES = V7X_VMEM_BYTES * 7 // 8

D_MODEL = 2048
MLA_HEADS = 16
Q_LORA = 512
KV_LORA = 512
QK_NOPE = 128
QK_ROPE = 64
QK_HEAD = QK_NOPE + QK_ROPE
V_HEAD = 128
MLA_WIDTH = MLA_HEADS * V_HEAD
ROPE_THETA = 10000.0
DIL_GROUPS = ((128, 1), (512, 4), (2048, 16))
N_GROUPS = 3
DIL_HEADS = 16
DIL_HEAD_DIM = 128
DIL_WIDTH = DIL_HEADS * DIL_HEAD_DIM
LOOKBACK = 128
BAND_BLK = 128
ALIBI_MAX = 8.0
EPS = 1e-6
NEG_BIG = -1e30
LOG2E = math.log2(math.e)

QK_PAD = 2 * V7X_LANES
A_IN_PAD = Q_LORA + KV_LORA + MLA_WIDTH + V7X_LANES

DIL_TILE = BAND_BLK * DIL_GROUPS[-1][1]
DIL_HEAD_CHUNK = 2
DIL_COLS = DIL_HEAD_CHUNK * DIL_HEAD_DIM
VT_ROWS = V_HEAD + 16
FIRST_SHIFT_KEYS = 16
LAG_LIMIT = 64.0
OUT_ROW_CHUNK = 256
STRIDE_STEP = 4
PROJ_ROW_CHUNK = 256
PROJ_COL_TILE = 512


def _params(semantics):
    return pltpu.CompilerParams(dimension_semantics=semantics, vmem_limit_bytes=VMEM_LIMIT_BYTES)


def _inv_rms(x, n):
    return lax.rsqrt(jnp.sum(x * x, axis=-1, keepdims=True) * (1.0 / n) + EPS)


def _rope(rv, cos_t, s1_t, s2_t):
    return rv * cos_t + pltpu.roll(rv, 96, 1) * s1_t + pltpu.roll(rv, 32, 1) * s2_t


def _rope_table_kernel(inv_ref, invt_ref, cos_ref, s1_ref, s2_ref, cost_ref, sint_ref):
    ts = cos_ref.shape[0]
    row = lax.broadcasted_iota(jnp.int32, (ts, V7X_LANES), 0) + pl.program_id(0) * ts
    lane = lax.broadcasted_iota(jnp.int32, (ts, V7X_LANES), 1)
    ang = row.astype(F32) * inv_ref[...]
    c = jnp.cos(ang)
    s = jnp.sin(ang)
    half = QK_ROPE // 2
    cos_ref[...] = jnp.where(lane < QK_ROPE, c, 0.0)
    s1_ref[...] = jnp.where(lane < half, -s, 0.0)
    s2_ref[...] = jnp.where((lane >= half) & (lane < QK_ROPE), s, 0.0)
    pos = lax.broadcasted_iota(jnp.int32, (half, ts), 1) + pl.program_id(0) * ts
    ang_t = pos.astype(F32) * jnp.tile(invt_ref[...], (1, ts // V7X_LANES))
    cost_ref[...] = jnp.cos(ang_t)
    sint_ref[...] = jnp.sin(ang_t)


def _rope_tables(seq):
    half = QK_ROPE // 2
    inv = ROPE_THETA ** (-jnp.arange(half, dtype=F32) / half)
    inv_lanes = jnp.concatenate([inv, inv, jnp.zeros((V7X_LANES - QK_ROPE,), F32)])[None, :]
    inv_rows = jnp.broadcast_to(inv[:, None], (half, V7X_LANES))
    ts = min(seq, 1024)
    tab = jax.ShapeDtypeStruct((seq, V7X_LANES), F32)
    tab_t = jax.ShapeDtypeStruct((half, seq), F32)
    spec = pl.BlockSpec((ts, V7X_LANES), lambda i: (i, 0))
    spec_t = pl.BlockSpec((half, ts), lambda i: (0, i))
    outs = pl.pallas_call(
        _rope_table_kernel,
        grid=(seq // ts,),
        in_specs=[pl.BlockSpec((1, V7X_LANES), lambda i: (0, 0)), pl.BlockSpec((half, V7X_LANES), lambda i: (0, 0))],
        out_specs=[spec, spec, spec, spec_t, spec_t],
        out_shape=[tab, tab, tab, tab_t, tab_t],
        name="rope_tables",
        compiler_params=_params(("arbitrary",)),
    )(inv_lanes, inv_rows)
    return outs[:3], outs[3:]


def _a_in_kernel(x_ref, ln_ref, w_ref, lnq_ref, lnkv_ref, cq_ref, ckv_ref, sg_ref, kpe_ref):
    x = x_ref[...]
    xn = ((x * _inv_rms(x, D_MODEL)) * ln_ref[...]).astype(BF16)
    u = jnp.dot(xn, w_ref[...], preferred_element_type=F32)
    cq = u[:, :Q_LORA]
    cq_ref[...] = ((cq * _inv_rms(cq, Q_LORA)) * lnq_ref[...]).T.astype(BF16)
    ckv = u[:, Q_LORA:Q_LORA + KV_LORA]
    ckv_ref[...] = ((ckv * _inv_rms(ckv, KV_LORA)) * lnkv_ref[...]).astype(BF16)
    g0 = Q_LORA + KV_LORA
    gate = u[:, g0:g0 + MLA_WIDTH]
    sg_ref[...] = (gate * jax.nn.sigmoid(gate)).astype(BF16)
    kpe_ref[...] = u[:, g0 + MLA_WIDTH:]


def _a_in(x2, ln, w_pad, ln_q, ln_kv, tm):
    m = x2.shape[0]
    row = lambda n: pl.BlockSpec((tm, n), lambda i: (i, 0))
    full = lambda a: pl.BlockSpec(a.shape, lambda i: (0, 0))
    return pl.pallas_call(
        _a_in_kernel,
        grid=(m // tm,),
        in_specs=[row(D_MODEL), full(ln), full(w_pad), full(ln_q), full(ln_kv)],
        out_specs=[pl.BlockSpec((Q_LORA, tm), lambda i: (0, i)), row(KV_LORA), row(MLA_WIDTH), row(V7X_LANES)],
        out_shape=[jax.ShapeDtypeStruct((Q_LORA, m), BF16), jax.ShapeDtypeStruct((m, KV_LORA), BF16),
                   jax.ShapeDtypeStruct((m, MLA_WIDTH), BF16), jax.ShapeDtypeStruct((m, V7X_LANES), F32)],
        name="a_in_proj",
        compiler_params=_params(("arbitrary",)),
    )(x2, ln, w_pad, ln_q, ln_kv)


def _q_up_kernel(cqt_ref, wt_ref, g_ref, cos_ref, sin_ref, q_ref, *, scale):
    yt = jnp.dot(wt_ref[...], cqt_ref[...], preferred_element_type=F32)
    tm = yt.shape[1]
    g = jnp.tile(g_ref[...], (1, tm // V7X_LANES))
    cos_t, sin_t = cos_ref[...], sin_ref[...]
    half = QK_ROPE // 2
    zeros = jnp.zeros((QK_PAD - QK_HEAD, tm), BF16)
    for h in range(MLA_HEADS):
        blk = yt[h * QK_HEAD:(h + 1) * QK_HEAD, :]
        r = lax.rsqrt(jnp.sum(blk * blk, axis=0, keepdims=True) * (1.0 / QK_HEAD) + EPS)
        z = (blk * r) * g
        x1, x2 = z[QK_NOPE:QK_NOPE + half, :], z[QK_NOPE + half:, :]
        q_ref[h, :QK_NOPE, :] = (z[:QK_NOPE, :] * scale).astype(BF16)
        q_ref[h, QK_NOPE:QK_NOPE + half, :] = ((x1 * cos_t - x2 * sin_t) * scale).astype(BF16)
        q_ref[h, QK_NOPE + half:QK_HEAD, :] = ((x2 * cos_t + x1 * sin_t) * scale).astype(BF16)
        q_ref[h, QK_HEAD:, :] = zeros


def _kv_up_kernel(ckv_ref, kpe_ref, w_ref, g_ref, cos_ref, s1_ref, s2_ref, k_ref, vt_ref):
    y = jnp.dot(ckv_ref[...], w_ref[...], preferred_element_type=F32)
    g_n = g_ref[:, :QK_NOPE]
    g_r = g_ref[:, QK_NOPE:]
    kpe = kpe_ref[...]
    ss_pe = jnp.sum(kpe * kpe, axis=-1, keepdims=True)
    cos_t, s1_t, s2_t = cos_ref[...], s1_ref[...], s2_ref[...]
    hw = QK_NOPE + V_HEAD
    pad_row = lax.broadcasted_iota(jnp.int32, (VT_ROWS - V_HEAD, y.shape[0]), 0)
    ones_rows = jnp.where(pad_row == 0, 1.0, 0.0).astype(BF16)
    kr = _rope(kpe * g_r, cos_t, s1_t, s2_t)
    for h in range(MLA_HEADS):
        kn = y[:, h * hw:h * hw + QK_NOPE]
        r = lax.rsqrt((jnp.sum(kn * kn, axis=-1, keepdims=True) + ss_pe) * (1.0 / QK_HEAD) + EPS)
        k_ref[h, :, :QK_NOPE] = ((kn * r) * g_n).astype(BF16)
        k_ref[h, :, QK_NOPE:] = (kr * r).astype(BF16)
        vt_ref[h, :V_HEAD, :] = y[:, h * hw + QK_NOPE:(h + 1) * hw].T.astype(BF16)
        vt_ref[h, V_HEAD:, :] = ones_rows


def _seq_spec(tm, n):
    return pl.BlockSpec((None, tm, n), lambda b, i: (b, i, 0))


def _tab_spec(tm):
    return pl.BlockSpec((tm, V7X_LANES), lambda b, i: (i, 0))


def _head_spec(tm, n):
    return pl.BlockSpec((None, MLA_HEADS, tm, n), lambda b, i: (b, 0, i, 0))


def _q_up(cqt, wt, g_rows, tabs_t, batch, seq, tm):
    full = lambda a: pl.BlockSpec(a.shape, lambda bb, i: (0, 0))
    scale = QK_HEAD ** -0.5 * LOG2E
    nblk = seq // tm
    tab = pl.BlockSpec((QK_ROPE // 2, tm), lambda bb, i: (0, i))
    return pl.pallas_call(
        functools.partial(_q_up_kernel, scale=scale),
        grid=(batch, nblk),
        in_specs=[pl.BlockSpec((Q_LORA, tm), lambda bb, i: (0, bb * nblk + i)), full(wt), full(g_rows), tab, tab],
        out_specs=pl.BlockSpec((None, MLA_HEADS, QK_PAD, tm), lambda bb, i: (bb, 0, 0, i)),
        out_shape=jax.ShapeDtypeStruct((batch, MLA_HEADS, QK_PAD, seq), BF16),
        name="a_q_up",
        compiler_params=_params(("arbitrary", "arbitrary")),
    )(cqt, wt, g_rows, *tabs_t)


def _kv_up(ckv3, kpe3, w, g_pad, tabs, tm):
    b, s, _ = ckv3.shape
    full = lambda a: pl.BlockSpec(a.shape, lambda bb, i: (0, 0))
    return pl.pallas_call(
        _kv_up_kernel,
        grid=(b, s // tm),
        in_specs=[_seq_spec(tm, KV_LORA), _seq_spec(tm, V7X_LANES), full(w), full(g_pad),
                  _tab_spec(tm), _tab_spec(tm), _tab_spec(tm)],
        out_specs=[_head_spec(tm, QK_PAD),
                   pl.BlockSpec((None, MLA_HEADS, None, VT_ROWS, tm), lambda bb, i: (bb, 0, i, 0, 0))],
        out_shape=[jax.ShapeDtypeStruct((b, MLA_HEADS, s, QK_PAD), BF16),
                   jax.ShapeDtypeStruct((b, MLA_HEADS, s // tm, VT_ROWS, tm), BF16)],
        name="a_kv_up",
        compiler_params=_params(("arbitrary", "arbitrary")),
    )(ckv3, kpe3, w, g_pad, *tabs)


def _mla_attn_kernel(q_ref, k_ref, vt_ref, o_ref, sig_sc, viol_sc, acc_sc, *, tq, tk):
    qi = pl.program_id(2)
    qt = q_ref[...]
    n_chunks = 2 * qi + 2

    def qk(kc):
        k = k_ref[pl.ds(pl.multiple_of(kc * tk, tk), tk), :]
        return jnp.dot(k, qt, preferred_element_type=F32)

    def causal(s, key_offset):
        key = lax.broadcasted_iota(jnp.int32, (tk, tq), 0) + key_offset
        qry = lax.broadcasted_iota(jnp.int32, (tk, tq), 1)
        return jnp.where(key <= qry, s, NEG_BIG)

    def finish():
        o_ref[...] = (acc_sc[:V_HEAD, :] / acc_sc[V_HEAD:V_HEAD + 1, :]).T.astype(BF16)

    def lagged_chunk(kc, key_offset):
        col0 = 0 if key_offset is None else key_offset
        cs = slice(col0, tq)
        k = k_ref[pl.ds(pl.multiple_of(kc * tk, tk), tk), :]
        s = jnp.dot(k, qt[:, cs], preferred_element_type=F32)
        if key_offset is not None:
            key = lax.broadcasted_iota(jnp.int32, s.shape, 0)
            qry = lax.broadcasted_iota(jnp.int32, s.shape, 1)
            s = jnp.where(key <= qry, s, NEG_BIG)
        shift = sig_sc[:, cs]
        mx = jnp.max(s, axis=0, keepdims=True)
        p = jnp.exp2(s - shift).astype(BF16)
        new_shift = jnp.maximum(shift, mx)
        pv = jnp.dot(vt_ref[kc], p, preferred_element_type=F32)
        acc_sc[:, cs] = (acc_sc[:, cs] + pv) * jnp.exp2(shift - new_shift)
        sig_sc[:, cs] = new_shift
        viol_sc[:, cs] = jnp.maximum(viol_sc[:, cs], mx - shift)

    s0 = jnp.dot(k_ref[:FIRST_SHIFT_KEYS, :], qt, preferred_element_type=F32)
    key0 = lax.broadcasted_iota(jnp.int32, s0.shape, 0) - qi * tq
    s0 = jnp.where(key0 <= lax.broadcasted_iota(jnp.int32, s0.shape, 1), s0, NEG_BIG)
    sig_sc[...] = jnp.max(s0, axis=0, keepdims=True)
    viol_sc[...] = jnp.zeros(viol_sc.shape, F32)
    acc_sc[...] = jnp.zeros(acc_sc.shape, F32)

    def quad(qd, carry):
        for u in range(4):
            lagged_chunk(4 * qd + u, None)
        return carry

    lax.fori_loop(0, lax.shift_right_logical(qi, 1), quad, 0)

    @pl.when(qi % 2 == 1)
    def _():
        lagged_chunk(2 * qi - 2, None)
        lagged_chunk(2 * qi - 1, None)

    lagged_chunk(2 * qi, 0)
    lagged_chunk(2 * qi + 1, tk)
    finish()

    @pl.when(jnp.max(viol_sc[...]) > LAG_LIMIT)
    def _():
        sig_sc[...] = jnp.full(sig_sc.shape, NEG_BIG, F32)
        acc_sc[...] = jnp.zeros(acc_sc.shape, F32)

        def exact_chunk(kc, carry):
            s = causal(qk(kc), kc * tk - qi * tq)
            m_old = sig_sc[...]
            m_new = jnp.maximum(m_old, jnp.max(s, axis=0, keepdims=True))
            p = jnp.exp2(s - m_new).astype(BF16)
            acc_sc[...] = jnp.exp2(m_old - m_new) * acc_sc[...] + jnp.dot(vt_ref[kc], p, preferred_element_type=F32)
            sig_sc[...] = m_new
            return carry

        lax.fori_loop(0, n_chunks, exact_chunk, 0)
        finish()


def _mla_attention(qt, k, vt, tk):
    b, h, s, _ = k.shape
    tq = 2 * tk
    assert vt.shape[-1] == tk and s % tq == 0
    return pl.pallas_call(
        functools.partial(_mla_attn_kernel, tq=tq, tk=tk),
        grid=(b, h, s // tq),
        in_specs=[pl.BlockSpec((None, None, QK_PAD, tq), lambda bb, hh, i: (bb, hh, 0, i)),
                  pl.BlockSpec((None, None, s, QK_PAD), lambda bb, hh, i: (bb, hh, 0, 0)),
                  pl.BlockSpec((None, None, s // tk, VT_ROWS, tk), lambda bb, hh, i: (bb, hh, 0, 0, 0))],
        out_specs=pl.BlockSpec((None, tq, V_HEAD), lambda bb, hh, i: (bb, i, hh)),
        out_shape=jax.ShapeDtypeStruct((b, s, MLA_WIDTH), BF16),
        scratch_shapes=[pltpu.VMEM((1, tq), F32), pltpu.VMEM((1, tq), F32), pltpu.VMEM((VT_ROWS, tq), F32)],
        name="a_mla_attention",
        compiler_params=_params(("arbitrary", "arbitrary", "arbitrary")),
    )(qt, k, vt)


def _a_out_kernel(o_ref, sg_ref, x_ref, w_ref, lnkv_ref, lnb_ref, h_ref, xkv_ref, xb_ref):
    tm = o_ref.shape[0]
    for c in range(tm // OUT_ROW_CHUNK):
        rows = slice(c * OUT_ROW_CHUNK, (c + 1) * OUT_ROW_CHUNK)
        a = (o_ref[rows, :] * sg_ref[rows, :]).astype(BF16)
        h = x_ref[rows, :] + jnp.dot(a, w_ref[...], preferred_element_type=F32)
        h_ref[rows, :] = h
        hn = h * _inv_rms(h, D_MODEL)
        xkv_ref[rows, :] = (hn * lnkv_ref[...]).astype(BF16)
        xb_ref[rows, :] = (hn * lnb_ref[...]).astype(BF16)


def _a_out(o2, sg, x2, w, ln_kv, ln_b, tm):
    m = x2.shape[0]
    row = pl.BlockSpec((tm, D_MODEL), lambda i: (i, 0))
    full = lambda a: pl.BlockSpec(a.shape, lambda i: (0, 0))
    return pl.pallas_call(
        _a_out_kernel,
        grid=(m // tm,),
        in_specs=[row, row, row, full(w), full(ln_kv), full(ln_b)],
        out_specs=[row, row, row],
        out_shape=[jax.ShapeDtypeStruct((m, D_MODEL), F32), jax.ShapeDtypeStruct((m, D_MODEL), BF16),
                   jax.ShapeDtypeStruct((m, D_MODEL), BF16)],
        name="a_out_proj",
        compiler_params=_params(("arbitrary",)),
    )(o2, sg, x2, w, ln_kv, ln_b)


def _branch_proj_kernel(x_ref, w_ref, g_ref, o_ref, y_sc, y2_sc, *, mode, dilation, scale):
    tm, tn = o_ref.shape
    rows = PROJ_ROW_CHUNK // dilation
    for c in range(tm // PROJ_ROW_CHUNK):
        y = jnp.dot(x_ref[c * PROJ_ROW_CHUNK:(c + 1) * PROJ_ROW_CHUNK, :], w_ref[...], preferred_element_type=F32)
        for hc in range(tn // DIL_HEAD_DIM):
            sl = slice(hc * DIL_HEAD_DIM, (hc + 1) * DIL_HEAD_DIM)
            yh = y[:, sl]
            if mode == "norm":
                yh = (yh * _inv_rms(yh, DIL_HEAD_DIM)) * g_ref[:, sl]
                if scale != 1.0:
                    yh = yh * scale
            elif mode == "silu":
                yh = yh * jax.nn.sigmoid(yh)
            if dilation == 1:
                o_ref[c * PROJ_ROW_CHUNK:(c + 1) * PROJ_ROW_CHUNK, sl] = yh.astype(BF16)
            else:
                base = c * PROJ_ROW_CHUNK
                y_sc[hc, base:base + PROJ_ROW_CHUNK, :] = yh
                if dilation == STRIDE_STEP * STRIDE_STEP:
                    quarter = PROJ_ROW_CHUNK // STRIDE_STEP
                    for a in range(STRIDE_STEP):
                        y2_sc[hc, base + a * quarter:base + (a + 1) * quarter, :] = (
                            y_sc[hc, pl.ds(base + a, quarter, stride=STRIDE_STEP), :])
                    for r in range(dilation):
                        a, b = r % STRIDE_STEP, r // STRIDE_STEP
                        dst = r * (tm // dilation) + c * rows
                        o_ref[dst:dst + rows, sl] = (
                            y2_sc[hc, pl.ds(base + a * quarter + b, rows, stride=STRIDE_STEP), :].astype(BF16))
                else:
                    for r in range(dilation):
                        dst = r * (tm // dilation) + c * rows
                        o_ref[dst:dst + rows, sl] = y_sc[hc, pl.ds(base + r, rows, stride=dilation), :].astype(BF16)


def _branch_proj(xn, w, g_cols, col0, mode, dilation, scale, name):
    m, kdim = xn.shape
    tm, tn = DIL_TILE, PROJ_COL_TILE
    blk0 = col0 // tn
    return pl.pallas_call(
        functools.partial(_branch_proj_kernel, mode=mode, dilation=dilation, scale=scale),
        grid=(m // tm, DIL_WIDTH // tn),
        in_specs=[pl.BlockSpec((tm, kdim), lambda i, j: (i, 0)),
                  pl.BlockSpec((kdim, tn), lambda i, j: (0, blk0 + j)),
                  pl.BlockSpec((1, tn), lambda i, j: (0, blk0 + j))],
        out_specs=pl.BlockSpec((tm, tn), lambda i, j: (i, j)),
        out_shape=jax.ShapeDtypeStruct((m, DIL_WIDTH), BF16),
        scratch_shapes=[pltpu.VMEM((tn // DIL_HEAD_DIM, tm, DIL_HEAD_DIM), F32)] * 2,
        name=name,
        compiler_params=_params(("arbitrary", "arbitrary")),
    )(xn, w, g_cols)


def _dilated_attn_kernel(*refs):
    qs, kps, kcs, vps, vcs = (refs[i * N_GROUPS:(i + 1) * N_GROUPS] for i in range(5))
    sg_ref, out_ref, o_sc, lse_sc = refs[5 * N_GROUPS:]
    t = pl.program_id(1)
    c = pl.program_id(2)
    row = lax.broadcasted_iota(jnp.int32, (BAND_BLK, 2 * BAND_BLK), 0)
    col = lax.broadcasted_iota(jnp.int32, (BAND_BLK, 2 * BAND_BLK), 1)
    rel = row + BAND_BLK - col
    band = (rel >= 0) & (rel <= LOOKBACK)
    first_band = band & (col >= jnp.where(t > 0, 0, BAND_BLK))
    relf = rel.astype(F32)
    for g, (_, dil) in enumerate(DIL_GROUPS):
        nblk = DIL_TILE // dil // BAND_BLK
        q_ref, kp_ref, kc_ref, vp_ref, vc_ref = qs[g], kps[g], kcs[g], vps[g], vcs[g]
        for hh in range(DIL_HEAD_CHUNK):
            hs = slice(hh * DIL_HEAD_DIM, (hh + 1) * DIL_HEAD_DIM)
            head = (c * DIL_HEAD_CHUNK + hh + 1).astype(F32)
            slope = jnp.exp2(jnp.full((1, 2 * BAND_BLK), -ALIBI_MAX / DIL_HEADS, F32) * head)
            bias = (slope * (-float(dil) * LOG2E)) * relf
            bias_band = jnp.where(band, bias, NEG_BIG)
            bias_first = jnp.where(first_band, bias, NEG_BIG)
            for r in range(dil):
                for a in range(nblk):
                    q = q_ref[r, a, :, hs]
                    k_lo = kp_ref[r, :, hs] if a == 0 else kc_ref[r, a - 1, :, hs]
                    v_lo = vp_ref[r, :, hs] if a == 0 else vc_ref[r, a - 1, :, hs]
                    k = jnp.concatenate([k_lo, kc_ref[r, a, :, hs]], axis=0)
                    v = jnp.concatenate([v_lo, vc_ref[r, a, :, hs]], axis=0)
                    s = lax.dot_general(q, k, (((1,), (1,)), ((), ())), preferred_element_type=F32)
                    s = s + (bias_first if a == 0 else bias_band)
                    m = jnp.max(s, axis=-1, keepdims=True)
                    p = jnp.exp2(s - m)
                    den = jnp.sum(p, axis=-1, keepdims=True)
                    o = jnp.dot(p.astype(BF16), v, preferred_element_type=F32) / den
                    tok = pl.ds(a * BAND_BLK * dil + r, BAND_BLK, stride=dil) if dil > 1 else pl.ds(a * BAND_BLK, BAND_BLK)
                    o_sc[g, hh, tok, :] = o
                    lse_sc[g, hh, tok, :] = jnp.broadcast_to(m + jnp.log2(den), (BAND_BLK, V7X_LANES))
    for hh in range(DIL_HEAD_CHUNK):
        hs = slice(hh * DIL_HEAD_DIM, (hh + 1) * DIL_HEAD_DIM)
        l0, l1, l2 = lse_sc[0, hh], lse_sc[1, hh], lse_sc[2, hh]
        mx = jnp.maximum(jnp.maximum(l0, l1), l2)
        e0, e1, e2 = jnp.exp2(l0 - mx), jnp.exp2(l1 - mx), jnp.exp2(l2 - mx)
        o = (e0 * o_sc[0, hh] + e1 * o_sc[1, hh] + e2 * o_sc[2, hh]) / (e0 + e1 + e2)
        out_ref[:, hs] = (o * sg_ref[:, hs].astype(F32)).astype(BF16)


def _dilated_attention(qs, ks, vs, gate, batch, seq):
    nt = seq // DIL_TILE
    ins, specs = [], []

    def add(arrs, prev):
        for arr, (_, dil) in zip(arrs, DIL_GROUPS):
            nblk = DIL_TILE // dil // BAND_BLK
            ins.append(arr.reshape(batch, nt, dil, nblk, BAND_BLK, DIL_WIDTH))
            if prev:
                specs.append(pl.BlockSpec((None, None, dil, None, BAND_BLK, DIL_COLS),
                                          lambda b, t, c, nblk=nblk: (b, jnp.maximum(t - 1, 0), 0, nblk - 1, 0, c)))
            else:
                specs.append(pl.BlockSpec((None, None, dil, nblk, BAND_BLK, DIL_COLS),
                                          lambda b, t, c: (b, t, 0, 0, 0, c)))

    add(qs, False)
    add(ks, True)
    add(ks, False)
    add(vs, True)
    add(vs, False)
    ins.append(gate.reshape(batch, nt, DIL_TILE, DIL_WIDTH))
    specs.append(pl.BlockSpec((None, None, DIL_TILE, DIL_COLS), lambda b, t, c: (b, t, 0, c)))
    out = pl.pallas_call(
        _dilated_attn_kernel,
        grid=(batch, nt, DIL_HEADS // DIL_HEAD_CHUNK),
        in_specs=specs,
        out_specs=pl.BlockSpec((None, None, DIL_TILE, DIL_COLS), lambda b, t, c: (b, t, 0, c)),
        out_shape=jax.ShapeDtypeStruct((batch, nt, DIL_TILE, DIL_WIDTH), BF16),
        scratch_shapes=[pltpu.VMEM((N_GROUPS, DIL_HEAD_CHUNK, DIL_TILE, DIL_HEAD_DIM), F32),
                        pltpu.VMEM((N_GROUPS, DIL_HEAD_CHUNK, DIL_TILE, V7X_LANES), F32)],
        name="b_dilated_attention",
        compiler_params=_params(("arbitrary", "arbitrary", "arbitrary")),
    )(*ins)
    return out.reshape(batch * seq, DIL_WIDTH)


def _b_out_kernel(a_ref, h_ref, w_ref, out_ref):
    out_ref[...] = h_ref[...] + jnp.dot(a_ref[...], w_ref[...], preferred_element_type=F32)


def _b_out(a, h1, w, tm):
    m = h1.shape[0]
    row = pl.BlockSpec((tm, D_MODEL), lambda i: (i, 0))
    return pl.pallas_call(
        _b_out_kernel,
        grid=(m // tm,),
        in_specs=[row, row, pl.BlockSpec(w.shape, lambda i: (0, 0))],
        out_specs=row,
        out_shape=jax.ShapeDtypeStruct((m, D_MODEL), F32),
        name="b_out_proj",
        compiler_params=_params(("arbitrary",)),
    )(a, h1, w)


def _tile(n, want):
    t = min(n, want)
    assert n % t == 0, (n, t)
    return t


def kernel(x, a_ln, a_w_in, a_ln_q, a_w_q_up, a_ln_kv, a_w_kv_up, a_q_norm, a_k_norm, a_w_o,
           kv_ln, kv_w, kv_k_norm, b_ln, b_w_in, b_q_norm, b_w_o):
    batch, seq, d = x.shape
    assert d == D_MODEL and a_ln.shape[0] == 1 and b_ln.shape[0] == 1
    assert seq % DIL_TILE == 0
    m = batch * seq
    x2 = x.reshape(m, d)

    lat = Q_LORA + KV_LORA
    w_in = a_w_in[0]
    w_in_pad = jnp.concatenate(
        [w_in[:, :lat], w_in[:, lat + QK_ROPE:], w_in[:, lat:lat + QK_ROPE],
         jnp.zeros((d, V7X_LANES - QK_ROPE), F32)], axis=1).astype(BF16)
    w_q_t = a_w_q_up[0].T.astype(BF16)
    w_kv = a_w_kv_up[0].astype(BF16)
    gq_rows = jnp.broadcast_to(a_q_norm[0][:, None], (QK_HEAD, V7X_LANES))
    gk_pad = jnp.pad(a_k_norm[0], (0, QK_PAD - QK_HEAD))[None, :]
    w_o_a = a_w_o[0].astype(BF16)
    w_kvs = kv_w.astype(BF16)
    w_b_in = b_w_in[0].astype(BF16)
    w_o_b = b_w_o[0].astype(BF16)
    k_cols = N_GROUPS * DIL_WIDTH
    gk_cols = jnp.concatenate([jnp.tile(kv_k_norm, (1, DIL_HEADS)).reshape(1, k_cols),
                               jnp.ones((1, k_cols), F32)], axis=1)
    gq_cols = jnp.concatenate([jnp.tile(b_q_norm[0], (1, DIL_HEADS)).reshape(1, k_cols),
                               jnp.ones((1, DIL_WIDTH), F32)], axis=1)

    tabs, tabs_t = _rope_tables(seq)
    cqt, ckv, sg_a, kpe = _a_in(x2, a_ln, w_in_pad, a_ln_q, a_ln_kv, _tile(m, 256))
    blk = _tile(seq, 512)
    q = _q_up(cqt, w_q_t, gq_rows, tabs_t, batch, seq, blk)
    k, vt = _kv_up(ckv.reshape(batch, seq, KV_LORA), kpe.reshape(batch, seq, V7X_LANES), w_kv, gk_pad, tabs, blk)
    o_a = _mla_attention(q, k, vt, blk)
    h1, xn_kv, xn_b = _a_out(o_a.reshape(m, MLA_WIDTH), sg_a, x2, w_o_a, kv_ln[None, :], b_ln, _tile(m, 512))

    q_scale = DIL_HEAD_DIM ** -0.5 * LOG2E
    qs, ks, vs = [], [], []
    for g, (_, dil) in enumerate(DIL_GROUPS):
        ks.append(_branch_proj(xn_kv, w_kvs, gk_cols, g * DIL_WIDTH, "norm", dil, 1.0, f"b_k_proj_d{dil}"))
        vs.append(_branch_proj(xn_kv, w_kvs, gk_cols, k_cols + g * DIL_WIDTH, "copy", dil, 1.0, f"b_v_proj_d{dil}"))
        qs.append(_branch_proj(xn_b, w_b_in, gq_cols, g * DIL_WIDTH, "norm", dil, q_scale, f"b_q_proj_d{dil}"))
    sg_b = _branch_proj(xn_b, w_b_in, gq_cols, k_cols, "silu", 1, 1.0, "b_gate_proj")
    a_b = _dilated_attention(qs, ks, vs, sg_b, batch, seq)
    out = _b_out(a_b, h1, w_o_b, _tile(m, 256))
    return out.reshape(batch, seq, d)
```

```python
import functools
import math

import jax
import jax.numpy as jnp
from jax import lax
from jax.experimental import pallas as pl
from jax.experimental.pallas import tpu as pltpu

F32 = jnp.float32
BF16 = jnp.bfloat16

V7X_LANES = 128
V7X_VMEM_BYTES = 64 * 1024 * 1024
VMEM_LIMIT_BYTES = V7X_VMEM_BYTES * 7 // 8

D_MODEL = 2048
MLA_HEADS = 16
Q_LORA = 512
KV_LORA = 512
QK_NOPE = 128
QK_ROPE = 64
QK_HEAD = QK_NOPE + QK_ROPE
V_HEAD = 128
MLA_WIDTH = MLA_HEADS * V_HEAD
ROPE_THETA = 10000.0
DIL_GROUPS = ((128, 1), (512, 4), (2048, 16))
N_GROUPS = 3
DIL_HEADS = 16
DIL_HEAD_DIM = 128
DIL_WIDTH = DIL_HEADS * DIL_HEAD_DIM
LOOKBACK = 128
BAND_BLK = 128
ALIBI_MAX = 8.0
EPS = 1e-6
NEG_BIG = -1e30
LOG2E = math.log2(math.e)

QK_PAD = 2 * V7X_LANES
A_IN_PAD = Q_LORA + KV_LORA + MLA_WIDTH + V7X_LANES

DIL_TILE = BAND_BLK * DIL_GROUPS[-1][1]
DIL_HEAD_CHUNK = 2
DIL_COLS = DIL_HEAD_CHUNK * DIL_HEAD_DIM
V7X_BF16_SUBLANES = 16
A_IN_ROW_TILE = 256
OUT_ROW_CHUNK = 256
MLA_KEY_CHUNK = 512
MLA_UNROLL = 8
VT_ROWS = V_HEAD + V7X_BF16_SUBLANES
FIRST_SHIFT_KEYS = V7X_BF16_SUBLANES
LAG_LIMIT = 64.0
STRIDE_STEP = 4
PROJ_ROW_CHUNK = 256
PROJ_COL_TILE = 512


def _params(semantics):
    return pltpu.CompilerParams(dimension_semantics=semantics, vmem_limit_bytes=VMEM_LIMIT_BYTES)


def _inv_rms(x, n):
    return lax.rsqrt(jnp.sum(x * x, axis=-1, keepdims=True) * (1.0 / n) + EPS)


def _rope(rv, cos_t, s1_t, s2_t):
    return rv * cos_t + pltpu.roll(rv, 96, 1) * s1_t + pltpu.roll(rv, 32, 1) * s2_t


def _rope_table_kernel(inv_ref, invt_ref, cos_ref, s1_ref, s2_ref, cost_ref, sint_ref):
    ts = cos_ref.shape[0]
    row = lax.broadcasted_iota(jnp.int32, (ts, V7X_LANES), 0) + pl.program_id(0) * ts
    lane = lax.broadcasted_iota(jnp.int32, (ts, V7X_LANES), 1)
    ang = row.astype(F32) * inv_ref[...]
    c = jnp.cos(ang)
    s = jnp.sin(ang)
    half = QK_ROPE // 2
    cos_ref[...] = jnp.where(lane < QK_ROPE, c, 0.0)
    s1_ref[...] = jnp.where(lane < half, -s, 0.0)
    s2_ref[...] = jnp.where((lane >= half) & (lane < QK_ROPE), s, 0.0)
    pos = lax.broadcasted_iota(jnp.int32, (half, ts), 1) + pl.program_id(0) * ts
    ang_t = pos.astype(F32) * jnp.tile(invt_ref[...], (1, ts // V7X_LANES))
    cost_ref[...] = jnp.cos(ang_t)
    sint_ref[...] = jnp.sin(ang_t)


def _rope_tables(seq):
    half = QK_ROPE // 2
    inv = ROPE_THETA ** (-jnp.arange(half, dtype=F32) / half)
    inv_lanes = jnp.concatenate([inv, inv, jnp.zeros((V7X_LANES - QK_ROPE,), F32)])[None, :]
    inv_rows = jnp.broadcast_to(inv[:, None], (half, V7X_LANES))
    ts = min(seq, 1024)
    tab = jax.ShapeDtypeStruct((seq, V7X_LANES), F32)
    tab_t = jax.ShapeDtypeStruct((half, seq), F32)
    spec = pl.BlockSpec((ts, V7X_LANES), lambda i: (i, 0))
    spec_t = pl.BlockSpec((half, ts), lambda i: (0, i))
    outs = pl.pallas_call(
        _rope_table_kernel,
        grid=(seq // ts,),
        in_specs=[pl.BlockSpec((1, V7X_LANES), lambda i: (0, 0)), pl.BlockSpec((half, V7X_LANES), lambda i: (0, 0))],
        out_specs=[spec, spec, spec, spec_t, spec_t],
        out_shape=[tab, tab, tab, tab_t, tab_t],
        name="rope_tables",
        compiler_params=_params(("arbitrary",)),
    )(inv_lanes, inv_rows)
    return outs[:3], outs[3:]


def _a_in_kernel(x_ref, ln_ref, w_ref, lnq_ref, lnkv_ref, cq_ref, ckv_ref, sg_ref, kpe_ref):
    x = x_ref[...]
    xn = ((x * _inv_rms(x, D_MODEL)) * ln_ref[...]).astype(BF16)
    u = jnp.dot(xn, w_ref[...], preferred_element_type=F32)
    cq = u[:, :Q_LORA]
    cq_ref[...] = ((cq * _inv_rms(cq, Q_LORA)) * lnq_ref[...]).T.astype(BF16)
    ckv = u[:, Q_LORA:Q_LORA + KV_LORA]
    ckv_ref[...] = ((ckv * _inv_rms(ckv, KV_LORA)) * lnkv_ref[...]).astype(BF16)
    g0 = Q_LORA + KV_LORA
    gate = u[:, g0:g0 + MLA_WIDTH]
    sg_ref[...] = (gate * jax.nn.sigmoid(gate)).astype(BF16)
    kpe_ref[...] = u[:, g0 + MLA_WIDTH:]


def _a_in(x2, ln, w_pad, ln_q, ln_kv, tm):
    m = x2.shape[0]
    row = lambda n: pl.BlockSpec((tm, n), lambda i: (i, 0))
    full = lambda a: pl.BlockSpec(a.shape, lambda i: (0, 0))
    return pl.pallas_call(
        _a_in_kernel,
        grid=(m // tm,),
        in_specs=[row(D_MODEL), full(ln), full(w_pad), full(ln_q), full(ln_kv)],
        out_specs=[pl.BlockSpec((Q_LORA, tm), lambda i: (0, i)), row(KV_LORA), row(MLA_WIDTH), row(V7X_LANES)],
        out_shape=[jax.ShapeDtypeStruct((Q_LORA, m), BF16), jax.ShapeDtypeStruct((m, KV_LORA), BF16),
                   jax.ShapeDtypeStruct((m, MLA_WIDTH), BF16), jax.ShapeDtypeStruct((m, V7X_LANES), F32)],
        name="a_in_proj",
        compiler_params=_params(("arbitrary",)),
    )(x2, ln, w_pad, ln_q, ln_kv)


def _q_up_kernel(cqt_ref, wt_ref, g_ref, cos_ref, sin_ref, q_ref, *, scale):
    yt = jnp.dot(wt_ref[...], cqt_ref[...], preferred_element_type=F32)
    tm = yt.shape[1]
    g = jnp.tile(g_ref[...], (1, tm // V7X_LANES))
    cos_t, sin_t = cos_ref[...], sin_ref[...]
    half = QK_ROPE // 2
    zeros = jnp.zeros((QK_PAD - QK_HEAD, tm), BF16)
    for h in range(MLA_HEADS):
        blk = yt[h * QK_HEAD:(h + 1) * QK_HEAD, :]
        r = lax.rsqrt(jnp.sum(blk * blk, axis=0, keepdims=True) * (1.0 / QK_HEAD) + EPS)
        z = (blk * r) * g
        x1, x2 = z[QK_NOPE:QK_NOPE + half, :], z[QK_NOPE + half:, :]
        q_ref[h, :QK_NOPE, :] = (z[:QK_NOPE, :] * scale).astype(BF16)
        q_ref[h, QK_NOPE:QK_NOPE + half, :] = ((x1 * cos_t - x2 * sin_t) * scale).astype(BF16)
        q_ref[h, QK_NOPE + half:QK_HEAD, :] = ((x2 * cos_t + x1 * sin_t) * scale).astype(BF16)
        q_ref[h, QK_HEAD:, :] = zeros


def _kv_up_kernel(ckv_ref, kpe_ref, w_ref, g_ref, cos_ref, s1_ref, s2_ref, k_ref, vt_ref):
    y = jnp.dot(ckv_ref[...], w_ref[...], preferred_element_type=F32)
    g_n = g_ref[:, :QK_NOPE]
    g_r = g_ref[:, QK_NOPE:]
    kpe = kpe_ref[...]
    ss_pe = jnp.sum(kpe * kpe, axis=-1, keepdims=True)
    cos_t, s1_t, s2_t = cos_ref[...], s1_ref[...], s2_ref[...]
    hw = QK_NOPE + V_HEAD
    pad_row = lax.broadcasted_iota(jnp.int32, (VT_ROWS - V_HEAD, y.shape[0]), 0)
    ones_rows = jnp.where(pad_row == 0, 1.0, 0.0).astype(BF16)
    kr = _rope(kpe * g_r, cos_t, s1_t, s2_t)
    for h in range(MLA_HEADS):
        kn = y[:, h * hw:h * hw + QK_NOPE]
        r = lax.rsqrt((jnp.sum(kn * kn, axis=-1, keepdims=True) + ss_pe) * (1.0 / QK_HEAD) + EPS)
        k_ref[h, :, :QK_NOPE] = ((kn * r) * g_n).astype(BF16)
        k_ref[h, :, QK_NOPE:] = (kr * r).astype(BF16)
        vt_ref[h, :V_HEAD, :] = y[:, h * hw + QK_NOPE:(h + 1) * hw].T.astype(BF16)
        vt_ref[h, V_HEAD:, :] = ones_rows


def _seq_spec(tm, n):
    return pl.BlockSpec((None, tm, n), lambda b, i: (b, i, 0))


def _tab_spec(tm):
    return pl.BlockSpec((tm, V7X_LANES), lambda b, i: (i, 0))


def _head_spec(tm, n):
    return pl.BlockSpec((None, MLA_HEADS, tm, n), lambda b, i: (b, 0, i, 0))


def _q_up(cqt, wt, g_rows, tabs_t, batch, seq, tm):
    full = lambda a: pl.BlockSpec(a.shape, lambda bb, i: (0, 0))
    scale = QK_HEAD ** -0.5 * LOG2E
    nblk = seq // tm
    tab = pl.BlockSpec((QK_ROPE // 2, tm), lambda bb, i: (0, i))
    return pl.pallas_call(
        functools.partial(_q_up_kernel, scale=scale),
        grid=(batch, nblk),
        in_specs=[pl.BlockSpec((Q_LORA, tm), lambda bb, i: (0, bb * nblk + i)), full(wt), full(g_rows), tab, tab],
        out_specs=pl.BlockSpec((None, MLA_HEADS, QK_PAD, tm), lambda bb, i: (bb, 0, 0, i)),
        out_shape=jax.ShapeDtypeStruct((batch, MLA_HEADS, QK_PAD, seq), BF16),
        name="a_q_up",
        compiler_params=_params(("arbitrary", "arbitrary")),
    )(cqt, wt, g_rows, *tabs_t)


def _kv_up(ckv3, kpe3, w, g_pad, tabs, tm):
    b, s, _ = ckv3.shape
    full = lambda a: pl.BlockSpec(a.shape, lambda bb, i: (0, 0))
    return pl.pallas_call(
        _kv_up_kernel,
        grid=(b, s // tm),
        in_specs=[_seq_spec(tm, KV_LORA), _seq_spec(tm, V7X_LANES), full(w), full(g_pad),
                  _tab_spec(tm), _tab_spec(tm), _tab_spec(tm)],
        out_specs=[_head_spec(tm, QK_PAD),
                   pl.BlockSpec((None, MLA_HEADS, None, VT_ROWS, tm), lambda bb, i: (bb, 0, i, 0, 0))],
        out_shape=[jax.ShapeDtypeStruct((b, MLA_HEADS, s, QK_PAD), BF16),
                   jax.ShapeDtypeStruct((b, MLA_HEADS, s // tm, VT_ROWS, tm), BF16)],
        name="a_kv_up",
        compiler_params=_params(("arbitrary", "arbitrary")),
    )(ckv3, kpe3, w, g_pad, *tabs)


def _mla_attn_kernel(q_ref, k_ref, vt_ref, o_ref, sig_sc, viol_sc, acc_sc, *, tq, tk):
    qi = pl.program_id(2)
    qt = q_ref[...]
    n_chunks = 2 * qi + 2

    def qk(kc):
        k = k_ref[pl.ds(pl.multiple_of(kc * tk, tk), tk), :]
        return jnp.dot(k, qt, preferred_element_type=F32)

    def causal(s, key_offset):
        key = lax.broadcasted_iota(jnp.int32, (tk, tq), 0) + key_offset
        qry = lax.broadcasted_iota(jnp.int32, (tk, tq), 1)
        return jnp.where(key <= qry, s, NEG_BIG)

    def finish():
        o_ref[...] = (acc_sc[:V_HEAD, :] / acc_sc[V_HEAD:V_HEAD + 1, :]).T.astype(BF16)

    def lagged_chunk(kc, key_offset):
        col0 = 0 if key_offset is None else key_offset
        cs = slice(col0, tq)
        k = k_ref[pl.ds(pl.multiple_of(kc * tk, tk), tk), :]
        s = jnp.dot(k, qt[:, cs], preferred_element_type=F32)
        if key_offset is not None:
            key = lax.broadcasted_iota(jnp.int32, s.shape, 0)
            qry = lax.broadcasted_iota(jnp.int32, s.shape, 1)
            s = jnp.where(key <= qry, s, NEG_BIG)
        shift = sig_sc[:, cs]
        mx = jnp.max(s, axis=0, keepdims=True)
        p = jnp.exp2(s - shift).astype(BF16)
        new_shift = jnp.maximum(shift, mx)
        pv = jnp.dot(vt_ref[kc], p, preferred_element_type=F32)
        acc_sc[:, cs] = (acc_sc[:, cs] + pv) * jnp.exp2(shift - new_shift)
        sig_sc[:, cs] = new_shift
        viol_sc[:, cs] = jnp.maximum(viol_sc[:, cs], mx - shift)

    s0 = jnp.dot(k_ref[:FIRST_SHIFT_KEYS, :], qt, preferred_element_type=F32)
    key0 = lax.broadcasted_iota(jnp.int32, s0.shape, 0) - qi * tq
    s0 = jnp.where(key0 <= lax.broadcasted_iota(jnp.int32, s0.shape, 1), s0, NEG_BIG)
    sig_sc[...] = jnp.max(s0, axis=0, keepdims=True)
    viol_sc[...] = jnp.zeros(viol_sc.shape, F32)
    acc_sc[...] = jnp.zeros(acc_sc.shape, F32)

    def octet(io, carry):
        for u in range(MLA_UNROLL):
            lagged_chunk(MLA_UNROLL * io + u, None)
        return carry

    n_oct = lax.shift_right_logical(qi, 2)
    lax.fori_loop(0, n_oct, octet, 0)

    @pl.when((qi & 2) != 0)
    def _():
        for u in range(4):
            lagged_chunk(MLA_UNROLL * n_oct + u, None)

    @pl.when((qi & 1) != 0)
    def _():
        lagged_chunk(2 * qi - 2, None)
        lagged_chunk(2 * qi - 1, None)

    lagged_chunk(2 * qi, 0)
    lagged_chunk(2 * qi + 1, tk)
    finish()

    @pl.when(jnp.max(viol_sc[...]) > LAG_LIMIT)
    def _():
        sig_sc[...] = jnp.full(sig_sc.shape, NEG_BIG, F32)
        acc_sc[...] = jnp.zeros(acc_sc.shape, F32)

        def exact_chunk(kc, carry):
            s = causal(qk(kc), kc * tk - qi * tq)
            m_old = sig_sc[...]
            m_new = jnp.maximum(m_old, jnp.max(s, axis=0, keepdims=True))
            p = jnp.exp2(s - m_new).astype(BF16)
            acc_sc[...] = jnp.exp2(m_old - m_new) * acc_sc[...] + jnp.dot(vt_ref[kc], p, preferred_element_type=F32)
            sig_sc[...] = m_new
            return carry

        lax.fori_loop(0, n_chunks, exact_chunk, 0)
        finish()


def _mla_attention(qt, k, vt, tk):
    b, h, s, _ = k.shape
    tq = 2 * tk
    assert vt.shape[-1] == tk and s % tq == 0
    return pl.pallas_call(
        functools.partial(_mla_attn_kernel, tq=tq, tk=tk),
        grid=(b, h, s // tq),
        in_specs=[pl.BlockSpec((None, None, QK_PAD, tq), lambda bb, hh, i: (bb, hh, 0, i)),
                  pl.BlockSpec((None, None, s, QK_PAD), lambda bb, hh, i: (bb, hh, 0, 0)),
                  pl.BlockSpec((None, None, s // tk, VT_ROWS, tk), lambda bb, hh, i: (bb, hh, 0, 0, 0))],
        out_specs=pl.BlockSpec((None, tq, V_HEAD), lambda bb, hh, i: (bb, i, hh)),
        out_shape=jax.ShapeDtypeStruct((b, s, MLA_WIDTH), BF16),
        scratch_shapes=[pltpu.VMEM((1, tq), F32), pltpu.VMEM((1, tq), F32), pltpu.VMEM((VT_ROWS, tq), F32)],
        name="a_mla_attention",
        compiler_params=_params(("arbitrary", "arbitrary", "arbitrary")),
    )(qt, k, vt)


def _a_out_kernel(o_ref, sg_ref, x_ref, w_ref, lnkv_ref, lnb_ref, h_ref, xkv_ref, xb_ref):
    tm = o_ref.shape[0]
    for c in range(tm // OUT_ROW_CHUNK):
        rows = slice(c * OUT_ROW_CHUNK, (c + 1) * OUT_ROW_CHUNK)
        a = (o_ref[rows, :] * sg_ref[rows, :]).astype(BF16)
        h = x_ref[rows, :] + jnp.dot(a, w_ref[...], preferred_element_type=F32)
        h_ref[rows, :] = h
        hn = h * _inv_rms(h, D_MODEL)
        xkv_ref[rows, :] = (hn * lnkv_ref[...]).astype(BF16)
        xb_ref[rows, :] = (hn * lnb_ref[...]).astype(BF16)


def _a_out(o2, sg, x2, w, ln_kv, ln_b, tm):
    m = x2.shape[0]
    row = pl.BlockSpec((tm, D_MODEL), lambda i: (i, 0))
    full = lambda a: pl.BlockSpec(a.shape, lambda i: (0, 0))
    return pl.pallas_call(
        _a_out_kernel,
        grid=(m // tm,),
        in_specs=[row, row, row, full(w), full(ln_kv), full(ln_b)],
        out_specs=[row, row, row],
        out_shape=[jax.ShapeDtypeStruct((m, D_MODEL), F32), jax.ShapeDtypeStruct((m, D_MODEL), BF16),
                   jax.ShapeDtypeStruct((m, D_MODEL), BF16)],
        name="a_out_proj",
        compiler_params=_params(("arbitrary",)),
    )(o2, sg, x2, w, ln_kv, ln_b)


def _branch_proj_kernel(x_ref, w_ref, g_ref, o_ref, y_sc, y2_sc, *, mode, dilation, scale):
    tm, tn = o_ref.shape
    rows = PROJ_ROW_CHUNK // dilation
    for c in range(tm // PROJ_ROW_CHUNK):
        y = jnp.dot(x_ref[c * PROJ_ROW_CHUNK:(c + 1) * PROJ_ROW_CHUNK, :], w_ref[...], preferred_element_type=F32)
        for hc in range(tn // DIL_HEAD_DIM):
            sl = slice(hc * DIL_HEAD_DIM, (hc + 1) * DIL_HEAD_DIM)
            yh = y[:, sl]
            if mode == "norm":
                yh = (yh * _inv_rms(yh, DIL_HEAD_DIM)) * g_ref[:, sl]
                if scale != 1.0:
                    yh = yh * scale
            elif mode == "silu":
                yh = yh * jax.nn.sigmoid(yh)
            if dilation == 1:
                o_ref[c * PROJ_ROW_CHUNK:(c + 1) * PROJ_ROW_CHUNK, sl] = yh.astype(BF16)
            else:
                base = c * PROJ_ROW_CHUNK
                y_sc[hc, base:base + PROJ_ROW_CHUNK, :] = yh
                if dilation == STRIDE_STEP * STRIDE_STEP:
                    quarter = PROJ_ROW_CHUNK // STRIDE_STEP
                    for a in range(STRIDE_STEP):
                        y2_sc[hc, base + a * quarter:base + (a + 1) * quarter, :] = (
                            y_sc[hc, pl.ds(base + a, quarter, stride=STRIDE_STEP), :])
                    for r in range(dilation):
                        a, b = r % STRIDE_STEP, r // STRIDE_STEP
                        dst = r * (tm // dilation) + c * rows
                        o_ref[dst:dst + rows, sl] = (
                            y2_sc[hc, pl.ds(base + a * quarter + b, rows, stride=STRIDE_STEP), :].astype(BF16))
                else:
                    for r in range(dilation):
                        dst = r * (tm // dilation) + c * rows
                        o_ref[dst:dst + rows, sl] = y_sc[hc, pl.ds(base + r, rows, stride=dilation), :].astype(BF16)


def _branch_proj(xn, w, g_cols, col0, mode, dilation, scale, name):
    m, kdim = xn.shape
    tm, tn = DIL_TILE, PROJ_COL_TILE
    blk0 = col0 // tn
    return pl.pallas_call(
        functools.partial(_branch_proj_kernel, mode=mode, dilation=dilation, scale=scale),
        grid=(m // tm, DIL_WIDTH // tn),
        in_specs=[pl.BlockSpec((tm, kdim), lambda i, j: (i, 0)),
                  pl.BlockSpec((kdim, tn), lambda i, j: (0, blk0 + j)),
                  pl.BlockSpec((1, tn), lambda i, j: (0, blk0 + j))],
        out_specs=pl.BlockSpec((tm, tn), lambda i, j: (i, j)),
        out_shape=jax.ShapeDtypeStruct((m, DIL_WIDTH), BF16),
        scratch_shapes=[pltpu.VMEM((tn // DIL_HEAD_DIM, tm, DIL_HEAD_DIM), F32)] * 2,
        name=name,
        compiler_params=_params(("arbitrary", "arbitrary")),
    )(xn, w, g_cols)


def _dilated_attn_kernel(*refs):
    qs, kps, kcs, vps, vcs = (refs[i * N_GROUPS:(i + 1) * N_GROUPS] for i in range(5))
    sg_ref, out_ref, o_sc, lse_sc = refs[5 * N_GROUPS:]
    t = pl.program_id(1)
    c = pl.program_id(2)
    row = lax.broadcasted_iota(jnp.int32, (BAND_BLK, 2 * BAND_BLK), 0)
    col = lax.broadcasted_iota(jnp.int32, (BAND_BLK, 2 * BAND_BLK), 1)
    rel = row + BAND_BLK - col
    band = (rel >= 0) & (rel <= LOOKBACK)
    first_band = band & (col >= jnp.where(t > 0, 0, BAND_BLK))
    relf = rel.astype(F32)
    for g, (_, dil) in enumerate(DIL_GROUPS):
        nblk = DIL_TILE // dil // BAND_BLK
        q_ref, kp_ref, kc_ref, vp_ref, vc_ref = qs[g], kps[g], kcs[g], vps[g], vcs[g]
        for hh in range(DIL_HEAD_CHUNK):
            hs = slice(hh * DIL_HEAD_DIM, (hh + 1) * DIL_HEAD_DIM)
            head = (c * DIL_HEAD_CHUNK + hh + 1).astype(F32)
            slope = jnp.exp2(jnp.full((1, 2 * BAND_BLK), -ALIBI_MAX / DIL_HEADS, F32) * head)
            bias = (slope * (-float(dil) * LOG2E)) * relf
            bias_band = jnp.where(band, bias, NEG_BIG)
            bias_first = jnp.where(first_band, bias, NEG_BIG)
            for r in range(dil):
                for a in range(nblk):
                    q = q_ref[r, a, :, hs]
                    k_lo = kp_ref[r, :, hs] if a == 0 else kc_ref[r, a - 1, :, hs]
                    v_lo = vp_ref[r, :, hs] if a == 0 else vc_ref[r, a - 1, :, hs]
                    k = jnp.concatenate([k_lo, kc_ref[r, a, :, hs]], axis=0)
                    v = jnp.concatenate([v_lo, vc_ref[r, a, :, hs]], axis=0)
                    s = lax.dot_general(q, k, (((1,), (1,)), ((), ())), preferred_element_type=F32)
                    s = s + (bias_first if a == 0 else bias_band)
                    m = jnp.max(s, axis=-1, keepdims=True)
                    p = jnp.exp2(s - m)
                    den = jnp.sum(p, axis=-1, keepdims=True)
                    o = jnp.dot(p.astype(BF16), v, preferred_element_type=F32) / den
                    tok = pl.ds(a * BAND_BLK * dil + r, BAND_BLK, stride=dil) if dil > 1 else pl.ds(a * BAND_BLK, BAND_BLK)
                    o_sc[g, hh, tok, :] = o
                    lse_sc[g, hh, tok, :] = jnp.broadcast_to(m + jnp.log2(den), (BAND_BLK, V7X_LANES))
    for hh in range(DIL_HEAD_CHUNK):
        hs = slice(hh * DIL_HEAD_DIM, (hh + 1) * DIL_HEAD_DIM)
        l0, l1, l2 = lse_sc[0, hh], lse_sc[1, hh], lse_sc[2, hh]
        mx = jnp.maximum(jnp.maximum(l0, l1), l2)
        e0, e1, e2 = jnp.exp2(l0 - mx), jnp.exp2(l1 - mx), jnp.exp2(l2 - mx)
        o = (e0 * o_sc[0, hh] + e1 * o_sc[1, hh] + e2 * o_sc[2, hh]) / (e0 + e1 + e2)
        out_ref[:, hs] = (o * sg_ref[:, hs].astype(F32)).astype(BF16)


def _dilated_attention(qs, ks, vs, gate, batch, seq):
    nt = seq // DIL_TILE
    ins, specs = [], []

    def add(arrs, prev):
        for arr, (_, dil) in zip(arrs, DIL_GROUPS):
            nblk = DIL_TILE // dil // BAND_BLK
            ins.append(arr.reshape(batch, nt, dil, nblk, BAND_BLK, DIL_WIDTH))
            if prev:
                specs.append(pl.BlockSpec((None, None, dil, None, BAND_BLK, DIL_COLS),
                                          lambda b, t, c, nblk=nblk: (b, jnp.maximum(t - 1, 0), 0, nblk - 1, 0, c)))
            else:
                specs.append(pl.BlockSpec((None, None, dil, nblk, BAND_BLK, DIL_COLS),
                                          lambda b, t, c: (b, t, 0, 0, 0, c)))

    add(qs, False)
    add(ks, True)
    add(ks, False)
    add(vs, True)
    add(vs, False)
    ins.append(gate.reshape(batch, nt, DIL_TILE, DIL_WIDTH))
    specs.append(pl.BlockSpec((None, None, DIL_TILE, DIL_COLS), lambda b, t, c: (b, t, 0, c)))
    out = pl.pallas_call(
        _dilated_attn_kernel,
        grid=(batch, nt, DIL_HEADS // DIL_HEAD_CHUNK),
        in_specs=specs,
        out_specs=pl.BlockSpec((None, None, DIL_TILE, DIL_COLS), lambda b, t, c: (b, t, 0, c)),
        out_shape=jax.ShapeDtypeStruct((batch, nt, DIL_TILE, DIL_WIDTH), BF16),
        scratch_shapes=[pltpu.VMEM((N_GROUPS, DIL_HEAD_CHUNK, DIL_TILE, DIL_HEAD_DIM), F32),
                        pltpu.VMEM((N_GROUPS, DIL_HEAD_CHUNK, DIL_TILE, V7X_LANES), F32)],
        name="b_dilated_attention",
        compiler_params=_params(("arbitrary", "arbitrary", "arbitrary")),
    )(*ins)
    return out.reshape(batch * seq, DIL_WIDTH)


def _b_out_kernel(a_ref, h_ref, w_ref, out_ref):
    out_ref[...] = h_ref[...] + jnp.dot(a_ref[...], w_ref[...], preferred_element_type=F32)


def _b_out(a, h1, w, tm):
    m = h1.shape[0]
    row = pl.BlockSpec((tm, D_MODEL), lambda i: (i, 0))
    return pl.pallas_call(
        _b_out_kernel,
        grid=(m // tm,),
        in_specs=[row, row, pl.BlockSpec(w.shape, lambda i: (0, 0))],
        out_specs=row,
        out_shape=jax.ShapeDtypeStruct((m, D_MODEL), F32),
        name="b_out_proj",
        compiler_params=_params(("arbitrary",)),
    )(a, h1, w)


def _tile(n, want):
    t = min(n, want)
    assert n % t == 0, (n, t)
    return t


def kernel(x, a_ln, a_w_in, a_ln_q, a_w_q_up, a_ln_kv, a_w_kv_up, a_q_norm, a_k_norm, a_w_o,
           kv_ln, kv_w, kv_k_norm, b_ln, b_w_in, b_q_norm, b_w_o):
    batch, seq, d = x.shape
    assert d == D_MODEL and a_ln.shape[0] == 1 and b_ln.shape[0] == 1
    assert seq % DIL_TILE == 0
    m = batch * seq
    x2 = x.reshape(m, d)

    lat = Q_LORA + KV_LORA
    w_in = a_w_in[0]
    w_in_pad = jnp.concatenate(
        [w_in[:, :lat], w_in[:, lat + QK_ROPE:], w_in[:, lat:lat + QK_ROPE],
         jnp.zeros((d, V7X_LANES - QK_ROPE), F32)], axis=1).astype(BF16)
    assert w_in_pad.shape == (d, A_IN_PAD)
    w_q_t = a_w_q_up[0].T.astype(BF16)
    w_kv = a_w_kv_up[0].astype(BF16)
    gq_rows = jnp.broadcast_to(a_q_norm[0][:, None], (QK_HEAD, V7X_LANES))
    gk_pad = jnp.pad(a_k_norm[0], (0, QK_PAD - QK_HEAD))[None, :]
    w_o_a = a_w_o[0].astype(BF16)
    w_kvs = kv_w.astype(BF16)
    w_b_in = b_w_in[0].astype(BF16)
    w_o_b = b_w_o[0].astype(BF16)
    k_cols = N_GROUPS * DIL_WIDTH
    gk_cols = jnp.concatenate([jnp.tile(kv_k_norm, (1, DIL_HEADS)).reshape(1, k_cols),
                               jnp.ones((1, k_cols), F32)], axis=1)
    gq_cols = jnp.concatenate([jnp.tile(b_q_norm[0], (1, DIL_HEADS)).reshape(1, k_cols),
                               jnp.ones((1, DIL_WIDTH), F32)], axis=1)

    tabs, tabs_t = _rope_tables(seq)
    cqt, ckv, sg_a, kpe = _a_in(x2, a_ln, w_in_pad, a_ln_q, a_ln_kv, _tile(m, A_IN_ROW_TILE))
    blk = _tile(seq, MLA_KEY_CHUNK)
    q = _q_up(cqt, w_q_t, gq_rows, tabs_t, batch, seq, blk)
    k, vt = _kv_up(ckv.reshape(batch, seq, KV_LORA), kpe.reshape(batch, seq, V7X_LANES), w_kv, gk_pad, tabs, blk)
    o_a = _mla_attention(q, k, vt, blk)
    h1, xn_kv, xn_b = _a_out(o_a.reshape(m, MLA_WIDTH), sg_a, x2, w_o_a, kv_ln[None, :], b_ln,
                             _tile(m, 2 * OUT_ROW_CHUNK))

    q_scale = DIL_HEAD_DIM ** -0.5 * LOG2E
    qs, ks, vs = [], [], []
    for g, (_, dil) in enumerate(DIL_GROUPS):
        ks.append(_branch_proj(xn_kv, w_kvs, gk_cols, g * DIL_WIDTH, "norm", dil, 1.0, f"b_k_proj_d{dil}"))
        vs.append(_branch_proj(xn_kv, w_kvs, gk_cols, k_cols + g * DIL_WIDTH, "copy", dil, 1.0, f"b_v_proj_d{dil}"))
        qs.append(_branch_proj(xn_b, w_b_in, gq_cols, g * DIL_WIDTH, "norm", dil, q_scale, f"b_q_proj_d{dil}"))
    sg_b = _branch_proj(xn_b, w_b_in, gq_cols, k_cols, "silu", 1, 1.0, "b_gate_proj")
    a_b = _dilated_attention(qs, ks, vs, sg_b, batch, seq)
    out = _b_out(a_b, h1, w_o_b, _tile(m, OUT_ROW_CHUNK))
    return out.reshape(batch, seq, d)
```

```python
import functools
import math

import jax
import jax.numpy as jnp
from jax import lax
from jax.experimental import pallas as pl
from jax.experimental.pallas import tpu as pltpu

F32 = jnp.float32
BF16 = jnp.bfloat16

V7X_LANES = 128
V7X_VMEM_BYTES = 64 * 1024 * 1024
VMEM_LIMIT_BYTES = V7X_VMEM_BYTES * 7 // 8

D_MODEL = 2048
MLA_HEADS = 16
Q_LORA = 512
KV_LORA = 512
QK_NOPE = 128
QK_ROPE = 64
QK_HEAD = QK_NOPE + QK_ROPE
V_HEAD = 128
MLA_WIDTH = MLA_HEADS * V_HEAD
ROPE_THETA = 10000.0
DIL_GROUPS = ((128, 1), (512, 4), (2048, 16))
N_GROUPS = 3
DIL_HEADS = 16
DIL_HEAD_DIM = 128
DIL_WIDTH = DIL_HEADS * DIL_HEAD_DIM
LOOKBACK = 128
BAND_BLK = 128
ALIBI_MAX = 8.0
EPS = 1e-6
NEG_BIG = -1e30
LOG2E = math.log2(math.e)

QK_PAD = 2 * V7X_LANES
A_IN_PAD = Q_LORA + KV_LORA + MLA_WIDTH + V7X_LANES

DIL_TILE = BAND_BLK * DIL_GROUPS[-1][1]
DIL_HEAD_CHUNK = 2
DIL_COLS = DIL_HEAD_CHUNK * DIL_HEAD_DIM
V7X_BF16_SUBLANES = 16
A_IN_ROW_TILE = 512
A_IN_ROW_CHUNK = 256
OUT_ROW_CHUNK = 256
MLA_KEY_CHUNK = 512
MLA_UNROLL = 8
VT_ROWS = V_HEAD + V7X_BF16_SUBLANES
FIRST_SHIFT_KEYS = V7X_BF16_SUBLANES
LAG_LIMIT = 64.0
STRIDE_STEP = 4
PROJ_ROW_CHUNK = 256
PROJ_COL_TILE = 512


def _params(semantics):
    return pltpu.CompilerParams(dimension_semantics=semantics, vmem_limit_bytes=VMEM_LIMIT_BYTES)


def _inv_rms(x, n):
    return lax.rsqrt(jnp.sum(x * x, axis=-1, keepdims=True) * (1.0 / n) + EPS)


def _rope(rv, cos_t, s1_t, s2_t):
    return rv * cos_t + pltpu.roll(rv, 96, 1) * s1_t + pltpu.roll(rv, 32, 1) * s2_t


def _rope_table_kernel(inv_ref, invt_ref, cos_ref, s1_ref, s2_ref, cost_ref, sint_ref):
    ts = cos_ref.shape[0]
    row = lax.broadcasted_iota(jnp.int32, (ts, V7X_LANES), 0) + pl.program_id(0) * ts
    lane = lax.broadcasted_iota(jnp.int32, (ts, V7X_LANES), 1)
    ang = row.astype(F32) * inv_ref[...]
    c = jnp.cos(ang)
    s = jnp.sin(ang)
    half = QK_ROPE // 2
    cos_ref[...] = jnp.where(lane < QK_ROPE, c, 0.0)
    s1_ref[...] = jnp.where(lane < half, -s, 0.0)
    s2_ref[...] = jnp.where((lane >= half) & (lane < QK_ROPE), s, 0.0)
    pos = lax.broadcasted_iota(jnp.int32, (half, ts), 1) + pl.program_id(0) * ts
    ang_t = pos.astype(F32) * jnp.tile(invt_ref[...], (1, ts // V7X_LANES))
    cost_ref[...] = jnp.cos(ang_t)
    sint_ref[...] = jnp.sin(ang_t)


def _rope_tables(seq):
    half = QK_ROPE // 2
    inv = ROPE_THETA ** (-jnp.arange(half, dtype=F32) / half)
    inv_lanes = jnp.concatenate([inv, inv, jnp.zeros((V7X_LANES - QK_ROPE,), F32)])[None, :]
    inv_rows = jnp.broadcast_to(inv[:, None], (half, V7X_LANES))
    ts = min(seq, 1024)
    tab = jax.ShapeDtypeStruct((seq, V7X_LANES), F32)
    tab_t = jax.ShapeDtypeStruct((half, seq), F32)
    spec = pl.BlockSpec((ts, V7X_LANES), lambda i: (i, 0))
    spec_t = pl.BlockSpec((half, ts), lambda i: (0, i))
    outs = pl.pallas_call(
        _rope_table_kernel,
        grid=(seq // ts,),
        in_specs=[pl.BlockSpec((1, V7X_LANES), lambda i: (0, 0)), pl.BlockSpec((half, V7X_LANES), lambda i: (0, 0))],
        out_specs=[spec, spec, spec, spec_t, spec_t],
        out_shape=[tab, tab, tab, tab_t, tab_t],
        name="rope_tables",
        compiler_params=_params(("arbitrary",)),
    )(inv_lanes, inv_rows)
    return outs[:3], outs[3:]


def _a_in_kernel(x_ref, ln_ref, w_ref, lnq_ref, lnkv_ref, cq_ref, ckv_ref, sg_ref, kpe_ref):
    g0 = Q_LORA + KV_LORA
    for c in range(x_ref.shape[0] // A_IN_ROW_CHUNK):
        rows = slice(c * A_IN_ROW_CHUNK, (c + 1) * A_IN_ROW_CHUNK)
        x = x_ref[rows, :]
        xn = ((x * _inv_rms(x, D_MODEL)) * ln_ref[...]).astype(BF16)
        u = jnp.dot(xn, w_ref[...], preferred_element_type=F32)
        cq = u[:, :Q_LORA]
        cq_ref[:, rows] = ((cq * _inv_rms(cq, Q_LORA)) * lnq_ref[...]).T.astype(BF16)
        ckv = u[:, Q_LORA:g0]
        ckv_ref[rows, :] = ((ckv * _inv_rms(ckv, KV_LORA)) * lnkv_ref[...]).astype(BF16)
        gate = u[:, g0:g0 + MLA_WIDTH]
        sg_ref[rows, :] = (gate * jax.nn.sigmoid(gate)).astype(BF16)
        kpe_ref[rows, :] = u[:, g0 + MLA_WIDTH:]


def _a_in(x2, ln, w_pad, ln_q, ln_kv, tm):
    m = x2.shape[0]
    row = lambda n: pl.BlockSpec((tm, n), lambda i: (i, 0))
    full = lambda a: pl.BlockSpec(a.shape, lambda i: (0, 0))
    return pl.pallas_call(
        _a_in_kernel,
        grid=(m // tm,),
        in_specs=[row(D_MODEL), full(ln),
                  pl.BlockSpec(w_pad.shape, lambda i: (0, 0), pipeline_mode=pl.Buffered(1)),
                  full(ln_q), full(ln_kv)],
        out_specs=[pl.BlockSpec((Q_LORA, tm), lambda i: (0, i)), row(KV_LORA), row(MLA_WIDTH), row(V7X_LANES)],
        out_shape=[jax.ShapeDtypeStruct((Q_LORA, m), BF16), jax.ShapeDtypeStruct((m, KV_LORA), BF16),
                   jax.ShapeDtypeStruct((m, MLA_WIDTH), BF16), jax.ShapeDtypeStruct((m, V7X_LANES), F32)],
        name="a_in_proj",
        compiler_params=_params(("arbitrary",)),
    )(x2, ln, w_pad, ln_q, ln_kv)


def _q_up_kernel(cqt_ref, wt_ref, g_ref, cos_ref, sin_ref, q_ref, *, scale):
    yt = jnp.dot(wt_ref[...], cqt_ref[...], preferred_element_type=F32)
    tm = yt.shape[1]
    g = jnp.tile(g_ref[...], (1, tm // V7X_LANES))
    cos_t, sin_t = cos_ref[...], sin_ref[...]
    half = QK_ROPE // 2
    zeros = jnp.zeros((QK_PAD - QK_HEAD, tm), BF16)
    for h in range(MLA_HEADS):
        blk = yt[h * QK_HEAD:(h + 1) * QK_HEAD, :]
        r = lax.rsqrt(jnp.sum(blk * blk, axis=0, keepdims=True) * (1.0 / QK_HEAD) + EPS)
        z = (blk * r) * g
        x1, x2 = z[QK_NOPE:QK_NOPE + half, :], z[QK_NOPE + half:, :]
        q_ref[h, :QK_NOPE, :] = (z[:QK_NOPE, :] * scale).astype(BF16)
        q_ref[h, QK_NOPE:QK_NOPE + half, :] = ((x1 * cos_t - x2 * sin_t) * scale).astype(BF16)
        q_ref[h, QK_NOPE + half:QK_HEAD, :] = ((x2 * cos_t + x1 * sin_t) * scale).astype(BF16)
        q_ref[h, QK_HEAD:, :] = zeros


def _kv_up_kernel(ckv_ref, kpe_ref, w_ref, g_ref, cos_ref, s1_ref, s2_ref, k_ref, vt_ref):
    y = jnp.dot(ckv_ref[...], w_ref[...], preferred_element_type=F32)
    g_n = g_ref[:, :QK_NOPE]
    g_r = g_ref[:, QK_NOPE:]
    kpe = kpe_ref[...]
    ss_pe = jnp.sum(kpe * kpe, axis=-1, keepdims=True)
    cos_t, s1_t, s2_t = cos_ref[...], s1_ref[...], s2_ref[...]
    hw = QK_NOPE + V_HEAD
    pad_row = lax.broadcasted_iota(jnp.int32, (VT_ROWS - V_HEAD, y.shape[0]), 0)
    ones_rows = jnp.where(pad_row == 0, 1.0, 0.0).astype(BF16)
    kr = _rope(kpe * g_r, cos_t, s1_t, s2_t)
    for h in range(MLA_HEADS):
        kn = y[:, h * hw:h * hw + QK_NOPE]
        r = lax.rsqrt((jnp.sum(kn * kn, axis=-1, keepdims=True) + ss_pe) * (1.0 / QK_HEAD) + EPS)
        k_ref[h, :, :QK_NOPE] = ((kn * r) * g_n).astype(BF16)
        k_ref[h, :, QK_NOPE:] = (kr * r).astype(BF16)
        vt_ref[h, :V_HEAD, :] = y[:, h * hw + QK_NOPE:(h + 1) * hw].T.astype(BF16)
        vt_ref[h, V_HEAD:, :] = ones_rows


def _seq_spec(tm, n):
    return pl.BlockSpec((None, tm, n), lambda b, i: (b, i, 0))


def _tab_spec(tm):
    return pl.BlockSpec((tm, V7X_LANES), lambda b, i: (i, 0))


def _head_spec(tm, n):
    return pl.BlockSpec((None, MLA_HEADS, tm, n), lambda b, i: (b, 0, i, 0))


def _q_up(cqt, wt, g_rows, tabs_t, batch, seq, tm):
    full = lambda a: pl.BlockSpec(a.shape, lambda bb, i: (0, 0))
    scale = QK_HEAD ** -0.5 * LOG2E
    nblk = seq // tm
    tab = pl.BlockSpec((QK_ROPE // 2, tm), lambda bb, i: (0, i))
    return pl.pallas_call(
        functools.partial(_q_up_kernel, scale=scale),
        grid=(batch, nblk),
        in_specs=[pl.BlockSpec((Q_LORA, tm), lambda bb, i: (0, bb * nblk + i)), full(wt), full(g_rows), tab, tab],
        out_specs=pl.BlockSpec((None, MLA_HEADS, QK_PAD, tm), lambda bb, i: (bb, 0, 0, i)),
        out_shape=jax.ShapeDtypeStruct((batch, MLA_HEADS, QK_PAD, seq), BF16),
        name="a_q_up",
        compiler_params=_params(("arbitrary", "arbitrary")),
    )(cqt, wt, g_rows, *tabs_t)


def _kv_up(ckv3, kpe3, w, g_pad, tabs, tm):
    b, s, _ = ckv3.shape
    full = lambda a: pl.BlockSpec(a.shape, lambda bb, i: (0, 0))
    return pl.pallas_call(
        _kv_up_kernel,
        grid=(b, s // tm),
        in_specs=[_seq_spec(tm, KV_LORA), _seq_spec(tm, V7X_LANES), full(w), full(g_pad),
                  _tab_spec(tm), _tab_spec(tm), _tab_spec(tm)],
        out_specs=[_head_spec(tm, QK_PAD),
                   pl.BlockSpec((None, MLA_HEADS, None, VT_ROWS, tm), lambda bb, i: (bb, 0, i, 0, 0))],
        out_shape=[jax.ShapeDtypeStruct((b, MLA_HEADS, s, QK_PAD), BF16),
                   jax.ShapeDtypeStruct((b, MLA_HEADS, s // tm, VT_ROWS, tm), BF16)],
        name="a_kv_up",
        compiler_params=_params(("arbitrary", "arbitrary")),
    )(ckv3, kpe3, w, g_pad, *tabs)


def _mla_attn_kernel(q_ref, k_ref, vt_ref, o_ref, sig_sc, viol_sc, acc_sc, *, tq, tk):
    qi = pl.program_id(2)
    qt = q_ref[...]
    n_chunks = 2 * qi + 2

    def qk(kc):
        k = k_ref[pl.ds(pl.multiple_of(kc * tk, tk), tk), :]
        return jnp.dot(k, qt, preferred_element_type=F32)

    def causal(s, key_offset):
        key = lax.broadcasted_iota(jnp.int32, (tk, tq), 0) + key_offset
        qry = lax.broadcasted_iota(jnp.int32, (tk, tq), 1)
        return jnp.where(key <= qry, s, NEG_BIG)

    def finish():
        o_ref[...] = (acc_sc[:V_HEAD, :] / acc_sc[V_HEAD:V_HEAD + 1, :]).T.astype(BF16)

    def lagged_chunk(kc, key_offset):
        col0 = 0 if key_offset is None else key_offset
        cs = slice(col0, tq)
        k = k_ref[pl.ds(pl.multiple_of(kc * tk, tk), tk), :]
        s = jnp.dot(k, qt[:, cs], preferred_element_type=F32)
        if key_offset is not None:
            key = lax.broadcasted_iota(jnp.int32, s.shape, 0)
            qry = lax.broadcasted_iota(jnp.int32, s.shape, 1)
            s = jnp.where(key <= qry, s, NEG_BIG)
        shift = sig_sc[:, cs]
        mx = jnp.max(s, axis=0, keepdims=True)
        p = jnp.exp2(s - shift).astype(BF16)
        new_shift = jnp.maximum(shift, mx)
        pv = jnp.dot(vt_ref[kc], p, preferred_element_type=F32)
        acc_sc[:, cs] = (acc_sc[:, cs] + pv) * jnp.exp2(shift - new_shift)
        sig_sc[:, cs] = new_shift
        viol_sc[:, cs] = jnp.maximum(viol_sc[:, cs], mx - shift)

    s0 = jnp.dot(k_ref[:FIRST_SHIFT_KEYS, :], qt, preferred_element_type=F32)
    key0 = lax.broadcasted_iota(jnp.int32, s0.shape, 0) - qi * tq
    s0 = jnp.where(key0 <= lax.broadcasted_iota(jnp.int32, s0.shape, 1), s0, NEG_BIG)
    sig_sc[...] = jnp.max(s0, axis=0, keepdims=True)
    viol_sc[...] = jnp.zeros(viol_sc.shape, F32)
    acc_sc[...] = jnp.zeros(acc_sc.shape, F32)

    def octet(io, carry):
        for u in range(MLA_UNROLL):
            lagged_chunk(MLA_UNROLL * io + u, None)
        return carry

    per_iter = MLA_UNROLL // 2
    n_oct = qi // per_iter
    lax.fori_loop(0, n_oct, octet, 0)
    for rem in range(per_iter):
        @pl.when(qi % per_iter == rem)
        def _(rem=rem):
            for u in range(2 * rem):
                lagged_chunk(MLA_UNROLL * n_oct + u, None)
            lagged_chunk(2 * qi, 0)
            lagged_chunk(2 * qi + 1, tk)
    finish()

    @pl.when(jnp.max(viol_sc[...]) > LAG_LIMIT)
    def _():
        sig_sc[...] = jnp.full(sig_sc.shape, NEG_BIG, F32)
        acc_sc[...] = jnp.zeros(acc_sc.shape, F32)

        def exact_chunk(kc, carry):
            s = causal(qk(kc), kc * tk - qi * tq)
            m_old = sig_sc[...]
            m_new = jnp.maximum(m_old, jnp.max(s, axis=0, keepdims=True))
            p = jnp.exp2(s - m_new).astype(BF16)
            acc_sc[...] = jnp.exp2(m_old - m_new) * acc_sc[...] + jnp.dot(vt_ref[kc], p, preferred_element_type=F32)
            sig_sc[...] = m_new
            return carry

        lax.fori_loop(0, n_chunks, exact_chunk, 0)
        finish()


def _mla_attention(qt, k, vt, tk):
    b, h, s, _ = k.shape
    tq = 2 * tk
    assert vt.shape[-1] == tk and s % tq == 0
    return pl.pallas_call(
        functools.partial(_mla_attn_kernel, tq=tq, tk=tk),
        grid=(b, h, s // tq),
        in_specs=[pl.BlockSpec((None, None, QK_PAD, tq), lambda bb, hh, i: (bb, hh, 0, i)),
                  pl.BlockSpec((None, None, s, QK_PAD), lambda bb, hh, i: (bb, hh, 0, 0)),
                  pl.BlockSpec((None, None, s // tk, VT_ROWS, tk), lambda bb, hh, i: (bb, hh, 0, 0, 0))],
        out_specs=pl.BlockSpec((None, tq, V_HEAD), lambda bb, hh, i: (bb, i, hh)),
        out_shape=jax.ShapeDtypeStruct((b, s, MLA_WIDTH), BF16),
        scratch_shapes=[pltpu.VMEM((1, tq), F32), pltpu.VMEM((1, tq), F32), pltpu.VMEM((VT_ROWS, tq), F32)],
        name="a_mla_attention",
        compiler_params=_params(("arbitrary", "arbitrary", "arbitrary")),
    )(qt, k, vt)


def _a_out_kernel(o_ref, sg_ref, x_ref, w_ref, lnkv_ref, lnb_ref, h_ref, xkv_ref, xb_ref):
    tm = o_ref.shape[0]
    for c in range(tm // OUT_ROW_CHUNK):
        rows = slice(c * OUT_ROW_CHUNK, (c + 1) * OUT_ROW_CHUNK)
        a = (o_ref[rows, :] * sg_ref[rows, :]).astype(BF16)
        h = x_ref[rows, :] + jnp.dot(a, w_ref[...], preferred_element_type=F32)
        h_ref[rows, :] = h
        hn = h * _inv_rms(h, D_MODEL)
        xkv_ref[rows, :] = (hn * lnkv_ref[...]).astype(BF16)
        xb_ref[rows, :] = (hn * lnb_ref[...]).astype(BF16)


def _a_out(o2, sg, x2, w, ln_kv, ln_b, tm):
    m = x2.shape[0]
    row = pl.BlockSpec((tm, D_MODEL), lambda i: (i, 0))
    full = lambda a: pl.BlockSpec(a.shape, lambda i: (0, 0))
    return pl.pallas_call(
        _a_out_kernel,
        grid=(m // tm,),
        in_specs=[row, row, row, pl.BlockSpec(w.shape, lambda i: (0, 0), pipeline_mode=pl.Buffered(1)),
                  full(ln_kv), full(ln_b)],
        out_specs=[row, row, row],
        out_shape=[jax.ShapeDtypeStruct((m, D_MODEL), F32), jax.ShapeDtypeStruct((m, D_MODEL), BF16),
                   jax.ShapeDtypeStruct((m, D_MODEL), BF16)],
        name="a_out_proj",
        compiler_params=_params(("arbitrary",)),
    )(o2, sg, x2, w, ln_kv, ln_b)


def _branch_proj_kernel(x_ref, w_ref, g_ref, o_ref, y_sc, y2_sc, *, mode, dilation, scale):
    tm, tn = o_ref.shape
    rows = PROJ_ROW_CHUNK // dilation
    for c in range(tm // PROJ_ROW_CHUNK):
        y = jnp.dot(x_ref[c * PROJ_ROW_CHUNK:(c + 1) * PROJ_ROW_CHUNK, :], w_ref[...], preferred_element_type=F32)
        for hc in range(tn // DIL_HEAD_DIM):
            sl = slice(hc * DIL_HEAD_DIM, (hc + 1) * DIL_HEAD_DIM)
            yh = y[:, sl]
            if mode == "norm":
                yh = (yh * _inv_rms(yh, DIL_HEAD_DIM)) * g_ref[:, sl]
                if scale != 1.0:
                    yh = yh * scale
            elif mode == "silu":
                yh = yh * jax.nn.sigmoid(yh)
            if dilation == 1:
                o_ref[c * PROJ_ROW_CHUNK:(c + 1) * PROJ_ROW_CHUNK, sl] = yh.astype(BF16)
            else:
                base = c * PROJ_ROW_CHUNK
                y_sc[hc, base:base + PROJ_ROW_CHUNK, :] = yh
                if dilation == STRIDE_STEP * STRIDE_STEP:
                    quarter = PROJ_ROW_CHUNK // STRIDE_STEP
                    for a in range(STRIDE_STEP):
                        y2_sc[hc, base + a * quarter:base + (a + 1) * quarter, :] = (
                            y_sc[hc, pl.ds(base + a, quarter, stride=STRIDE_STEP), :])
                    for r in range(dilation):
                        a, b = r % STRIDE_STEP, r // STRIDE_STEP
                        dst = r * (tm // dilation) + c * rows
                        o_ref[dst:dst + rows, sl] = (
                            y2_sc[hc, pl.ds(base + a * quarter + b, rows, stride=STRIDE_STEP), :].astype(BF16))
                else:
                    for r in range(dilation):
                        dst = r * (tm // dilation) + c * rows
                        o_ref[dst:dst + rows, sl] = y_sc[hc, pl.ds(base + r, rows, stride=dilation), :].astype(BF16)


def _branch_proj(xn, w, g_cols, col0, mode, dilation, scale, name):
    m, kdim = xn.shape
    tm, tn = DIL_TILE, PROJ_COL_TILE
    blk0 = col0 // tn
    return pl.pallas_call(
        functools.partial(_branch_proj_kernel, mode=mode, dilation=dilation, scale=scale),
        grid=(m // tm, DIL_WIDTH // tn),
        in_specs=[pl.BlockSpec((tm, kdim), lambda i, j: (i, 0)),
                  pl.BlockSpec((kdim, tn), lambda i, j: (0, blk0 + j)),
                  pl.BlockSpec((1, tn), lambda i, j: (0, blk0 + j))],
        out_specs=pl.BlockSpec((tm, tn), lambda i, j: (i, j)),
        out_shape=jax.ShapeDtypeStruct((m, DIL_WIDTH), BF16),
        scratch_shapes=[pltpu.VMEM((tn // DIL_HEAD_DIM, tm, DIL_HEAD_DIM), F32)] * 2,
        name=name,
        compiler_params=_params(("arbitrary", "arbitrary")),
    )(xn, w, g_cols)


def _dilated_attn_kernel(*refs):
    qs, kps, kcs, vps, vcs = (refs[i * N_GROUPS:(i + 1) * N_GROUPS] for i in range(5))
    sg_ref, out_ref, o_sc, lse_sc = refs[5 * N_GROUPS:]
    t = pl.program_id(1)
    c = pl.program_id(2)
    row = lax.broadcasted_iota(jnp.int32, (BAND_BLK, 2 * BAND_BLK), 0)
    col = lax.broadcasted_iota(jnp.int32, (BAND_BLK, 2 * BAND_BLK), 1)
    rel = row + BAND_BLK - col
    band = (rel >= 0) & (rel <= LOOKBACK)
    first_band = band & (col >= jnp.where(t > 0, 0, BAND_BLK))
    relf = rel.astype(F32)
    for g, (_, dil) in enumerate(DIL_GROUPS):
        nblk = DIL_TILE // dil // BAND_BLK
        q_ref, kp_ref, kc_ref, vp_ref, vc_ref = qs[g], kps[g], kcs[g], vps[g], vcs[g]
        for hh in range(DIL_HEAD_CHUNK):
            hs = slice(hh * DIL_HEAD_DIM, (hh + 1) * DIL_HEAD_DIM)
            head = (c * DIL_HEAD_CHUNK + hh + 1).astype(F32)
            slope = jnp.exp2(jnp.full((1, 2 * BAND_BLK), -ALIBI_MAX / DIL_HEADS, F32) * head)
            bias = (slope * (-float(dil) * LOG2E)) * relf
            bias_band = jnp.where(band, bias, NEG_BIG)
            bias_first = jnp.where(first_band, bias, NEG_BIG)
            for r in range(dil):
                for a in range(nblk):
                    q = q_ref[r, a, :, hs]
                    k_lo = kp_ref[r, :, hs] if a == 0 else kc_ref[r, a - 1, :, hs]
                    v_lo = vp_ref[r, :, hs] if a == 0 else vc_ref[r, a - 1, :, hs]
                    k = jnp.concatenate([k_lo, kc_ref[r, a, :, hs]], axis=0)
                    v = jnp.concatenate([v_lo, vc_ref[r, a, :, hs]], axis=0)
                    s = lax.dot_general(q, k, (((1,), (1,)), ((), ())), preferred_element_type=F32)
                    s = s + (bias_first if a == 0 else bias_band)
                    m = jnp.max(s, axis=-1, keepdims=True)
                    p = jnp.exp2(s - m)
                    den = jnp.sum(p, axis=-1, keepdims=True)
                    o = jnp.dot(p.astype(BF16), v, preferred_element_type=F32) / den
                    tok = pl.ds(a * BAND_BLK * dil + r, BAND_BLK, stride=dil) if dil > 1 else pl.ds(a * BAND_BLK, BAND_BLK)
                    o_sc[g, hh, tok, :] = o
                    lse_sc[g, hh, tok, :] = jnp.broadcast_to(m + jnp.log2(den), (BAND_BLK, V7X_LANES))
    for hh in range(DIL_HEAD_CHUNK):
        hs = slice(hh * DIL_HEAD_DIM, (hh + 1) * DIL_HEAD_DIM)
        l0, l1, l2 = lse_sc[0, hh], lse_sc[1, hh], lse_sc[2, hh]
        mx = jnp.maximum(jnp.maximum(l0, l1), l2)
        e0, e1, e2 = jnp.exp2(l0 - mx), jnp.exp2(l1 - mx), jnp.exp2(l2 - mx)
        o = (e0 * o_sc[0, hh] + e1 * o_sc[1, hh] + e2 * o_sc[2, hh]) / (e0 + e1 + e2)
        out_ref[:, hs] = (o * sg_ref[:, hs].astype(F32)).astype(BF16)


def _dilated_attention(qs, ks, vs, gate, batch, seq):
    nt = seq // DIL_TILE
    ins, specs = [], []

    def add(arrs, prev):
        for arr, (_, dil) in zip(arrs, DIL_GROUPS):
            nblk = DIL_TILE // dil // BAND_BLK
            ins.append(arr.reshape(batch, nt, dil, nblk, BAND_BLK, DIL_WIDTH))
            if prev:
                specs.append(pl.BlockSpec((None, None, dil, None, BAND_BLK, DIL_COLS),
                                          lambda b, t, c, nblk=nblk: (b, jnp.maximum(t - 1, 0), 0, nblk - 1, 0, c)))
            else:
                specs.append(pl.BlockSpec((None, None, dil, nblk, BAND_BLK, DIL_COLS),
                                          lambda b, t, c: (b, t, 0, 0, 0, c)))

    add(qs, False)
    add(ks, True)
    add(ks, False)
    add(vs, True)
    add(vs, False)
    ins.append(gate.reshape(batch, nt, DIL_TILE, DIL_WIDTH))
    specs.append(pl.BlockSpec((None, None, DIL_TILE, DIL_COLS), lambda b, t, c: (b, t, 0, c)))
    out = pl.pallas_call(
        _dilated_attn_kernel,
        grid=(batch, nt, DIL_HEADS // DIL_HEAD_CHUNK),
        in_specs=specs,
        out_specs=pl.BlockSpec((None, None, DIL_TILE, DIL_COLS), lambda b, t, c: (b, t, 0, c)),
        out_shape=jax.ShapeDtypeStruct((batch, nt, DIL_TILE, DIL_WIDTH), BF16),
        scratch_shapes=[pltpu.VMEM((N_GROUPS, DIL_HEAD_CHUNK, DIL_TILE, DIL_HEAD_DIM), F32),
                        pltpu.VMEM((N_GROUPS, DIL_HEAD_CHUNK, DIL_TILE, V7X_LANES), F32)],
        name="b_dilated_attention",
        compiler_params=_params(("arbitrary", "arbitrary", "arbitrary")),
    )(*ins)
    return out.reshape(batch * seq, DIL_WIDTH)


def _b_out_kernel(a_ref, h_ref, w_ref, out_ref):
    for c in range(out_ref.shape[0] // OUT_ROW_CHUNK):
        rows = slice(c * OUT_ROW_CHUNK, (c + 1) * OUT_ROW_CHUNK)
        out_ref[rows, :] = h_ref[rows, :] + jnp.dot(a_ref[rows, :], w_ref[...], preferred_element_type=F32)


def _b_out(a, h1, w, tm):
    m = h1.shape[0]
    row = pl.BlockSpec((tm, D_MODEL), lambda i: (i, 0))
    return pl.pallas_call(
        _b_out_kernel,
        grid=(m // tm,),
        in_specs=[row, row, pl.BlockSpec(w.shape, lambda i: (0, 0), pipeline_mode=pl.Buffered(1))],
        out_specs=row,
        out_shape=jax.ShapeDtypeStruct((m, D_MODEL), F32),
        name="b_out_proj",
        compiler_params=_params(("arbitrary",)),
    )(a, h1, w)


def _tile(n, want):
    t = min(n, want)
    assert n % t == 0, (n, t)
    return t


def kernel(x, a_ln, a_w_in, a_ln_q, a_w_q_up, a_ln_kv, a_w_kv_up, a_q_norm, a_k_norm, a_w_o,
           kv_ln, kv_w, kv_k_norm, b_ln, b_w_in, b_q_norm, b_w_o):
    batch, seq, d = x.shape
    assert d == D_MODEL and a_ln.shape[0] == 1 and b_ln.shape[0] == 1
    assert seq % DIL_TILE == 0
    m = batch * seq
    x2 = x.reshape(m, d)

    lat = Q_LORA + KV_LORA
    w_in = a_w_in[0]
    w_in_pad = jnp.concatenate(
        [w_in[:, :lat], w_in[:, lat + QK_ROPE:], w_in[:, lat:lat + QK_ROPE],
         jnp.zeros((d, V7X_LANES - QK_ROPE), F32)], axis=1).astype(BF16)
    assert w_in_pad.shape == (d, A_IN_PAD)
    w_q_t = a_w_q_up[0].T.astype(BF16)
    w_kv = a_w_kv_up[0].astype(BF16)
    gq_rows = jnp.broadcast_to(a_q_norm[0][:, None], (QK_HEAD, V7X_LANES))
    gk_pad = jnp.pad(a_k_norm[0], (0, QK_PAD - QK_HEAD))[None, :]
    w_o_a = a_w_o[0].astype(BF16)
    w_kvs = kv_w.astype(BF16)
    w_b_in = b_w_in[0].astype(BF16)
    w_o_b = b_w_o[0].astype(BF16)
    k_cols = N_GROUPS * DIL_WIDTH
    gk_cols = jnp.concatenate([jnp.tile(kv_k_norm, (1, DIL_HEADS)).reshape(1, k_cols),
                               jnp.ones((1, k_cols), F32)], axis=1)
    gq_cols = jnp.concatenate([jnp.tile(b_q_norm[0], (1, DIL_HEADS)).reshape(1, k_cols),
                               jnp.ones((1, DIL_WIDTH), F32)], axis=1)

    tabs, tabs_t = _rope_tables(seq)
    cqt, ckv, sg_a, kpe = _a_in(x2, a_ln, w_in_pad, a_ln_q, a_ln_kv, _tile(m, A_IN_ROW_TILE))
    blk = _tile(seq, MLA_KEY_CHUNK)
    q = _q_up(cqt, w_q_t, gq_rows, tabs_t, batch, seq, blk)
    k, vt = _kv_up(ckv.reshape(batch, seq, KV_LORA), kpe.reshape(batch, seq, V7X_LANES), w_kv, gk_pad, tabs, blk)
    o_a = _mla_attention(q, k, vt, blk)
    h1, xn_kv, xn_b = _a_out(o_a.reshape(m, MLA_WIDTH), sg_a, x2, w_o_a, kv_ln[None, :], b_ln,
                             _tile(m, 2 * OUT_ROW_CHUNK))

    q_scale = DIL_HEAD_DIM ** -0.5 * LOG2E
    qs, ks, vs = [], [], []
    for g, (_, dil) in enumerate(DIL_GROUPS):
        ks.append(_branch_proj(xn_kv, w_kvs, gk_cols, g * DIL_WIDTH, "norm", dil, 1.0, f"b_k_proj_d{dil}"))
        vs.append(_branch_proj(xn_kv, w_kvs, gk_cols, k_cols + g * DIL_WIDTH, "copy", dil, 1.0, f"b_v_proj_d{dil}"))
        qs.append(_branch_proj(xn_b, w_b_in, gq_cols, g * DIL_WIDTH, "norm", dil, q_scale, f"b_q_proj_d{dil}"))
    sg_b = _branch_proj(xn_b, w_b_in, gq_cols, k_cols, "silu", 1, 1.0, "b_gate_proj")
    a_b = _dilated_attention(qs, ks, vs, sg_b, batch, seq)
    out = _b_out(a_b, h1, w_o_b, _tile(m, 2 * OUT_ROW_CHUNK))
    return out.reshape(batch, seq, d)
```

```python
import functools
import math

import jax
import jax.numpy as jnp
from jax import lax
from jax.experimental import pallas as pl
from jax.experimental.pallas import tpu as pltpu

F32 = jnp.float32
BF16 = jnp.bfloat16

V7X_LANES = 128
V7X_VMEM_BYTES = 64 * 1024 * 1024
VMEM_LIMIT_BYTES = V7X_VMEM_BYTES * 7 // 8

D_MODEL = 2048
MLA_HEADS = 16
Q_LORA = 512
KV_LORA = 512
QK_NOPE = 128
QK_ROPE = 64
QK_HEAD = QK_NOPE + QK_ROPE
V_HEAD = 128
MLA_WIDTH = MLA_HEADS * V_HEAD
ROPE_THETA = 10000.0
DIL_GROUPS = ((128, 1), (512, 4), (2048, 16))
N_GROUPS = 3
DIL_HEADS = 16
DIL_HEAD_DIM = 128
DIL_WIDTH = DIL_HEADS * DIL_HEAD_DIM
LOOKBACK = 128
BAND_BLK = 128
ALIBI_MAX = 8.0
EPS = 1e-6
NEG_BIG = -1e30
LOG2E = math.log2(math.e)

QK_PAD = 2 * V7X_LANES
A_IN_PAD = Q_LORA + KV_LORA + MLA_WIDTH + V7X_LANES

DIL_TILE = BAND_BLK * DIL_GROUPS[-1][1]
DIL_HEAD_CHUNK = 2
DIL_COLS = DIL_HEAD_CHUNK * DIL_HEAD_DIM
V7X_BF16_SUBLANES = 16
A_IN_ROW_TILE = 512
A_IN_ROW_CHUNK = 256
OUT_ROW_CHUNK = 256
MLA_KEY_CHUNK = 512
MLA_UNROLL = 8
VT_ROWS = V_HEAD + V7X_BF16_SUBLANES
FIRST_SHIFT_KEYS = V7X_BF16_SUBLANES
LAG_LIMIT = 64.0
STRIDE_STEP = 4
PROJ_ROW_CHUNK = 256
PROJ_COL_TILE = 512


def _params(semantics):
    return pltpu.CompilerParams(dimension_semantics=semantics, vmem_limit_bytes=VMEM_LIMIT_BYTES)


def _inv_rms(x, n):
    return lax.rsqrt(jnp.sum(x * x, axis=-1, keepdims=True) * (1.0 / n) + EPS)


def _rope(rv, cos_t, s1_t, s2_t):
    return rv * cos_t + pltpu.roll(rv, 96, 1) * s1_t + pltpu.roll(rv, 32, 1) * s2_t


def _rope_table_kernel(inv_ref, invt_ref, cos_ref, s1_ref, s2_ref, cost_ref, sint_ref):
    ts = cos_ref.shape[0]
    row = lax.broadcasted_iota(jnp.int32, (ts, V7X_LANES), 0) + pl.program_id(0) * ts
    lane = lax.broadcasted_iota(jnp.int32, (ts, V7X_LANES), 1)
    ang = row.astype(F32) * inv_ref[...]
    c = jnp.cos(ang)
    s = jnp.sin(ang)
    half = QK_ROPE // 2
    cos_ref[...] = jnp.where(lane < QK_ROPE, c, 0.0)
    s1_ref[...] = jnp.where(lane < half, -s, 0.0)
    s2_ref[...] = jnp.where((lane >= half) & (lane < QK_ROPE), s, 0.0)
    pos = lax.broadcasted_iota(jnp.int32, (half, ts), 1) + pl.program_id(0) * ts
    ang_t = pos.astype(F32) * jnp.tile(invt_ref[...], (1, ts // V7X_LANES))
    cost_ref[...] = jnp.cos(ang_t)
    sint_ref[...] = jnp.sin(ang_t)


def _rope_tables(seq):
    half = QK_ROPE // 2
    inv = ROPE_THETA ** (-jnp.arange(half, dtype=F32) / half)
    inv_lanes = jnp.concatenate([inv, inv, jnp.zeros((V7X_LANES - QK_ROPE,), F32)])[None, :]
    inv_rows = jnp.broadcast_to(inv[:, None], (half, V7X_LANES))
    ts = min(seq, 1024)
    tab = jax.ShapeDtypeStruct((seq, V7X_LANES), F32)
    tab_t = jax.ShapeDtypeStruct((half, seq), F32)
    spec = pl.BlockSpec((ts, V7X_LANES), lambda i: (i, 0))
    spec_t = pl.BlockSpec((half, ts), lambda i: (0, i))
    outs = pl.pallas_call(
        _rope_table_kernel,
        grid=(seq // ts,),
        in_specs=[pl.BlockSpec((1, V7X_LANES), lambda i: (0, 0)), pl.BlockSpec((half, V7X_LANES), lambda i: (0, 0))],
        out_specs=[spec, spec, spec, spec_t, spec_t],
        out_shape=[tab, tab, tab, tab_t, tab_t],
        name="rope_tables",
        compiler_params=_params(("arbitrary",)),
    )(inv_lanes, inv_rows)
    return outs[:3], outs[3:]


def _a_in_kernel(x_ref, ln_ref, w_ref, lnq_ref, lnkv_ref, cq_ref, ckv_ref, sg_ref, kpe_ref):
    g0 = Q_LORA + KV_LORA
    for c in range(x_ref.shape[0] // A_IN_ROW_CHUNK):
        rows = slice(c * A_IN_ROW_CHUNK, (c + 1) * A_IN_ROW_CHUNK)
        x = x_ref[rows, :]
        xn = ((x * _inv_rms(x, D_MODEL)) * ln_ref[...]).astype(BF16)
        u = jnp.dot(xn, w_ref[...], preferred_element_type=F32)
        cq = u[:, :Q_LORA]
        cq_ref[:, rows] = ((cq * _inv_rms(cq, Q_LORA)) * lnq_ref[...]).T.astype(BF16)
        ckv = u[:, Q_LORA:g0]
        ckv_ref[rows, :] = ((ckv * _inv_rms(ckv, KV_LORA)) * lnkv_ref[...]).astype(BF16)
        gate = u[:, g0:g0 + MLA_WIDTH]
        sg_ref[rows, :] = (gate * jax.nn.sigmoid(gate)).astype(BF16)
        kpe_ref[rows, :] = u[:, g0 + MLA_WIDTH:]


def _a_in(x2, ln, w_pad, ln_q, ln_kv, tm):
    m = x2.shape[0]
    row = lambda n: pl.BlockSpec((tm, n), lambda i: (i, 0))
    full = lambda a: pl.BlockSpec(a.shape, lambda i: (0, 0))
    return pl.pallas_call(
        _a_in_kernel,
        grid=(m // tm,),
        in_specs=[row(D_MODEL), full(ln),
                  pl.BlockSpec(w_pad.shape, lambda i: (0, 0), pipeline_mode=pl.Buffered(1)),
                  full(ln_q), full(ln_kv)],
        out_specs=[pl.BlockSpec((Q_LORA, tm), lambda i: (0, i)), row(KV_LORA), row(MLA_WIDTH), row(V7X_LANES)],
        out_shape=[jax.ShapeDtypeStruct((Q_LORA, m), BF16), jax.ShapeDtypeStruct((m, KV_LORA), BF16),
                   jax.ShapeDtypeStruct((m, MLA_WIDTH), BF16), jax.ShapeDtypeStruct((m, V7X_LANES), F32)],
        name="a_in_proj",
        compiler_params=_params(("arbitrary",)),
    )(x2, ln, w_pad, ln_q, ln_kv)


def _q_up_kernel(cqt_ref, wt_ref, g_ref, cos_ref, sin_ref, q_ref, *, scale):
    yt = jnp.dot(wt_ref[...], cqt_ref[...], preferred_element_type=F32)
    tm = yt.shape[1]
    g = jnp.tile(g_ref[...], (1, tm // V7X_LANES))
    cos_t, sin_t = cos_ref[...], sin_ref[...]
    half = QK_ROPE // 2
    zeros = jnp.zeros((QK_PAD - QK_HEAD, tm), BF16)
    for h in range(MLA_HEADS):
        blk = yt[h * QK_HEAD:(h + 1) * QK_HEAD, :]
        r = lax.rsqrt(jnp.sum(blk * blk, axis=0, keepdims=True) * (1.0 / QK_HEAD) + EPS)
        z = (blk * r) * g
        x1, x2 = z[QK_NOPE:QK_NOPE + half, :], z[QK_NOPE + half:, :]
        q_ref[h, :QK_NOPE, :] = (z[:QK_NOPE, :] * scale).astype(BF16)
        q_ref[h, QK_NOPE:QK_NOPE + half, :] = ((x1 * cos_t - x2 * sin_t) * scale).astype(BF16)
        q_ref[h, QK_NOPE + half:QK_HEAD, :] = ((x2 * cos_t + x1 * sin_t) * scale).astype(BF16)
        q_ref[h, QK_HEAD:, :] = zeros


def _kv_up_kernel(ckv_ref, kpe_ref, w_ref, g_ref, cos_ref, s1_ref, s2_ref, k_ref, vt_ref):
    y = jnp.dot(ckv_ref[...], w_ref[...], preferred_element_type=F32)
    g_n = g_ref[:, :QK_NOPE]
    g_r = g_ref[:, QK_NOPE:]
    kpe = kpe_ref[...]
    ss_pe = jnp.sum(kpe * kpe, axis=-1, keepdims=True)
    cos_t, s1_t, s2_t = cos_ref[...], s1_ref[...], s2_ref[...]
    hw = QK_NOPE + V_HEAD
    pad_row = lax.broadcasted_iota(jnp.int32, (VT_ROWS - V_HEAD, y.shape[0]), 0)
    ones_rows = jnp.where(pad_row == 0, 1.0, 0.0).astype(BF16)
    kr = _rope(kpe * g_r, cos_t, s1_t, s2_t)
    for h in range(MLA_HEADS):
        kn = y[:, h * hw:h * hw + QK_NOPE]
        r = lax.rsqrt((jnp.sum(kn * kn, axis=-1, keepdims=True) + ss_pe) * (1.0 / QK_HEAD) + EPS)
        k_ref[h, :, :QK_NOPE] = ((kn * r) * g_n).astype(BF16)
        k_ref[h, :, QK_NOPE:] = (kr * r).astype(BF16)
        vt_ref[h, :V_HEAD, :] = y[:, h * hw + QK_NOPE:(h + 1) * hw].T.astype(BF16)
        vt_ref[h, V_HEAD:, :] = ones_rows


def _seq_spec(tm, n):
    return pl.BlockSpec((None, tm, n), lambda b, i: (b, i, 0))


def _tab_spec(tm):
    return pl.BlockSpec((tm, V7X_LANES), lambda b, i: (i, 0))


def _head_spec(tm, n):
    return pl.BlockSpec((None, MLA_HEADS, tm, n), lambda b, i: (b, 0, i, 0))


def _q_up(cqt, wt, g_rows, tabs_t, batch, seq, tm):
    full = lambda a: pl.BlockSpec(a.shape, lambda bb, i: (0, 0))
    scale = QK_HEAD ** -0.5 * LOG2E
    nblk = seq // tm
    tab = pl.BlockSpec((QK_ROPE // 2, tm), lambda bb, i: (0, i))
    return pl.pallas_call(
        functools.partial(_q_up_kernel, scale=scale),
        grid=(batch, nblk),
        in_specs=[pl.BlockSpec((Q_LORA, tm), lambda bb, i: (0, bb * nblk + i)), full(wt), full(g_rows), tab, tab],
        out_specs=pl.BlockSpec((None, MLA_HEADS, QK_PAD, tm), lambda bb, i: (bb, 0, 0, i)),
        out_shape=jax.ShapeDtypeStruct((batch, MLA_HEADS, QK_PAD, seq), BF16),
        name="a_q_up",
        compiler_params=_params(("arbitrary", "arbitrary")),
    )(cqt, wt, g_rows, *tabs_t)


def _kv_up(ckv3, kpe3, w, g_pad, tabs, tm):
    b, s, _ = ckv3.shape
    full = lambda a: pl.BlockSpec(a.shape, lambda bb, i: (0, 0))
    return pl.pallas_call(
        _kv_up_kernel,
        grid=(b, s // tm),
        in_specs=[_seq_spec(tm, KV_LORA), _seq_spec(tm, V7X_LANES), full(w), full(g_pad),
                  _tab_spec(tm), _tab_spec(tm), _tab_spec(tm)],
        out_specs=[_head_spec(tm, QK_PAD),
                   pl.BlockSpec((None, MLA_HEADS, None, VT_ROWS, tm), lambda bb, i: (bb, 0, i, 0, 0))],
        out_shape=[jax.ShapeDtypeStruct((b, MLA_HEADS, s, QK_PAD), BF16),
                   jax.ShapeDtypeStruct((b, MLA_HEADS, s // tm, VT_ROWS, tm), BF16)],
        name="a_kv_up",
        compiler_params=_params(("arbitrary", "arbitrary")),
    )(ckv3, kpe3, w, g_pad, *tabs)


def _mla_attn_kernel(q_ref, k_ref, vt_ref, o_ref, sig_sc, viol_sc, acc_sc, *, tq, tk):
    qi = pl.program_id(2)
    qt = q_ref[...]
    n_chunks = 2 * qi + 2

    def qk(kc):
        k = k_ref[pl.ds(pl.multiple_of(kc * tk, tk), tk), :]
        return jnp.dot(k, qt, preferred_element_type=F32)

    def causal(s, key_offset):
        key = lax.broadcasted_iota(jnp.int32, (tk, tq), 0) + key_offset
        qry = lax.broadcasted_iota(jnp.int32, (tk, tq), 1)
        return jnp.where(key <= qry, s, NEG_BIG)

    def finish():
        o_ref[...] = (acc_sc[:V_HEAD, :] / acc_sc[V_HEAD:V_HEAD + 1, :]).T.astype(BF16)

    def lagged_chunk(kc, key_offset):
        col0 = 0 if key_offset is None else key_offset
        cs = slice(col0, tq)
        k = k_ref[pl.ds(pl.multiple_of(kc * tk, tk), tk), :]
        s = jnp.dot(k, qt[:, cs], preferred_element_type=F32)
        if key_offset is not None:
            key = lax.broadcasted_iota(jnp.int32, s.shape, 0)
            qry = lax.broadcasted_iota(jnp.int32, s.shape, 1)
            s = jnp.where(key <= qry, s, NEG_BIG)
        shift = sig_sc[:, cs]
        mx = jnp.max(s, axis=0, keepdims=True)
        p = jnp.exp2(s - shift).astype(BF16)
        new_shift = jnp.maximum(shift, mx)
        pv = jnp.dot(vt_ref[kc], p, preferred_element_type=F32)
        acc_sc[:, cs] = (acc_sc[:, cs] + pv) * jnp.exp2(shift - new_shift)
        sig_sc[:, cs] = new_shift
        viol_sc[:, cs] = jnp.maximum(viol_sc[:, cs], mx - shift)

    s0 = jnp.dot(k_ref[:FIRST_SHIFT_KEYS, :], qt, preferred_element_type=F32)
    key0 = lax.broadcasted_iota(jnp.int32, s0.shape, 0) - qi * tq
    s0 = jnp.where(key0 <= lax.broadcasted_iota(jnp.int32, s0.shape, 1), s0, NEG_BIG)
    sig_sc[...] = jnp.max(s0, axis=0, keepdims=True)
    viol_sc[...] = jnp.zeros(viol_sc.shape, F32)
    acc_sc[...] = jnp.zeros(acc_sc.shape, F32)

    def octet(io, carry):
        for u in range(MLA_UNROLL):
            lagged_chunk(MLA_UNROLL * io + u, None)
        return carry

    per_iter = MLA_UNROLL // 2
    n_oct = qi // per_iter
    lax.fori_loop(0, n_oct, octet, 0)
    for rem in range(per_iter):
        @pl.when(qi % per_iter == rem)
        def _(rem=rem):
            for u in range(2 * rem):
                lagged_chunk(MLA_UNROLL * n_oct + u, None)
            lagged_chunk(2 * qi, 0)
            lagged_chunk(2 * qi + 1, tk)
    finish()

    @pl.when(jnp.max(viol_sc[...]) > LAG_LIMIT)
    def _():
        sig_sc[...] = jnp.full(sig_sc.shape, NEG_BIG, F32)
        acc_sc[...] = jnp.zeros(acc_sc.shape, F32)

        def exact_chunk(kc, carry):
            s = causal(qk(kc), kc * tk - qi * tq)
            m_old = sig_sc[...]
            m_new = jnp.maximum(m_old, jnp.max(s, axis=0, keepdims=True))
            p = jnp.exp2(s - m_new).astype(BF16)
            acc_sc[...] = jnp.exp2(m_old - m_new) * acc_sc[...] + jnp.dot(vt_ref[kc], p, preferred_element_type=F32)
            sig_sc[...] = m_new
            return carry

        lax.fori_loop(0, n_chunks, exact_chunk, 0)
        finish()


def _mla_attention(qt, k, vt, tk):
    b, h, s, _ = k.shape
    tq = 2 * tk
    assert vt.shape[-1] == tk and s % tq == 0
    return pl.pallas_call(
        functools.partial(_mla_attn_kernel, tq=tq, tk=tk),
        grid=(b, h, s // tq),
        in_specs=[pl.BlockSpec((None, None, QK_PAD, tq), lambda bb, hh, i: (bb, hh, 0, i)),
                  pl.BlockSpec((None, None, s, QK_PAD), lambda bb, hh, i: (bb, hh, 0, 0)),
                  pl.BlockSpec((None, None, s // tk, VT_ROWS, tk), lambda bb, hh, i: (bb, hh, 0, 0, 0))],
        out_specs=pl.BlockSpec((None, tq, V_HEAD), lambda bb, hh, i: (bb, i, hh)),
        out_shape=jax.ShapeDtypeStruct((b, s, MLA_WIDTH), BF16),
        scratch_shapes=[pltpu.VMEM((1, tq), F32), pltpu.VMEM((1, tq), F32), pltpu.VMEM((VT_ROWS, tq), F32)],
        name="a_mla_attention",
        compiler_params=_params(("arbitrary", "arbitrary", "arbitrary")),
    )(qt, k, vt)


def _a_out_kernel(o_ref, sg_ref, x_ref, w_ref, lnkv_ref, lnb_ref, h_ref, xkv_ref, xb_ref):
    tm = o_ref.shape[0]
    for c in range(tm // OUT_ROW_CHUNK):
        rows = slice(c * OUT_ROW_CHUNK, (c + 1) * OUT_ROW_CHUNK)
        a = (o_ref[rows, :] * sg_ref[rows, :]).astype(BF16)
        h = x_ref[rows, :] + jnp.dot(a, w_ref[...], preferred_element_type=F32)
        h_ref[rows, :] = h
        hn = h * _inv_rms(h, D_MODEL)
        xkv_ref[rows, :] = (hn * lnkv_ref[...]).astype(BF16)
        xb_ref[rows, :] = (hn * lnb_ref[...]).astype(BF16)


def _a_out(o2, sg, x2, w, ln_kv, ln_b, tm):
    m = x2.shape[0]
    row = pl.BlockSpec((tm, D_MODEL), lambda i: (i, 0))
    full = lambda a: pl.BlockSpec(a.shape, lambda i: (0, 0))
    return pl.pallas_call(
        _a_out_kernel,
        grid=(m // tm,),
        in_specs=[row, row, row, pl.BlockSpec(w.shape, lambda i: (0, 0), pipeline_mode=pl.Buffered(1)),
                  full(ln_kv), full(ln_b)],
        out_specs=[row, row, row],
        out_shape=[jax.ShapeDtypeStruct((m, D_MODEL), F32), jax.ShapeDtypeStruct((m, D_MODEL), BF16),
                   jax.ShapeDtypeStruct((m, D_MODEL), BF16)],
        name="a_out_proj",
        compiler_params=_params(("arbitrary",)),
    )(o2, sg, x2, w, ln_kv, ln_b)


def _branch_proj_kernel(x_ref, w_ref, g_ref, o_ref, y_sc, y2_sc, *, mode, dilation, scale):
    tm, tn = o_ref.shape
    rows = PROJ_ROW_CHUNK // dilation
    for c in range(tm // PROJ_ROW_CHUNK):
        y = jnp.dot(x_ref[c * PROJ_ROW_CHUNK:(c + 1) * PROJ_ROW_CHUNK, :], w_ref[...], preferred_element_type=F32)
        for hc in range(tn // DIL_HEAD_DIM):
            sl = slice(hc * DIL_HEAD_DIM, (hc + 1) * DIL_HEAD_DIM)
            yh = y[:, sl]
            if mode == "norm":
                yh = (yh * _inv_rms(yh, DIL_HEAD_DIM)) * g_ref[:, sl]
                if scale != 1.0:
                    yh = yh * scale
            elif mode == "silu":
                yh = yh * jax.nn.sigmoid(yh)
            if dilation == 1:
                o_ref[c * PROJ_ROW_CHUNK:(c + 1) * PROJ_ROW_CHUNK, sl] = yh.astype(BF16)
            else:
                base = c * PROJ_ROW_CHUNK
                y_sc[hc, base:base + PROJ_ROW_CHUNK, :] = yh
                if dilation == STRIDE_STEP * STRIDE_STEP:
                    quarter = PROJ_ROW_CHUNK // STRIDE_STEP
                    for a in range(STRIDE_STEP):
                        y2_sc[hc, base + a * quarter:base + (a + 1) * quarter, :] = (
                            y_sc[hc, pl.ds(base + a, quarter, stride=STRIDE_STEP), :])
                    for r in range(dilation):
                        a, b = r % STRIDE_STEP, r // STRIDE_STEP
                        dst = r * (tm // dilation) + c * rows
                        o_ref[dst:dst + rows, sl] = (
                            y2_sc[hc, pl.ds(base + a * quarter + b, rows, stride=STRIDE_STEP), :].astype(BF16))
                else:
                    for r in range(dilation):
                        dst = r * (tm // dilation) + c * rows
                        o_ref[dst:dst + rows, sl] = y_sc[hc, pl.ds(base + r, rows, stride=dilation), :].astype(BF16)


def _branch_proj(xn, w, g_cols, col0, mode, dilation, scale, name):
    m, kdim = xn.shape
    tm, tn = DIL_TILE, PROJ_COL_TILE
    blk0 = col0 // tn
    return pl.pallas_call(
        functools.partial(_branch_proj_kernel, mode=mode, dilation=dilation, scale=scale),
        grid=(m // tm, DIL_WIDTH // tn),
        in_specs=[pl.BlockSpec((tm, kdim), lambda i, j: (i, 0)),
                  pl.BlockSpec((kdim, tn), lambda i, j: (0, blk0 + j)),
                  pl.BlockSpec((1, tn), lambda i, j: (0, blk0 + j))],
        out_specs=pl.BlockSpec((tm, tn), lambda i, j: (i, j)),
        out_shape=jax.ShapeDtypeStruct((m, DIL_WIDTH), BF16),
        scratch_shapes=[pltpu.VMEM((tn // DIL_HEAD_DIM, tm, DIL_HEAD_DIM), F32)] * 2,
        name=name,
        compiler_params=_params(("arbitrary", "arbitrary")),
    )(xn, w, g_cols)


def _dilated_attn_kernel(*refs):
    qs, kps, kcs, vps, vcs = (refs[i * N_GROUPS:(i + 1) * N_GROUPS] for i in range(5))
    sg_ref, out_ref, o_sc, lse_sc = refs[5 * N_GROUPS:]
    t = pl.program_id(1)
    c = pl.program_id(2)
    row = lax.broadcasted_iota(jnp.int32, (BAND_BLK, 2 * BAND_BLK), 0)
    col = lax.broadcasted_iota(jnp.int32, (BAND_BLK, 2 * BAND_BLK), 1)
    rel = row + BAND_BLK - col
    band = (rel >= 0) & (rel <= LOOKBACK)
    first_band = band & (col >= jnp.where(t > 0, 0, BAND_BLK))
    relf = rel.astype(F32)
    assert DIL_GROUPS[0][1] == 1
    for g in (1, 2, 0):
        dil = DIL_GROUPS[g][1]
        nblk = DIL_TILE // dil // BAND_BLK
        q_ref, kp_ref, kc_ref, vp_ref, vc_ref = qs[g], kps[g], kcs[g], vps[g], vcs[g]
        for hh in range(DIL_HEAD_CHUNK):
            hs = slice(hh * DIL_HEAD_DIM, (hh + 1) * DIL_HEAD_DIM)
            head = (c * DIL_HEAD_CHUNK + hh + 1).astype(F32)
            slope = jnp.exp2(jnp.full((1, 2 * BAND_BLK), -ALIBI_MAX / DIL_HEADS, F32) * head)
            bias = (slope * (-float(dil) * LOG2E)) * relf
            bias_band = jnp.where(band, bias, NEG_BIG)
            bias_first = jnp.where(first_band, bias, NEG_BIG)
            for r in range(dil):
                for a in range(nblk):
                    q = q_ref[r, a, :, hs]
                    k_lo = kp_ref[r, :, hs] if a == 0 else kc_ref[r, a - 1, :, hs]
                    v_lo = vp_ref[r, :, hs] if a == 0 else vc_ref[r, a - 1, :, hs]
                    k = jnp.concatenate([k_lo, kc_ref[r, a, :, hs]], axis=0)
                    v = jnp.concatenate([v_lo, vc_ref[r, a, :, hs]], axis=0)
                    s = lax.dot_general(q, k, (((1,), (1,)), ((), ())), preferred_element_type=F32)
                    s = s + (bias_first if a == 0 else bias_band)
                    m = jnp.max(s, axis=-1, keepdims=True)
                    p = jnp.exp2(s - m)
                    den = jnp.sum(p, axis=-1, keepdims=True)
                    o = jnp.dot(p.astype(BF16), v, preferred_element_type=F32) / den
                    lse = jnp.broadcast_to(m + jnp.log2(den), (BAND_BLK, V7X_LANES))
                    if dil > 1:
                        tok = pl.ds(a * BAND_BLK * dil + r, BAND_BLK, stride=dil)
                        o_sc[g - 1, hh, tok, :] = o
                        lse_sc[g - 1, hh, tok, :] = lse
                    else:
                        tok = slice(a * BAND_BLK, (a + 1) * BAND_BLK)
                        l1, l2 = lse_sc[0, hh, tok, :], lse_sc[1, hh, tok, :]
                        mx = jnp.maximum(jnp.maximum(lse, l1), l2)
                        e0, e1, e2 = jnp.exp2(lse - mx), jnp.exp2(l1 - mx), jnp.exp2(l2 - mx)
                        merged = (e0 * o + e1 * o_sc[0, hh, tok, :] + e2 * o_sc[1, hh, tok, :]) / (e0 + e1 + e2)
                        out_ref[tok, hs] = (merged * sg_ref[tok, hs].astype(F32)).astype(BF16)


def _dilated_attention(qs, ks, vs, gate, batch, seq):
    nt = seq // DIL_TILE
    ins, specs = [], []

    def add(arrs, prev):
        for arr, (_, dil) in zip(arrs, DIL_GROUPS):
            nblk = DIL_TILE // dil // BAND_BLK
            ins.append(arr.reshape(batch, nt, dil, nblk, BAND_BLK, DIL_WIDTH))
            if prev:
                specs.append(pl.BlockSpec((None, None, dil, None, BAND_BLK, DIL_COLS),
                                          lambda b, t, c, nblk=nblk: (b, jnp.maximum(t - 1, 0), 0, nblk - 1, 0, c)))
            else:
                specs.append(pl.BlockSpec((None, None, dil, nblk, BAND_BLK, DIL_COLS),
                                          lambda b, t, c: (b, t, 0, 0, 0, c)))

    add(qs, False)
    add(ks, True)
    add(ks, False)
    add(vs, True)
    add(vs, False)
    ins.append(gate.reshape(batch, nt, DIL_TILE, DIL_WIDTH))
    specs.append(pl.BlockSpec((None, None, DIL_TILE, DIL_COLS), lambda b, t, c: (b, t, 0, c)))
    out = pl.pallas_call(
        _dilated_attn_kernel,
        grid=(batch, nt, DIL_HEADS // DIL_HEAD_CHUNK),
        in_specs=specs,
        out_specs=pl.BlockSpec((None, None, DIL_TILE, DIL_COLS), lambda b, t, c: (b, t, 0, c)),
        out_shape=jax.ShapeDtypeStruct((batch, nt, DIL_TILE, DIL_WIDTH), BF16),
        scratch_shapes=[pltpu.VMEM((N_GROUPS - 1, DIL_HEAD_CHUNK, DIL_TILE, DIL_HEAD_DIM), F32),
                        pltpu.VMEM((N_GROUPS - 1, DIL_HEAD_CHUNK, DIL_TILE, V7X_LANES), F32)],
        name="b_dilated_attention",
        compiler_params=_params(("arbitrary", "arbitrary", "arbitrary")),
    )(*ins)
    return out.reshape(batch * seq, DIL_WIDTH)


def _b_out_kernel(a_ref, h_ref, w_ref, out_ref):
    for c in range(out_ref.shape[0] // OUT_ROW_CHUNK):
        rows = slice(c * OUT_ROW_CHUNK, (c + 1) * OUT_ROW_CHUNK)
        out_ref[rows, :] = h_ref[rows, :] + jnp.dot(a_ref[rows, :], w_ref[...], preferred_element_type=F32)


def _b_out(a, h1, w, tm):
    m = h1.shape[0]
    row = pl.BlockSpec((tm, D_MODEL), lambda i: (i, 0))
    return pl.pallas_call(
        _b_out_kernel,
        grid=(m // tm,),
        in_specs=[row, row, pl.BlockSpec(w.shape, lambda i: (0, 0), pipeline_mode=pl.Buffered(1))],
        out_specs=row,
        out_shape=jax.ShapeDtypeStruct((m, D_MODEL), F32),
        name="b_out_proj",
        compiler_params=_params(("arbitrary",)),
    )(a, h1, w)


def _tile(n, want):
    t = min(n, want)
    assert n % t == 0, (n, t)
    return t


def kernel(x, a_ln, a_w_in, a_ln_q, a_w_q_up, a_ln_kv, a_w_kv_up, a_q_norm, a_k_norm, a_w_o,
           kv_ln, kv_w, kv_k_norm, b_ln, b_w_in, b_q_norm, b_w_o):
    batch, seq, d = x.shape
    assert d == D_MODEL and a_ln.shape[0] == 1 and b_ln.shape[0] == 1
    assert seq % DIL_TILE == 0
    m = batch * seq
    x2 = x.reshape(m, d)

    lat = Q_LORA + KV_LORA
    w_in = a_w_in[0]
    w_in_pad = jnp.concatenate(
        [w_in[:, :lat], w_in[:, lat + QK_ROPE:], w_in[:, lat:lat + QK_ROPE],
         jnp.zeros((d, V7X_LANES - QK_ROPE), F32)], axis=1).astype(BF16)
    assert w_in_pad.shape == (d, A_IN_PAD)
    w_q_t = a_w_q_up[0].T.astype(BF16)
    w_kv = a_w_kv_up[0].astype(BF16)
    gq_rows = jnp.broadcast_to(a_q_norm[0][:, None], (QK_HEAD, V7X_LANES))
    gk_pad = jnp.pad(a_k_norm[0], (0, QK_PAD - QK_HEAD))[None, :]
    w_o_a = a_w_o[0].astype(BF16)
    w_kvs = kv_w.astype(BF16)
    w_b_in = b_w_in[0].astype(BF16)
    w_o_b = b_w_o[0].astype(BF16)
    k_cols = N_GROUPS * DIL_WIDTH
    gk_cols = jnp.concatenate([jnp.tile(kv_k_norm, (1, DIL_HEADS)).reshape(1, k_cols),
                               jnp.ones((1, k_cols), F32)], axis=1)
    gq_cols = jnp.concatenate([jnp.tile(b_q_norm[0], (1, DIL_HEADS)).reshape(1, k_cols),
                               jnp.ones((1, DIL_WIDTH), F32)], axis=1)

    tabs, tabs_t = _rope_tables(seq)
    cqt, ckv, sg_a, kpe = _a_in(x2, a_ln, w_in_pad, a_ln_q, a_ln_kv, _tile(m, A_IN_ROW_TILE))
    blk = _tile(seq, MLA_KEY_CHUNK)
    q = _q_up(cqt, w_q_t, gq_rows, tabs_t, batch, seq, blk)
    k, vt = _kv_up(ckv.reshape(batch, seq, KV_LORA), kpe.reshape(batch, seq, V7X_LANES), w_kv, gk_pad, tabs, blk)
    o_a = _mla_attention(q, k, vt, blk)
    h1, xn_kv, xn_b = _a_out(o_a.reshape(m, MLA_WIDTH), sg_a, x2, w_o_a, kv_ln[None, :], b_ln,
                             _tile(m, 2 * OUT_ROW_CHUNK))

    q_scale = DIL_HEAD_DIM ** -0.5 * LOG2E
    qs, ks, vs = [], [], []
    for g, (_, dil) in enumerate(DIL_GROUPS):
        ks.append(_branch_proj(xn_kv, w_kvs, gk_cols, g * DIL_WIDTH, "norm", dil, 1.0, f"b_k_proj_d{dil}"))
        vs.append(_branch_proj(xn_kv, w_kvs, gk_cols, k_cols + g * DIL_WIDTH, "copy", dil, 1.0, f"b_v_proj_d{dil}"))
        qs.append(_branch_proj(xn_b, w_b_in, gq_cols, g * DIL_WIDTH, "norm", dil, q_scale, f"b_q_proj_d{dil}"))
    sg_b = _branch_proj(xn_b, w_b_in, gq_cols, k_cols, "silu", 1, 1.0, "b_gate_proj")
    a_b = _dilated_attention(qs, ks, vs, sg_b, batch, seq)
    out = _b_out(a_b, h1, w_o_b, _tile(m, 2 * OUT_ROW_CHUNK))
    return out.reshape(batch, seq, d)
```

```python
import functools
import math

import jax
import jax.numpy as jnp
from jax import lax
from jax.experimental import pallas as pl
from jax.experimental.pallas import tpu as pltpu

F32 = jnp.float32
BF16 = jnp.bfloat16

V7X_LANES = 128
V7X_VMEM_BYTES = 64 * 1024 * 1024
VMEM_LIMIT_BYTES = V7X_VMEM_BYTES * 7 // 8

D_MODEL = 2048
MLA_HEADS = 16
Q_LORA = 512
KV_LORA = 512
QK_NOPE = 128
QK_ROPE = 64
QK_HEAD = QK_NOPE + QK_ROPE
V_HEAD = 128
MLA_WIDTH = MLA_HEADS * V_HEAD
ROPE_THETA = 10000.0
DIL_GROUPS = ((128, 1), (512, 4), (2048, 16))
N_GROUPS = 3
DIL_HEADS = 16
DIL_HEAD_DIM = 128
DIL_WIDTH = DIL_HEADS * DIL_HEAD_DIM
LOOKBACK = 128
BAND_BLK = 128
ALIBI_MAX = 8.0
EPS = 1e-6
NEG_BIG = -1e30
LOG2E = math.log2(math.e)

QK_PAD = 2 * V7X_LANES
A_IN_PAD = Q_LORA + KV_LORA + MLA_WIDTH + V7X_LANES

DIL_TILE = BAND_BLK * DIL_GROUPS[-1][1]
DIL_HEAD_CHUNK = 2
DIL_COLS = DIL_HEAD_CHUNK * DIL_HEAD_DIM
V7X_BF16_SUBLANES = 16
A_IN_ROW_TILE = 512
A_IN_ROW_CHUNK = 256
OUT_ROW_CHUNK = 256
MLA_KEY_CHUNK = 512
MLA_UNROLL = 8
VT_ROWS = V_HEAD + V7X_BF16_SUBLANES
FIRST_SHIFT_KEYS = V7X_BF16_SUBLANES
LAG_LIMIT = 64.0
STRIDE_STEP = 4
PROJ_ROW_CHUNK = 256
PROJ_COL_TILE = 1024


def _params(semantics):
    return pltpu.CompilerParams(dimension_semantics=semantics, vmem_limit_bytes=VMEM_LIMIT_BYTES)


def _inv_rms(x, n):
    return lax.rsqrt(jnp.sum(x * x, axis=-1, keepdims=True) * (1.0 / n) + EPS)


def _rope(rv, cos_t, s1_t, s2_t):
    return rv * cos_t + pltpu.roll(rv, 96, 1) * s1_t + pltpu.roll(rv, 32, 1) * s2_t


def _rope_table_kernel(inv_ref, invt_ref, cos_ref, s1_ref, s2_ref, cost_ref, sint_ref):
    ts = cos_ref.shape[0]
    row = lax.broadcasted_iota(jnp.int32, (ts, V7X_LANES), 0) + pl.program_id(0) * ts
    lane = lax.broadcasted_iota(jnp.int32, (ts, V7X_LANES), 1)
    ang = row.astype(F32) * inv_ref[...]
    c = jnp.cos(ang)
    s = jnp.sin(ang)
    half = QK_ROPE // 2
    cos_ref[...] = jnp.where(lane < QK_ROPE, c, 0.0)
    s1_ref[...] = jnp.where(lane < half, -s, 0.0)
    s2_ref[...] = jnp.where((lane >= half) & (lane < QK_ROPE), s, 0.0)
    pos = lax.broadcasted_iota(jnp.int32, (half, ts), 1) + pl.program_id(0) * ts
    ang_t = pos.astype(F32) * jnp.tile(invt_ref[...], (1, ts // V7X_LANES))
    cost_ref[...] = jnp.cos(ang_t)
    sint_ref[...] = jnp.sin(ang_t)


def _rope_tables(seq):
    half = QK_ROPE // 2
    inv = ROPE_THETA ** (-jnp.arange(half, dtype=F32) / half)
    inv_lanes = jnp.concatenate([inv, inv, jnp.zeros((V7X_LANES - QK_ROPE,), F32)])[None, :]
    inv_rows = jnp.broadcast_to(inv[:, None], (half, V7X_LANES))
    ts = min(seq, 1024)
    tab = jax.ShapeDtypeStruct((seq, V7X_LANES), F32)
    tab_t = jax.ShapeDtypeStruct((half, seq), F32)
    spec = pl.BlockSpec((ts, V7X_LANES), lambda i: (i, 0))
    spec_t = pl.BlockSpec((half, ts), lambda i: (0, i))
    outs = pl.pallas_call(
        _rope_table_kernel,
        grid=(seq // ts,),
        in_specs=[pl.BlockSpec((1, V7X_LANES), lambda i: (0, 0)), pl.BlockSpec((half, V7X_LANES), lambda i: (0, 0))],
        out_specs=[spec, spec, spec, spec_t, spec_t],
        out_shape=[tab, tab, tab, tab_t, tab_t],
        name="rope_tables",
        compiler_params=_params(("arbitrary",)),
    )(inv_lanes, inv_rows)
    return outs[:3], outs[3:]


def _a_in_kernel(x_ref, ln_ref, w_ref, lnq_ref, lnkv_ref, cq_ref, ckv_ref, sg_ref, kpe_ref):
    g0 = Q_LORA + KV_LORA
    for c in range(x_ref.shape[0] // A_IN_ROW_CHUNK):
        rows = slice(c * A_IN_ROW_CHUNK, (c + 1) * A_IN_ROW_CHUNK)
        x = x_ref[rows, :]
        xn = ((x * _inv_rms(x, D_MODEL)) * ln_ref[...]).astype(BF16)
        u = jnp.dot(xn, w_ref[...], preferred_element_type=F32)
        cq = u[:, :Q_LORA]
        cq_ref[:, rows] = ((cq * _inv_rms(cq, Q_LORA)) * lnq_ref[...]).T.astype(BF16)
        ckv = u[:, Q_LORA:g0]
        ckv_ref[rows, :] = ((ckv * _inv_rms(ckv, KV_LORA)) * lnkv_ref[...]).astype(BF16)
        gate = u[:, g0:g0 + MLA_WIDTH]
        sg_ref[rows, :] = (gate * jax.nn.sigmoid(gate)).astype(BF16)
        kpe_ref[rows, :] = u[:, g0 + MLA_WIDTH:]


def _a_in(x2, ln, w_pad, ln_q, ln_kv, tm):
    m = x2.shape[0]
    row = lambda n: pl.BlockSpec((tm, n), lambda i: (i, 0))
    full = lambda a: pl.BlockSpec(a.shape, lambda i: (0, 0))
    return pl.pallas_call(
        _a_in_kernel,
        grid=(m // tm,),
        in_specs=[row(D_MODEL), full(ln),
                  pl.BlockSpec(w_pad.shape, lambda i: (0, 0), pipeline_mode=pl.Buffered(1)),
                  full(ln_q), full(ln_kv)],
        out_specs=[pl.BlockSpec((Q_LORA, tm), lambda i: (0, i)), row(KV_LORA), row(MLA_WIDTH), row(V7X_LANES)],
        out_shape=[jax.ShapeDtypeStruct((Q_LORA, m), BF16), jax.ShapeDtypeStruct((m, KV_LORA), BF16),
                   jax.ShapeDtypeStruct((m, MLA_WIDTH), BF16), jax.ShapeDtypeStruct((m, V7X_LANES), F32)],
        name="a_in_proj",
        compiler_params=_params(("arbitrary",)),
    )(x2, ln, w_pad, ln_q, ln_kv)


def _q_up_kernel(cqt_ref, wt_ref, g_ref, cos_ref, sin_ref, q_ref, *, scale):
    yt = jnp.dot(wt_ref[...], cqt_ref[...], preferred_element_type=F32)
    tm = yt.shape[1]
    g = jnp.tile(g_ref[...], (1, tm // V7X_LANES))
    cos_t, sin_t = cos_ref[...], sin_ref[...]
    half = QK_ROPE // 2
    zeros = jnp.zeros((QK_PAD - QK_HEAD, tm), BF16)
    for h in range(MLA_HEADS):
        blk = yt[h * QK_HEAD:(h + 1) * QK_HEAD, :]
        r = lax.rsqrt(jnp.sum(blk * blk, axis=0, keepdims=True) * (1.0 / QK_HEAD) + EPS)
        z = (blk * r) * g
        x1, x2 = z[QK_NOPE:QK_NOPE + half, :], z[QK_NOPE + half:, :]
        q_ref[h, :QK_NOPE, :] = (z[:QK_NOPE, :] * scale).astype(BF16)
        q_ref[h, QK_NOPE:QK_NOPE + half, :] = ((x1 * cos_t - x2 * sin_t) * scale).astype(BF16)
        q_ref[h, QK_NOPE + half:QK_HEAD, :] = ((x2 * cos_t + x1 * sin_t) * scale).astype(BF16)
        q_ref[h, QK_HEAD:, :] = zeros


def _kv_up_kernel(ckv_ref, kpe_ref, w_ref, g_ref, cos_ref, s1_ref, s2_ref, k_ref, vt_ref):
    y = jnp.dot(ckv_ref[...], w_ref[...], preferred_element_type=F32)
    g_n = g_ref[:, :QK_NOPE]
    g_r = g_ref[:, QK_NOPE:]
    kpe = kpe_ref[...]
    ss_pe = jnp.sum(kpe * kpe, axis=-1, keepdims=True)
    cos_t, s1_t, s2_t = cos_ref[...], s1_ref[...], s2_ref[...]
    hw = QK_NOPE + V_HEAD
    pad_row = lax.broadcasted_iota(jnp.int32, (VT_ROWS - V_HEAD, y.shape[0]), 0)
    ones_rows = jnp.where(pad_row == 0, 1.0, 0.0).astype(BF16)
    kr = _rope(kpe * g_r, cos_t, s1_t, s2_t)
    for h in range(MLA_HEADS):
        kn = y[:, h * hw:h * hw + QK_NOPE]
        r = lax.rsqrt((jnp.sum(kn * kn, axis=-1, keepdims=True) + ss_pe) * (1.0 / QK_HEAD) + EPS)
        k_ref[h, :, :QK_NOPE] = ((kn * r) * g_n).astype(BF16)
        k_ref[h, :, QK_NOPE:] = (kr * r).astype(BF16)
        vt_ref[h, :V_HEAD, :] = y[:, h * hw + QK_NOPE:(h + 1) * hw].T.astype(BF16)
        vt_ref[h, V_HEAD:, :] = ones_rows


def _seq_spec(tm, n):
    return pl.BlockSpec((None, tm, n), lambda b, i: (b, i, 0))


def _tab_spec(tm):
    return pl.BlockSpec((tm, V7X_LANES), lambda b, i: (i, 0))


def _head_spec(tm, n):
    return pl.BlockSpec((None, MLA_HEADS, tm, n), lambda b, i: (b, 0, i, 0))


def _q_up(cqt, wt, g_rows, tabs_t, batch, seq, tm):
    full = lambda a: pl.BlockSpec(a.shape, lambda bb, i: (0, 0))
    scale = QK_HEAD ** -0.5 * LOG2E
    nblk = seq // tm
    tab = pl.BlockSpec((QK_ROPE // 2, tm), lambda bb, i: (0, i))
    return pl.pallas_call(
        functools.partial(_q_up_kernel, scale=scale),
        grid=(batch, nblk),
        in_specs=[pl.BlockSpec((Q_LORA, tm), lambda bb, i: (0, bb * nblk + i)), full(wt), full(g_rows), tab, tab],
        out_specs=pl.BlockSpec((None, MLA_HEADS, QK_PAD, tm), lambda bb, i: (bb, 0, 0, i)),
        out_shape=jax.ShapeDtypeStruct((batch, MLA_HEADS, QK_PAD, seq), BF16),
        name="a_q_up",
        compiler_params=_params(("arbitrary", "arbitrary")),
    )(cqt, wt, g_rows, *tabs_t)


def _kv_up(ckv3, kpe3, w, g_pad, tabs, tm):
    b, s, _ = ckv3.shape
    full = lambda a: pl.BlockSpec(a.shape, lambda bb, i: (0, 0))
    return pl.pallas_call(
        _kv_up_kernel,
        grid=(b, s // tm),
        in_specs=[_seq_spec(tm, KV_LORA), _seq_spec(tm, V7X_LANES), full(w), full(g_pad),
                  _tab_spec(tm), _tab_spec(tm), _tab_spec(tm)],
        out_specs=[_head_spec(tm, QK_PAD),
                   pl.BlockSpec((None, MLA_HEADS, None, VT_ROWS, tm), lambda bb, i: (bb, 0, i, 0, 0))],
        out_shape=[jax.ShapeDtypeStruct((b, MLA_HEADS, s, QK_PAD), BF16),
                   jax.ShapeDtypeStruct((b, MLA_HEADS, s // tm, VT_ROWS, tm), BF16)],
        name="a_kv_up",
        compiler_params=_params(("arbitrary", "arbitrary")),
    )(ckv3, kpe3, w, g_pad, *tabs)


def _mla_attn_kernel(q_ref, k_ref, vt_ref, o_ref, sig_sc, viol_sc, acc_sc, *, tq, tk):
    qi = pl.program_id(2)
    qt = q_ref[...]
    n_chunks = 2 * qi + 2

    def qk(kc):
        k = k_ref[pl.ds(pl.multiple_of(kc * tk, tk), tk), :]
        return jnp.dot(k, qt, preferred_element_type=F32)

    def causal(s, key_offset):
        key = lax.broadcasted_iota(jnp.int32, (tk, tq), 0) + key_offset
        qry = lax.broadcasted_iota(jnp.int32, (tk, tq), 1)
        return jnp.where(key <= qry, s, NEG_BIG)

    def finish():
        o_ref[...] = (acc_sc[:V_HEAD, :] / acc_sc[V_HEAD:V_HEAD + 1, :]).T.astype(BF16)

    def lagged_chunk(kc, key_offset):
        col0 = 0 if key_offset is None else key_offset
        cs = slice(col0, tq)
        k = k_ref[pl.ds(pl.multiple_of(kc * tk, tk), tk), :]
        s = jnp.dot(k, qt[:, cs], preferred_element_type=F32)
        if key_offset is not None:
            key = lax.broadcasted_iota(jnp.int32, s.shape, 0)
            qry = lax.broadcasted_iota(jnp.int32, s.shape, 1)
            s = jnp.where(key <= qry, s, NEG_BIG)
        shift = sig_sc[:, cs]
        mx = jnp.max(s, axis=0, keepdims=True)
        p = jnp.exp2(s - shift).astype(BF16)
        new_shift = jnp.maximum(shift, mx)
        pv = jnp.dot(vt_ref[kc], p, preferred_element_type=F32)
        acc_sc[:, cs] = (acc_sc[:, cs] + pv) * jnp.exp2(shift - new_shift)
        sig_sc[:, cs] = new_shift
        viol_sc[:, cs] = jnp.maximum(viol_sc[:, cs], mx - shift)

    s0 = jnp.dot(k_ref[:FIRST_SHIFT_KEYS, :], qt, preferred_element_type=F32)
    key0 = lax.broadcasted_iota(jnp.int32, s0.shape, 0) - qi * tq
    s0 = jnp.where(key0 <= lax.broadcasted_iota(jnp.int32, s0.shape, 1), s0, NEG_BIG)
    sig_sc[...] = jnp.max(s0, axis=0, keepdims=True)
    viol_sc[...] = jnp.zeros(viol_sc.shape, F32)
    acc_sc[...] = jnp.zeros(acc_sc.shape, F32)

    def octet(io, carry):
        for u in range(MLA_UNROLL):
            lagged_chunk(MLA_UNROLL * io + u, None)
        return carry

    per_iter = MLA_UNROLL // 2
    n_oct = qi // per_iter
    lax.fori_loop(0, n_oct, octet, 0)
    for rem in range(per_iter):
        @pl.when(qi % per_iter == rem)
        def _(rem=rem):
            for u in range(2 * rem):
                lagged_chunk(MLA_UNROLL * n_oct + u, None)
            lagged_chunk(2 * qi, 0)
            lagged_chunk(2 * qi + 1, tk)
    finish()

    @pl.when(jnp.max(viol_sc[...]) > LAG_LIMIT)
    def _():
        sig_sc[...] = jnp.full(sig_sc.shape, NEG_BIG, F32)
        acc_sc[...] = jnp.zeros(acc_sc.shape, F32)

        def exact_chunk(kc, carry):
            s = causal(qk(kc), kc * tk - qi * tq)
            m_old = sig_sc[...]
            m_new = jnp.maximum(m_old, jnp.max(s, axis=0, keepdims=True))
            p = jnp.exp2(s - m_new).astype(BF16)
            acc_sc[...] = jnp.exp2(m_old - m_new) * acc_sc[...] + jnp.dot(vt_ref[kc], p, preferred_element_type=F32)
            sig_sc[...] = m_new
            return carry

        lax.fori_loop(0, n_chunks, exact_chunk, 0)
        finish()


def _mla_attention(qt, k, vt, tk):
    b, h, s, _ = k.shape
    tq = 2 * tk
    assert vt.shape[-1] == tk and s % tq == 0
    return pl.pallas_call(
        functools.partial(_mla_attn_kernel, tq=tq, tk=tk),
        grid=(b, h, s // tq),
        in_specs=[pl.BlockSpec((None, None, QK_PAD, tq), lambda bb, hh, i: (bb, hh, 0, i)),
                  pl.BlockSpec((None, None, s, QK_PAD), lambda bb, hh, i: (bb, hh, 0, 0)),
                  pl.BlockSpec((None, None, s // tk, VT_ROWS, tk), lambda bb, hh, i: (bb, hh, 0, 0, 0))],
        out_specs=pl.BlockSpec((None, tq, V_HEAD), lambda bb, hh, i: (bb, i, hh)),
        out_shape=jax.ShapeDtypeStruct((b, s, MLA_WIDTH), BF16),
        scratch_shapes=[pltpu.VMEM((1, tq), F32), pltpu.VMEM((1, tq), F32), pltpu.VMEM((VT_ROWS, tq), F32)],
        name="a_mla_attention",
        compiler_params=_params(("arbitrary", "arbitrary", "arbitrary")),
    )(qt, k, vt)


def _a_out_kernel(o_ref, sg_ref, x_ref, w_ref, lnkv_ref, lnb_ref, h_ref, xkv_ref, xb_ref):
    tm = o_ref.shape[0]
    for c in range(tm // OUT_ROW_CHUNK):
        rows = slice(c * OUT_ROW_CHUNK, (c + 1) * OUT_ROW_CHUNK)
        a = (o_ref[rows, :] * sg_ref[rows, :]).astype(BF16)
        h = x_ref[rows, :] + jnp.dot(a, w_ref[...], preferred_element_type=F32)
        h_ref[rows, :] = h
        hn = h * _inv_rms(h, D_MODEL)
        xkv_ref[rows, :] = (hn * lnkv_ref[...]).astype(BF16)
        xb_ref[rows, :] = (hn * lnb_ref[...]).astype(BF16)


def _a_out(o2, sg, x2, w, ln_kv, ln_b, tm):
    m = x2.shape[0]
    row = pl.BlockSpec((tm, D_MODEL), lambda i: (i, 0))
    full = lambda a: pl.BlockSpec(a.shape, lambda i: (0, 0))
    return pl.pallas_call(
        _a_out_kernel,
        grid=(m // tm,),
        in_specs=[row, row, row, pl.BlockSpec(w.shape, lambda i: (0, 0), pipeline_mode=pl.Buffered(1)),
                  full(ln_kv), full(ln_b)],
        out_specs=[row, row, row],
        out_shape=[jax.ShapeDtypeStruct((m, D_MODEL), F32), jax.ShapeDtypeStruct((m, D_MODEL), BF16),
                   jax.ShapeDtypeStruct((m, D_MODEL), BF16)],
        name="a_out_proj",
        compiler_params=_params(("arbitrary",)),
    )(o2, sg, x2, w, ln_kv, ln_b)


def _branch_proj_kernel(x_ref, w_ref, g_ref, o_ref, y_sc, y2_sc, *, mode, dilation, scale):
    tm, tn = o_ref.shape
    rows = PROJ_ROW_CHUNK // dilation
    for c in range(tm // PROJ_ROW_CHUNK):
        y = jnp.dot(x_ref[c * PROJ_ROW_CHUNK:(c + 1) * PROJ_ROW_CHUNK, :], w_ref[...], preferred_element_type=F32)
        for hc in range(tn // DIL_HEAD_DIM):
            sl = slice(hc * DIL_HEAD_DIM, (hc + 1) * DIL_HEAD_DIM)
            yh = y[:, sl]
            if mode == "norm":
                yh = (yh * _inv_rms(yh, DIL_HEAD_DIM)) * g_ref[:, sl]
                if scale != 1.0:
                    yh = yh * scale
            elif mode == "silu":
                yh = yh * jax.nn.sigmoid(yh)
            if dilation == 1:
                o_ref[c * PROJ_ROW_CHUNK:(c + 1) * PROJ_ROW_CHUNK, sl] = yh.astype(BF16)
            else:
                base = c * PROJ_ROW_CHUNK
                y_sc[hc, base:base + PROJ_ROW_CHUNK, :] = yh
                if dilation == STRIDE_STEP * STRIDE_STEP:
                    quarter = PROJ_ROW_CHUNK // STRIDE_STEP
                    for a in range(STRIDE_STEP):
                        y2_sc[hc, base + a * quarter:base + (a + 1) * quarter, :] = (
                            y_sc[hc, pl.ds(base + a, quarter, stride=STRIDE_STEP), :])
                    for r in range(dilation):
                        a, b = r % STRIDE_STEP, r // STRIDE_STEP
                        dst = r * (tm // dilation) + c * rows
                        o_ref[dst:dst + rows, sl] = (
                            y2_sc[hc, pl.ds(base + a * quarter + b, rows, stride=STRIDE_STEP), :].astype(BF16))
                else:
                    for r in range(dilation):
                        dst = r * (tm // dilation) + c * rows
                        o_ref[dst:dst + rows, sl] = y_sc[hc, pl.ds(base + r, rows, stride=dilation), :].astype(BF16)


def _branch_proj(xn, w, g_cols, col0, mode, dilation, scale, name):
    m, kdim = xn.shape
    tm, tn = DIL_TILE, PROJ_COL_TILE
    blk0 = col0 // tn
    return pl.pallas_call(
        functools.partial(_branch_proj_kernel, mode=mode, dilation=dilation, scale=scale),
        grid=(m // tm, DIL_WIDTH // tn),
        in_specs=[pl.BlockSpec((tm, kdim), lambda i, j: (i, 0)),
                  pl.BlockSpec((kdim, tn), lambda i, j: (0, blk0 + j)),
                  pl.BlockSpec((1, tn), lambda i, j: (0, blk0 + j))],
        out_specs=pl.BlockSpec((tm, tn), lambda i, j: (i, j)),
        out_shape=jax.ShapeDtypeStruct((m, DIL_WIDTH), BF16),
        scratch_shapes=[pltpu.VMEM((tn // DIL_HEAD_DIM, tm, DIL_HEAD_DIM), F32)] * 2,
        name=name,
        compiler_params=_params(("arbitrary", "arbitrary")),
    )(xn, w, g_cols)


def _dilated_attn_kernel(*refs):
    qs, kps, kcs, vps, vcs = (refs[i * N_GROUPS:(i + 1) * N_GROUPS] for i in range(5))
    sg_ref, out_ref, o_sc, lse_sc = refs[5 * N_GROUPS:]
    t = pl.program_id(1)
    c = pl.program_id(2)
    row = lax.broadcasted_iota(jnp.int32, (BAND_BLK, 2 * BAND_BLK), 0)
    col = lax.broadcasted_iota(jnp.int32, (BAND_BLK, 2 * BAND_BLK), 1)
    rel = row + BAND_BLK - col
    band = (rel >= 0) & (rel <= LOOKBACK)
    first_band = band & (col >= jnp.where(t > 0, 0, BAND_BLK))
    relf = rel.astype(F32)
    assert DIL_GROUPS[0][1] == 1
    for g in (1, 2, 0):
        dil = DIL_GROUPS[g][1]
        nblk = DIL_TILE // dil // BAND_BLK
        q_ref, kp_ref, kc_ref, vp_ref, vc_ref = qs[g], kps[g], kcs[g], vps[g], vcs[g]
        for hh in range(DIL_HEAD_CHUNK):
            hs = slice(hh * DIL_HEAD_DIM, (hh + 1) * DIL_HEAD_DIM)
            head = (c * DIL_HEAD_CHUNK + hh + 1).astype(F32)
            slope = jnp.exp2(jnp.full((1, 2 * BAND_BLK), -ALIBI_MAX / DIL_HEADS, F32) * head)
            bias = (slope * (-float(dil) * LOG2E)) * relf
            bias_band = jnp.where(band, bias, NEG_BIG)
            bias_first = jnp.where(first_band, bias, NEG_BIG)
            for r in range(dil):
                for a in range(nblk):
                    q = q_ref[r, a, :, hs]
                    k_lo = kp_ref[r, :, hs] if a == 0 else kc_ref[r, a - 1, :, hs]
                    v_lo = vp_ref[r, :, hs] if a == 0 else vc_ref[r, a - 1, :, hs]
                    k = jnp.concatenate([k_lo, kc_ref[r, a, :, hs]], axis=0)
                    v = jnp.concatenate([v_lo, vc_ref[r, a, :, hs]], axis=0)
                    s = lax.dot_general(q, k, (((1,), (1,)), ((), ())), preferred_element_type=F32)
                    s = s + (bias_first if a == 0 else bias_band)
                    m = jnp.max(s, axis=-1, keepdims=True)
                    p = jnp.exp2(s - m)
                    den = jnp.sum(p, axis=-1, keepdims=True)
                    o = jnp.dot(p.astype(BF16), v, preferred_element_type=F32) / den
                    lse = jnp.broadcast_to(m + jnp.log2(den), (BAND_BLK, V7X_LANES))
                    if dil > 1:
                        tok = pl.ds(a * BAND_BLK * dil + r, BAND_BLK, stride=dil)
                        o_sc[g - 1, hh, tok, :] = o
                        lse_sc[g - 1, hh, tok, :] = lse
                    else:
                        tok = slice(a * BAND_BLK, (a + 1) * BAND_BLK)
                        l1, l2 = lse_sc[0, hh, tok, :], lse_sc[1, hh, tok, :]
                        mx = jnp.maximum(jnp.maximum(lse, l1), l2)
                        e0, e1, e2 = jnp.exp2(lse - mx), jnp.exp2(l1 - mx), jnp.exp2(l2 - mx)
                        merged = (e0 * o + e1 * o_sc[0, hh, tok, :] + e2 * o_sc[1, hh, tok, :]) / (e0 + e1 + e2)
                        out_ref[tok, hs] = (merged * sg_ref[tok, hs].astype(F32)).astype(BF16)


def _dilated_attention(qs, ks, vs, gate, batch, seq):
    nt = seq // DIL_TILE
    ins, specs = [], []

    def add(arrs, prev):
        for arr, (_, dil) in zip(arrs, DIL_GROUPS):
            nblk = DIL_TILE // dil // BAND_BLK
            ins.append(arr.reshape(batch, nt, dil, nblk, BAND_BLK, DIL_WIDTH))
            if prev:
                specs.append(pl.BlockSpec((None, None, dil, None, BAND_BLK, DIL_COLS),
                                          lambda b, t, c, nblk=nblk: (b, jnp.maximum(t - 1, 0), 0, nblk - 1, 0, c)))
            else:
                specs.append(pl.BlockSpec((None, None, dil, nblk, BAND_BLK, DIL_COLS),
                                          lambda b, t, c: (b, t, 0, 0, 0, c)))

    add(qs, False)
    add(ks, True)
    add(ks, False)
    add(vs, True)
    add(vs, False)
    ins.append(gate.reshape(batch, nt, DIL_TILE, DIL_WIDTH))
    specs.append(pl.BlockSpec((None, None, DIL_TILE, DIL_COLS), lambda b, t, c: (b, t, 0, c)))
    out = pl.pallas_call(
        _dilated_attn_kernel,
        grid=(batch, nt, DIL_HEADS // DIL_HEAD_CHUNK),
        in_specs=specs,
        out_specs=pl.BlockSpec((None, None, DIL_TILE, DIL_COLS), lambda b, t, c: (b, t, 0, c)),
        out_shape=jax.ShapeDtypeStruct((batch, nt, DIL_TILE, DIL_WIDTH), BF16),
        scratch_shapes=[pltpu.VMEM((N_GROUPS - 1, DIL_HEAD_CHUNK, DIL_TILE, DIL_HEAD_DIM), F32),
                        pltpu.VMEM((N_GROUPS - 1, DIL_HEAD_CHUNK, DIL_TILE, V7X_LANES), F32)],
        name="b_dilated_attention",
        compiler_params=_params(("arbitrary", "arbitrary", "arbitrary")),
    )(*ins)
    return out.reshape(batch * seq, DIL_WIDTH)


def _b_out_kernel(a_ref, h_ref, w_ref, out_ref):
    for c in range(out_ref.shape[0] // OUT_ROW_CHUNK):
        rows = slice(c * OUT_ROW_CHUNK, (c + 1) * OUT_ROW_CHUNK)
        out_ref[rows, :] = h_ref[rows, :] + jnp.dot(a_ref[rows, :], w_ref[...], preferred_element_type=F32)


def _b_out(a, h1, w, tm):
    m = h1.shape[0]
    row = pl.BlockSpec((tm, D_MODEL), lambda i: (i, 0))
    return pl.pallas_call(
        _b_out_kernel,
        grid=(m // tm,),
        in_specs=[row, row, pl.BlockSpec(w.shape, lambda i: (0, 0), pipeline_mode=pl.Buffered(1))],
        out_specs=row,
        out_shape=jax.ShapeDtypeStruct((m, D_MODEL), F32),
        name="b_out_proj",
        compiler_params=_params(("arbitrary",)),
    )(a, h1, w)


def _tile(n, want):
    t = min(n, want)
    assert n % t == 0, (n, t)
    return t


def kernel(x, a_ln, a_w_in, a_ln_q, a_w_q_up, a_ln_kv, a_w_kv_up, a_q_norm, a_k_norm, a_w_o,
           kv_ln, kv_w, kv_k_norm, b_ln, b_w_in, b_q_norm, b_w_o):
    batch, seq, d = x.shape
    assert d == D_MODEL and a_ln.shape[0] == 1 and b_ln.shape[0] == 1
    assert seq % DIL_TILE == 0
    m = batch * seq
    x2 = x.reshape(m, d)

    lat = Q_LORA + KV_LORA
    w_in = a_w_in[0]
    w_in_pad = jnp.concatenate(
        [w_in[:, :lat], w_in[:, lat + QK_ROPE:], w_in[:, lat:lat + QK_ROPE],
         jnp.zeros((d, V7X_LANES - QK_ROPE), F32)], axis=1).astype(BF16)
    assert w_in_pad.shape == (d, A_IN_PAD)
    w_q_t = a_w_q_up[0].T.astype(BF16)
    w_kv = a_w_kv_up[0].astype(BF16)
    gq_rows = jnp.broadcast_to(a_q_norm[0][:, None], (QK_HEAD, V7X_LANES))
    gk_pad = jnp.pad(a_k_norm[0], (0, QK_PAD - QK_HEAD))[None, :]
    w_o_a = a_w_o[0].astype(BF16)
    w_kvs = kv_w.astype(BF16)
    w_b_in = b_w_in[0].astype(BF16)
    w_o_b = b_w_o[0].astype(BF16)
    k_cols = N_GROUPS * DIL_WIDTH
    gk_cols = jnp.concatenate([jnp.tile(kv_k_norm, (1, DIL_HEADS)).reshape(1, k_cols),
                               jnp.ones((1, k_cols), F32)], axis=1)
    gq_cols = jnp.concatenate([jnp.tile(b_q_norm[0], (1, DIL_HEADS)).reshape(1, k_cols),
                               jnp.ones((1, DIL_WIDTH), F32)], axis=1)

    tabs, tabs_t = _rope_tables(seq)
    cqt, ckv, sg_a, kpe = _a_in(x2, a_ln, w_in_pad, a_ln_q, a_ln_kv, _tile(m, A_IN_ROW_TILE))
    blk = _tile(seq, MLA_KEY_CHUNK)
    q = _q_up(cqt, w_q_t, gq_rows, tabs_t, batch, seq, blk)
    k, vt = _kv_up(ckv.reshape(batch, seq, KV_LORA), kpe.reshape(batch, seq, V7X_LANES), w_kv, gk_pad, tabs, blk)
    o_a = _mla_attention(q, k, vt, blk)
    h1, xn_kv, xn_b = _a_out(o_a.reshape(m, MLA_WIDTH), sg_a, x2, w_o_a, kv_ln[None, :], b_ln,
                             _tile(m, 2 * OUT_ROW_CHUNK))

    q_scale = DIL_HEAD_DIM ** -0.5 * LOG2E
    qs, ks, vs = [], [], []
    for g, (_, dil) in enumerate(DIL_GROUPS):
        ks.append(_branch_proj(xn_kv, w_kvs, gk_cols, g * DIL_WIDTH, "norm", dil, 1.0, f"b_k_proj_d{dil}"))
        vs.append(_branch_proj(xn_kv, w_kvs, gk_cols, k_cols + g * DIL_WIDTH, "copy", dil, 1.0, f"b_v_proj_d{dil}"))
        qs.append(_branch_proj(xn_b, w_b_in, gq_cols, g * DIL_WIDTH, "norm", dil, q_scale, f"b_q_proj_d{dil}"))
    sg_b = _branch_proj(xn_b, w_b_in, gq_cols, k_cols, "silu", 1, 1.0, "b_gate_proj")
    a_b = _dilated_attention(qs, ks, vs, sg_b, batch, seq)
    out = _b_out(a_b, h1, w_o_b, _tile(m, 2 * OUT_ROW_CHUNK))
    return out.reshape(batch, seq, d)
```

```python
import functools
import math

import jax
import jax.numpy as jnp
from jax import lax
from jax.experimental import pallas as pl
from jax.experimental.pallas import tpu as pltpu

F32 = jnp.float32
BF16 = jnp.bfloat16

V7X_LANES = 128
V7X_VMEM_BYTES = 64 * 1024 * 1024
VMEM_LIMIT_BYTES = V7X_VMEM_BYTES * 7 // 8

D_MODEL = 2048
MLA_HEADS = 16
Q_LORA = 512
KV_LORA = 512
QK_NOPE = 128
QK_ROPE = 64
QK_HEAD = QK_NOPE + QK_ROPE
V_HEAD = 128
MLA_WIDTH = MLA_HEADS * V_HEAD
ROPE_THETA = 10000.0
DIL_GROUPS = ((128, 1), (512, 4), (2048, 16))
N_GROUPS = 3
DIL_HEADS = 16
DIL_HEAD_DIM = 128
DIL_WIDTH = DIL_HEADS * DIL_HEAD_DIM
LOOKBACK = 128
BAND_BLK = 128
ALIBI_MAX = 8.0
EPS = 1e-6
NEG_BIG = -1e30
LOG2E = math.log2(math.e)

QK_PAD = 2 * V7X_LANES
A_IN_PAD = Q_LORA + KV_LORA + MLA_WIDTH + V7X_LANES

DIL_TILE = BAND_BLK * DIL_GROUPS[-1][1]
DIL_HEAD_CHUNK = 2
DIL_COLS = DIL_HEAD_CHUNK * DIL_HEAD_DIM
V7X_BF16_SUBLANES = 16
A_IN_ROW_TILE = 512
A_IN_ROW_CHUNK = 256
UP_HEAD_GROUP = 4
OUT_ROW_CHUNK = 256
MLA_KEY_CHUNK = 512
MLA_UNROLL = 8
VT_ROWS = V_HEAD + V7X_BF16_SUBLANES
FIRST_SHIFT_KEYS = V7X_BF16_SUBLANES
LAG_LIMIT = 64.0
STRIDE_STEP = 4
PROJ_ROW_CHUNK = 256
PROJ_COL_TILE = 1024


def _params(semantics):
    return pltpu.CompilerParams(dimension_semantics=semantics, vmem_limit_bytes=VMEM_LIMIT_BYTES)


def _inv_rms(x, n):
    return lax.rsqrt(jnp.sum(x * x, axis=-1, keepdims=True) * (1.0 / n) + EPS)


def _rope(rv, cos_t, s1_t, s2_t):
    return rv * cos_t + pltpu.roll(rv, 96, 1) * s1_t + pltpu.roll(rv, 32, 1) * s2_t


def _rope_table_kernel(inv_ref, invt_ref, cos_ref, s1_ref, s2_ref, cost_ref, sint_ref):
    ts = cos_ref.shape[0]
    row = lax.broadcasted_iota(jnp.int32, (ts, V7X_LANES), 0) + pl.program_id(0) * ts
    lane = lax.broadcasted_iota(jnp.int32, (ts, V7X_LANES), 1)
    ang = row.astype(F32) * inv_ref[...]
    c = jnp.cos(ang)
    s = jnp.sin(ang)
    half = QK_ROPE // 2
    cos_ref[...] = jnp.where(lane < QK_ROPE, c, 0.0)
    s1_ref[...] = jnp.where(lane < half, -s, 0.0)
    s2_ref[...] = jnp.where((lane >= half) & (lane < QK_ROPE), s, 0.0)
    pos = lax.broadcasted_iota(jnp.int32, (half, ts), 1) + pl.program_id(0) * ts
    ang_t = pos.astype(F32) * jnp.tile(invt_ref[...], (1, ts // V7X_LANES))
    cost_ref[...] = jnp.cos(ang_t)
    sint_ref[...] = jnp.sin(ang_t)


def _rope_tables(seq):
    half = QK_ROPE // 2
    inv = ROPE_THETA ** (-jnp.arange(half, dtype=F32) / half)
    inv_lanes = jnp.concatenate([inv, inv, jnp.zeros((V7X_LANES - QK_ROPE,), F32)])[None, :]
    inv_rows = jnp.broadcast_to(inv[:, None], (half, V7X_LANES))
    ts = min(seq, 1024)
    tab = jax.ShapeDtypeStruct((seq, V7X_LANES), F32)
    tab_t = jax.ShapeDtypeStruct((half, seq), F32)
    spec = pl.BlockSpec((ts, V7X_LANES), lambda i: (i, 0))
    spec_t = pl.BlockSpec((half, ts), lambda i: (0, i))
    outs = pl.pallas_call(
        _rope_table_kernel,
        grid=(seq // ts,),
        in_specs=[pl.BlockSpec((1, V7X_LANES), lambda i: (0, 0)), pl.BlockSpec((half, V7X_LANES), lambda i: (0, 0))],
        out_specs=[spec, spec, spec, spec_t, spec_t],
        out_shape=[tab, tab, tab, tab_t, tab_t],
        name="rope_tables",
        compiler_params=_params(("arbitrary",)),
    )(inv_lanes, inv_rows)
    return outs[:3], outs[3:]


def _a_in_kernel(x_ref, ln_ref, w_ref, lnq_ref, lnkv_ref, cq_ref, ckv_ref, sg_ref, kpe_ref):
    g0 = Q_LORA + KV_LORA
    for c in range(x_ref.shape[0] // A_IN_ROW_CHUNK):
        rows = slice(c * A_IN_ROW_CHUNK, (c + 1) * A_IN_ROW_CHUNK)
        x = x_ref[rows, :]
        xn = ((x * _inv_rms(x, D_MODEL)) * ln_ref[...]).astype(BF16)
        u = jnp.dot(xn, w_ref[...], preferred_element_type=F32)
        cq = u[:, :Q_LORA]
        cq_ref[:, rows] = ((cq * _inv_rms(cq, Q_LORA)) * lnq_ref[...]).T.astype(BF16)
        ckv = u[:, Q_LORA:g0]
        ckv_ref[rows, :] = ((ckv * _inv_rms(ckv, KV_LORA)) * lnkv_ref[...]).astype(BF16)
        gate = u[:, g0:g0 + MLA_WIDTH]
        sg_ref[rows, :] = (gate * jax.nn.sigmoid(gate)).astype(BF16)
        kpe_ref[rows, :] = u[:, g0 + MLA_WIDTH:]


def _a_in(x2, ln, w_pad, ln_q, ln_kv, tm):
    m = x2.shape[0]
    row = lambda n: pl.BlockSpec((tm, n), lambda i: (i, 0))
    full = lambda a: pl.BlockSpec(a.shape, lambda i: (0, 0))
    return pl.pallas_call(
        _a_in_kernel,
        grid=(m // tm,),
        in_specs=[row(D_MODEL), full(ln),
                  pl.BlockSpec(w_pad.shape, lambda i: (0, 0), pipeline_mode=pl.Buffered(1)),
                  full(ln_q), full(ln_kv)],
        out_specs=[pl.BlockSpec((Q_LORA, tm), lambda i: (0, i)), row(KV_LORA), row(MLA_WIDTH), row(V7X_LANES)],
        out_shape=[jax.ShapeDtypeStruct((Q_LORA, m), BF16), jax.ShapeDtypeStruct((m, KV_LORA), BF16),
                   jax.ShapeDtypeStruct((m, MLA_WIDTH), BF16), jax.ShapeDtypeStruct((m, V7X_LANES), F32)],
        name="a_in_proj",
        compiler_params=_params(("arbitrary",)),
    )(x2, ln, w_pad, ln_q, ln_kv)


def _q_up_kernel(cqt_ref, wt_ref, g_ref, cos_ref, sin_ref, q_ref, *, scale):
    tm = cqt_ref.shape[1]
    g = jnp.tile(g_ref[...], (1, tm // V7X_LANES))
    cos_t, sin_t = cos_ref[...], sin_ref[...]
    half = QK_ROPE // 2
    zeros = jnp.zeros((QK_PAD - QK_HEAD, tm), BF16)
    for h in range(MLA_HEADS):
        if h % UP_HEAD_GROUP == 0:
            w_rows = slice(h * QK_HEAD, (h + UP_HEAD_GROUP) * QK_HEAD)
            yt = jnp.dot(wt_ref[w_rows, :], cqt_ref[...], preferred_element_type=F32)
        hl = h % UP_HEAD_GROUP
        blk = yt[hl * QK_HEAD:(hl + 1) * QK_HEAD, :]
        r = lax.rsqrt(jnp.sum(blk * blk, axis=0, keepdims=True) * (1.0 / QK_HEAD) + EPS)
        z = (blk * r) * g
        x1, x2 = z[QK_NOPE:QK_NOPE + half, :], z[QK_NOPE + half:, :]
        q_ref[h, :QK_NOPE, :] = (z[:QK_NOPE, :] * scale).astype(BF16)
        q_ref[h, QK_NOPE:QK_NOPE + half, :] = ((x1 * cos_t - x2 * sin_t) * scale).astype(BF16)
        q_ref[h, QK_NOPE + half:QK_HEAD, :] = ((x2 * cos_t + x1 * sin_t) * scale).astype(BF16)
        q_ref[h, QK_HEAD:, :] = zeros


def _kv_up_kernel(ckv_ref, kpe_ref, w_ref, g_ref, cos_ref, s1_ref, s2_ref, k_ref, vt_ref):
    g_n = g_ref[:, :QK_NOPE]
    g_r = g_ref[:, QK_NOPE:]
    kpe = kpe_ref[...]
    ss_pe = jnp.sum(kpe * kpe, axis=-1, keepdims=True)
    cos_t, s1_t, s2_t = cos_ref[...], s1_ref[...], s2_ref[...]
    hw = QK_NOPE + V_HEAD
    pad_row = lax.broadcasted_iota(jnp.int32, (VT_ROWS - V_HEAD, ckv_ref.shape[0]), 0)
    ones_rows = jnp.where(pad_row == 0, 1.0, 0.0).astype(BF16)
    kr = _rope(kpe * g_r, cos_t, s1_t, s2_t)
    for h in range(MLA_HEADS):
        if h % UP_HEAD_GROUP == 0:
            w_cols = slice(h * hw, (h + UP_HEAD_GROUP) * hw)
            y = jnp.dot(ckv_ref[...], w_ref[:, w_cols], preferred_element_type=F32)
        hl = h % UP_HEAD_GROUP
        kn = y[:, hl * hw:hl * hw + QK_NOPE]
        r = lax.rsqrt((jnp.sum(kn * kn, axis=-1, keepdims=True) + ss_pe) * (1.0 / QK_HEAD) + EPS)
        k_ref[h, :, :QK_NOPE] = ((kn * r) * g_n).astype(BF16)
        k_ref[h, :, QK_NOPE:] = (kr * r).astype(BF16)
        vt_ref[h, :V_HEAD, :] = y[:, hl * hw + QK_NOPE:(hl + 1) * hw].T.astype(BF16)
        vt_ref[h, V_HEAD:, :] = ones_rows


def _seq_spec(tm, n):
    return pl.BlockSpec((None, tm, n), lambda b, i: (b, i, 0))


def _tab_spec(tm):
    return pl.BlockSpec((tm, V7X_LANES), lambda b, i: (i, 0))


def _head_spec(tm, n):
    return pl.BlockSpec((None, MLA_HEADS, tm, n), lambda b, i: (b, 0, i, 0))


def _q_up(cqt, wt, g_rows, tabs_t, batch, seq, tm):
    full = lambda a: pl.BlockSpec(a.shape, lambda bb, i: (0, 0))
    scale = QK_HEAD ** -0.5 * LOG2E
    nblk = seq // tm
    tab = pl.BlockSpec((QK_ROPE // 2, tm), lambda bb, i: (0, i))
    return pl.pallas_call(
        functools.partial(_q_up_kernel, scale=scale),
        grid=(batch, nblk),
        in_specs=[pl.BlockSpec((Q_LORA, tm), lambda bb, i: (0, bb * nblk + i)), full(wt), full(g_rows), tab, tab],
        out_specs=pl.BlockSpec((None, MLA_HEADS, QK_PAD, tm), lambda bb, i: (bb, 0, 0, i)),
        out_shape=jax.ShapeDtypeStruct((batch, MLA_HEADS, QK_PAD, seq), BF16),
        name="a_q_up",
        compiler_params=_params(("arbitrary", "arbitrary")),
    )(cqt, wt, g_rows, *tabs_t)


def _kv_up(ckv3, kpe3, w, g_pad, tabs, tm):
    b, s, _ = ckv3.shape
    full = lambda a: pl.BlockSpec(a.shape, lambda bb, i: (0, 0))
    return pl.pallas_call(
        _kv_up_kernel,
        grid=(b, s // tm),
        in_specs=[_seq_spec(tm, KV_LORA), _seq_spec(tm, V7X_LANES), full(w), full(g_pad),
                  _tab_spec(tm), _tab_spec(tm), _tab_spec(tm)],
        out_specs=[_head_spec(tm, QK_PAD),
                   pl.BlockSpec((None, MLA_HEADS, None, VT_ROWS, tm), lambda bb, i: (bb, 0, i, 0, 0))],
        out_shape=[jax.ShapeDtypeStruct((b, MLA_HEADS, s, QK_PAD), BF16),
                   jax.ShapeDtypeStruct((b, MLA_HEADS, s // tm, VT_ROWS, tm), BF16)],
        name="a_kv_up",
        compiler_params=_params(("arbitrary", "arbitrary")),
    )(ckv3, kpe3, w, g_pad, *tabs)


def _mla_attn_kernel(q_ref, k_ref, vt_ref, o_ref, sig_sc, viol_sc, acc_sc, *, tq, tk):
    qi = pl.program_id(2)
    qt = q_ref[...]
    n_chunks = 2 * qi + 2

    def qk(kc):
        k = k_ref[pl.ds(pl.multiple_of(kc * tk, tk), tk), :]
        return jnp.dot(k, qt, preferred_element_type=F32)

    def causal(s, key_offset):
        key = lax.broadcasted_iota(jnp.int32, (tk, tq), 0) + key_offset
        qry = lax.broadcasted_iota(jnp.int32, (tk, tq), 1)
        return jnp.where(key <= qry, s, NEG_BIG)

    def finish():
        o_ref[...] = (acc_sc[:V_HEAD, :] / acc_sc[V_HEAD:V_HEAD + 1, :]).T.astype(BF16)

    def lagged_chunk(kc, key_offset):
        col0 = 0 if key_offset is None else key_offset
        cs = slice(col0, tq)
        k = k_ref[pl.ds(pl.multiple_of(kc * tk, tk), tk), :]
        s = jnp.dot(k, qt[:, cs], preferred_element_type=F32)
        if key_offset is not None:
            key = lax.broadcasted_iota(jnp.int32, s.shape, 0)
            qry = lax.broadcasted_iota(jnp.int32, s.shape, 1)
            s = jnp.where(key <= qry, s, NEG_BIG)
        shift = sig_sc[:, cs]
        mx = jnp.max(s, axis=0, keepdims=True)
        p = jnp.exp2(s - shift).astype(BF16)
        new_shift = jnp.maximum(shift, mx)
        pv = jnp.dot(vt_ref[kc], p, preferred_element_type=F32)
        acc_sc[:, cs] = (acc_sc[:, cs] + pv) * jnp.exp2(shift - new_shift)
        sig_sc[:, cs] = new_shift
        viol_sc[:, cs] = jnp.maximum(viol_sc[:, cs], mx - shift)

    s0 = jnp.dot(k_ref[:FIRST_SHIFT_KEYS, :], qt, preferred_element_type=F32)
    key0 = lax.broadcasted_iota(jnp.int32, s0.shape, 0) - qi * tq
    s0 = jnp.where(key0 <= lax.broadcasted_iota(jnp.int32, s0.shape, 1), s0, NEG_BIG)
    sig_sc[...] = jnp.max(s0, axis=0, keepdims=True)
    viol_sc[...] = jnp.zeros(viol_sc.shape, F32)
    acc_sc[...] = jnp.zeros(acc_sc.shape, F32)

    def octet(io, carry):
        for u in range(MLA_UNROLL):
            lagged_chunk(MLA_UNROLL * io + u, None)
        return carry

    per_iter = MLA_UNROLL // 2
    n_oct = qi // per_iter
    lax.fori_loop(0, n_oct, octet, 0)
    for rem in range(per_iter):
        @pl.when(qi % per_iter == rem)
        def _(rem=rem):
            for u in range(2 * rem):
                lagged_chunk(MLA_UNROLL * n_oct + u, None)
            lagged_chunk(2 * qi, 0)
            lagged_chunk(2 * qi + 1, tk)
    finish()

    @pl.when(jnp.max(viol_sc[...]) > LAG_LIMIT)
    def _():
        sig_sc[...] = jnp.full(sig_sc.shape, NEG_BIG, F32)
        acc_sc[...] = jnp.zeros(acc_sc.shape, F32)

        def exact_chunk(kc, carry):
            s = causal(qk(kc), kc * tk - qi * tq)
            m_old = sig_sc[...]
            m_new = jnp.maximum(m_old, jnp.max(s, axis=0, keepdims=True))
            p = jnp.exp2(s - m_new).astype(BF16)
            acc_sc[...] = jnp.exp2(m_old - m_new) * acc_sc[...] + jnp.dot(vt_ref[kc], p, preferred_element_type=F32)
            sig_sc[...] = m_new
            return carry

        lax.fori_loop(0, n_chunks, exact_chunk, 0)
        finish()


def _mla_attention(qt, k, vt, tk):
    b, h, s, _ = k.shape
    tq = 2 * tk
    assert vt.shape[-1] == tk and s % tq == 0
    return pl.pallas_call(
        functools.partial(_mla_attn_kernel, tq=tq, tk=tk),
        grid=(b, h, s // tq),
        in_specs=[pl.BlockSpec((None, None, QK_PAD, tq), lambda bb, hh, i: (bb, hh, 0, i)),
                  pl.BlockSpec((None, None, s, QK_PAD), lambda bb, hh, i: (bb, hh, 0, 0)),
                  pl.BlockSpec((None, None, s // tk, VT_ROWS, tk), lambda bb, hh, i: (bb, hh, 0, 0, 0))],
        out_specs=pl.BlockSpec((None, tq, V_HEAD), lambda bb, hh, i: (bb, i, hh)),
        out_shape=jax.ShapeDtypeStruct((b, s, MLA_WIDTH), BF16),
        scratch_shapes=[pltpu.VMEM((1, tq), F32), pltpu.VMEM((1, tq), F32), pltpu.VMEM((VT_ROWS, tq), F32)],
        name="a_mla_attention",
        compiler_params=_params(("arbitrary", "arbitrary", "arbitrary")),
    )(qt, k, vt)


def _a_out_kernel(o_ref, sg_ref, x_ref, w_ref, lnkv_ref, lnb_ref, h_ref, xkv_ref, xb_ref):
    tm = o_ref.shape[0]
    for c in range(tm // OUT_ROW_CHUNK):
        rows = slice(c * OUT_ROW_CHUNK, (c + 1) * OUT_ROW_CHUNK)
        a = (o_ref[rows, :] * sg_ref[rows, :]).astype(BF16)
        h = x_ref[rows, :] + jnp.dot(a, w_ref[...], preferred_element_type=F32)
        h_ref[rows, :] = h
        hn = h * _inv_rms(h, D_MODEL)
        xkv_ref[rows, :] = (hn * lnkv_ref[...]).astype(BF16)
        xb_ref[rows, :] = (hn * lnb_ref[...]).astype(BF16)


def _a_out(o2, sg, x2, w, ln_kv, ln_b, tm):
    m = x2.shape[0]
    row = pl.BlockSpec((tm, D_MODEL), lambda i: (i, 0))
    full = lambda a: pl.BlockSpec(a.shape, lambda i: (0, 0))
    return pl.pallas_call(
        _a_out_kernel,
        grid=(m // tm,),
        in_specs=[row, row, row, pl.BlockSpec(w.shape, lambda i: (0, 0), pipeline_mode=pl.Buffered(1)),
                  full(ln_kv), full(ln_b)],
        out_specs=[row, row, row],
        out_shape=[jax.ShapeDtypeStruct((m, D_MODEL), F32), jax.ShapeDtypeStruct((m, D_MODEL), BF16),
                   jax.ShapeDtypeStruct((m, D_MODEL), BF16)],
        name="a_out_proj",
        compiler_params=_params(("arbitrary",)),
    )(o2, sg, x2, w, ln_kv, ln_b)


def _branch_proj_kernel(x_ref, w_ref, g_ref, o_ref, y_sc, y2_sc, *, mode, dilation, scale):
    tm, tn = o_ref.shape
    rows = PROJ_ROW_CHUNK // dilation
    for c in range(tm // PROJ_ROW_CHUNK):
        y = jnp.dot(x_ref[c * PROJ_ROW_CHUNK:(c + 1) * PROJ_ROW_CHUNK, :], w_ref[...], preferred_element_type=F32)
        for hc in range(tn // DIL_HEAD_DIM):
            sl = slice(hc * DIL_HEAD_DIM, (hc + 1) * DIL_HEAD_DIM)
            yh = y[:, sl]
            if mode == "norm":
                yh = (yh * _inv_rms(yh, DIL_HEAD_DIM)) * g_ref[:, sl]
                if scale != 1.0:
                    yh = yh * scale
            elif mode == "silu":
                yh = yh * jax.nn.sigmoid(yh)
            if dilation == 1:
                o_ref[c * PROJ_ROW_CHUNK:(c + 1) * PROJ_ROW_CHUNK, sl] = yh.astype(BF16)
            else:
                base = c * PROJ_ROW_CHUNK
                y_sc[hc, base:base + PROJ_ROW_CHUNK, :] = yh
                if dilation == STRIDE_STEP * STRIDE_STEP:
                    quarter = PROJ_ROW_CHUNK // STRIDE_STEP
                    for a in range(STRIDE_STEP):
                        y2_sc[hc, base + a * quarter:base + (a + 1) * quarter, :] = (
                            y_sc[hc, pl.ds(base + a, quarter, stride=STRIDE_STEP), :])
                    for r in range(dilation):
                        a, b = r % STRIDE_STEP, r // STRIDE_STEP
                        dst = r * (tm // dilation) + c * rows
                        o_ref[dst:dst + rows, sl] = (
                            y2_sc[hc, pl.ds(base + a * quarter + b, rows, stride=STRIDE_STEP), :].astype(BF16))
                else:
                    for r in range(dilation):
                        dst = r * (tm // dilation) + c * rows
                        o_ref[dst:dst + rows, sl] = y_sc[hc, pl.ds(base + r, rows, stride=dilation), :].astype(BF16)


def _branch_proj(xn, w, g_cols, col0, mode, dilation, scale, name):
    m, kdim = xn.shape
    tm, tn = DIL_TILE, PROJ_COL_TILE
    blk0 = col0 // tn
    return pl.pallas_call(
        functools.partial(_branch_proj_kernel, mode=mode, dilation=dilation, scale=scale),
        grid=(m // tm, DIL_WIDTH // tn),
        in_specs=[pl.BlockSpec((tm, kdim), lambda i, j: (i, 0)),
                  pl.BlockSpec((kdim, tn), lambda i, j: (0, blk0 + j)),
                  pl.BlockSpec((1, tn), lambda i, j: (0, blk0 + j))],
        out_specs=pl.BlockSpec((tm, tn), lambda i, j: (i, j)),
        out_shape=jax.ShapeDtypeStruct((m, DIL_WIDTH), BF16),
        scratch_shapes=[pltpu.VMEM((tn // DIL_HEAD_DIM, tm, DIL_HEAD_DIM), F32)] * 2,
        name=name,
        compiler_params=_params(("arbitrary", "arbitrary")),
    )(xn, w, g_cols)


def _dilated_attn_kernel(*refs):
    qs, kps, kcs, vps, vcs = (refs[i * N_GROUPS:(i + 1) * N_GROUPS] for i in range(5))
    sg_ref, out_ref, o_sc, lse_sc = refs[5 * N_GROUPS:]
    t = pl.program_id(1)
    c = pl.program_id(2)
    row = lax.broadcasted_iota(jnp.int32, (BAND_BLK, 2 * BAND_BLK), 0)
    col = lax.broadcasted_iota(jnp.int32, (BAND_BLK, 2 * BAND_BLK), 1)
    rel = row + BAND_BLK - col
    band = (rel >= 0) & (rel <= LOOKBACK)
    first_band = band & (col >= jnp.where(t > 0, 0, BAND_BLK))
    relf = rel.astype(F32)
    assert DIL_GROUPS[0][1] == 1
    for g in (1, 2, 0):
        dil = DIL_GROUPS[g][1]
        nblk = DIL_TILE // dil // BAND_BLK
        q_ref, kp_ref, kc_ref, vp_ref, vc_ref = qs[g], kps[g], kcs[g], vps[g], vcs[g]
        for hh in range(DIL_HEAD_CHUNK):
            hs = slice(hh * DIL_HEAD_DIM, (hh + 1) * DIL_HEAD_DIM)
            head = (c * DIL_HEAD_CHUNK + hh + 1).astype(F32)
            slope = jnp.exp2(jnp.full((1, 2 * BAND_BLK), -ALIBI_MAX / DIL_HEADS, F32) * head)
            bias = (slope * (-float(dil) * LOG2E)) * relf
            bias_band = jnp.where(band, bias, NEG_BIG)
            bias_first = jnp.where(first_band, bias, NEG_BIG)
            for r in range(dil):
                for a in range(nblk):
                    q = q_ref[r, a, :, hs]
                    k_lo = kp_ref[r, :, hs] if a == 0 else kc_ref[r, a - 1, :, hs]
                    v_lo = vp_ref[r, :, hs] if a == 0 else vc_ref[r, a - 1, :, hs]
                    k = jnp.concatenate([k_lo, kc_ref[r, a, :, hs]], axis=0)
                    v = jnp.concatenate([v_lo, vc_ref[r, a, :, hs]], axis=0)
                    s = lax.dot_general(q, k, (((1,), (1,)), ((), ())), preferred_element_type=F32)
                    s = s + (bias_first if a == 0 else bias_band)
                    m = jnp.max(s, axis=-1, keepdims=True)
                    p = jnp.exp2(s - m)
                    den = jnp.sum(p, axis=-1, keepdims=True)
                    o = jnp.dot(p.astype(BF16), v, preferred_element_type=F32) / den
                    lse = jnp.broadcast_to(m + jnp.log2(den), (BAND_BLK, V7X_LANES))
                    if dil > 1:
                        tok = pl.ds(a * BAND_BLK * dil + r, BAND_BLK, stride=dil)
                        o_sc[g - 1, hh, tok, :] = o
                        lse_sc[g - 1, hh, tok, :] = lse
                    else:
                        tok = slice(a * BAND_BLK, (a + 1) * BAND_BLK)
                        l1, l2 = lse_sc[0, hh, tok, :], lse_sc[1, hh, tok, :]
                        mx = jnp.maximum(jnp.maximum(lse, l1), l2)
                        e0, e1, e2 = jnp.exp2(lse - mx), jnp.exp2(l1 - mx), jnp.exp2(l2 - mx)
                        merged = (e0 * o + e1 * o_sc[0, hh, tok, :] + e2 * o_sc[1, hh, tok, :]) / (e0 + e1 + e2)
                        out_ref[tok, hs] = (merged * sg_ref[tok, hs].astype(F32)).astype(BF16)


def _dilated_attention(qs, ks, vs, gate, batch, seq):
    nt = seq // DIL_TILE
    ins, specs = [], []

    def add(arrs, prev):
        for arr, (_, dil) in zip(arrs, DIL_GROUPS):
            nblk = DIL_TILE // dil // BAND_BLK
            ins.append(arr.reshape(batch, nt, dil, nblk, BAND_BLK, DIL_WIDTH))
            if prev:
                specs.append(pl.BlockSpec((None, None, dil, None, BAND_BLK, DIL_COLS),
                                          lambda b, t, c, nblk=nblk: (b, jnp.maximum(t - 1, 0), 0, nblk - 1, 0, c)))
            else:
                specs.append(pl.BlockSpec((None, None, dil, nblk, BAND_BLK, DIL_COLS),
                                          lambda b, t, c: (b, t, 0, 0, 0, c)))

    add(qs, False)
    add(ks, True)
    add(ks, False)
    add(vs, True)
    add(vs, False)
    ins.append(gate.reshape(batch, nt, DIL_TILE, DIL_WIDTH))
    specs.append(pl.BlockSpec((None, None, DIL_TILE, DIL_COLS), lambda b, t, c: (b, t, 0, c)))
    out = pl.pallas_call(
        _dilated_attn_kernel,
        grid=(batch, nt, DIL_HEADS // DIL_HEAD_CHUNK),
        in_specs=specs,
        out_specs=pl.BlockSpec((None, None, DIL_TILE, DIL_COLS), lambda b, t, c: (b, t, 0, c)),
        out_shape=jax.ShapeDtypeStruct((batch, nt, DIL_TILE, DIL_WIDTH), BF16),
        scratch_shapes=[pltpu.VMEM((N_GROUPS - 1, DIL_HEAD_CHUNK, DIL_TILE, DIL_HEAD_DIM), F32),
                        pltpu.VMEM((N_GROUPS - 1, DIL_HEAD_CHUNK, DIL_TILE, V7X_LANES), F32)],
        name="b_dilated_attention",
        compiler_params=_params(("arbitrary", "arbitrary", "arbitrary")),
    )(*ins)
    return out.reshape(batch * seq, DIL_WIDTH)


def _b_out_kernel(a_ref, h_ref, w_ref, out_ref):
    for c in range(out_ref.shape[0] // OUT_ROW_CHUNK):
        rows = slice(c * OUT_ROW_CHUNK, (c + 1) * OUT_ROW_CHUNK)
        out_ref[rows, :] = h_ref[rows, :] + jnp.dot(a_ref[rows, :], w_ref[...], preferred_element_type=F32)


def _b_out(a, h1, w, tm):
    m = h1.shape[0]
    row = pl.BlockSpec((tm, D_MODEL), lambda i: (i, 0))
    return pl.pallas_call(
        _b_out_kernel,
        grid=(m // tm,),
        in_specs=[row, row, pl.BlockSpec(w.shape, lambda i: (0, 0), pipeline_mode=pl.Buffered(1))],
        out_specs=row,
        out_shape=jax.ShapeDtypeStruct((m, D_MODEL), F32),
        name="b_out_proj",
        compiler_params=_params(("arbitrary",)),
    )(a, h1, w)


def _tile(n, want):
    t = min(n, want)
    assert n % t == 0, (n, t)
    return t


def kernel(x, a_ln, a_w_in, a_ln_q, a_w_q_up, a_ln_kv, a_w_kv_up, a_q_norm, a_k_norm, a_w_o,
           kv_ln, kv_w, kv_k_norm, b_ln, b_w_in, b_q_norm, b_w_o):
    batch, seq, d = x.shape
    assert d == D_MODEL and a_ln.shape[0] == 1 and b_ln.shape[0] == 1
    assert seq % DIL_TILE == 0
    m = batch * seq
    x2 = x.reshape(m, d)

    lat = Q_LORA + KV_LORA
    w_in = a_w_in[0]
    w_in_pad = jnp.concatenate(
        [w_in[:, :lat], w_in[:, lat + QK_ROPE:], w_in[:, lat:lat + QK_ROPE],
         jnp.zeros((d, V7X_LANES - QK_ROPE), F32)], axis=1).astype(BF16)
    assert w_in_pad.shape == (d, A_IN_PAD)
    w_q_t = a_w_q_up[0].T.astype(BF16)
    w_kv = a_w_kv_up[0].astype(BF16)
    gq_rows = jnp.broadcast_to(a_q_norm[0][:, None], (QK_HEAD, V7X_LANES))
    gk_pad = jnp.pad(a_k_norm[0], (0, QK_PAD - QK_HEAD))[None, :]
    w_o_a = a_w_o[0].astype(BF16)
    w_kvs = kv_w.astype(BF16)
    w_b_in = b_w_in[0].astype(BF16)
    w_o_b = b_w_o[0].astype(BF16)
    k_cols = N_GROUPS * DIL_WIDTH
    gk_cols = jnp.concatenate([jnp.tile(kv_k_norm, (1, DIL_HEADS)).reshape(1, k_cols),
                               jnp.ones((1, k_cols), F32)], axis=1)
    gq_cols = jnp.concatenate([jnp.tile(b_q_norm[0], (1, DIL_HEADS)).reshape(1, k_cols),
                               jnp.ones((1, DIL_WIDTH), F32)], axis=1)

    tabs, tabs_t = _rope_tables(seq)
    cqt, ckv, sg_a, kpe = _a_in(x2, a_ln, w_in_pad, a_ln_q, a_ln_kv, _tile(m, A_IN_ROW_TILE))
    blk = _tile(seq, MLA_KEY_CHUNK)
    q = _q_up(cqt, w_q_t, gq_rows, tabs_t, batch, seq, blk)
    k, vt = _kv_up(ckv.reshape(batch, seq, KV_LORA), kpe.reshape(batch, seq, V7X_LANES), w_kv, gk_pad, tabs, blk)
    o_a = _mla_attention(q, k, vt, blk)
    h1, xn_kv, xn_b = _a_out(o_a.reshape(m, MLA_WIDTH), sg_a, x2, w_o_a, kv_ln[None, :], b_ln,
                             _tile(m, 2 * OUT_ROW_CHUNK))

    q_scale = DIL_HEAD_DIM ** -0.5 * LOG2E
    qs, ks, vs = [], [], []
    for g, (_, dil) in enumerate(DIL_GROUPS):
        ks.append(_branch_proj(xn_kv, w_kvs, gk_cols, g * DIL_WIDTH, "norm", dil, 1.0, f"b_k_proj_d{dil}"))
        vs.append(_branch_proj(xn_kv, w_kvs, gk_cols, k_cols + g * DIL_WIDTH, "copy", dil, 1.0, f"b_v_proj_d{dil}"))
        qs.append(_branch_proj(xn_b, w_b_in, gq_cols, g * DIL_WIDTH, "norm", dil, q_scale, f"b_q_proj_d{dil}"))
    sg_b = _branch_proj(xn_b, w_b_in, gq_cols, k_cols, "silu", 1, 1.0, "b_gate_proj")
    a_b = _dilated_attention(qs, ks, vs, sg_b, batch, seq)
    out = _b_out(a_b, h1, w_o_b, _tile(m, 2 * OUT_ROW_CHUNK))
    return out.reshape(batch, seq, d)
```

```python
import functools
import math

import jax
import jax.numpy as jnp
from jax import lax
from jax.experimental import pallas as pl
from jax.experimental.pallas import tpu as pltpu

F32 = jnp.float32
BF16 = jnp.bfloat16

V7X_LANES = 128
V7X_VMEM_BYTES = 64 * 1024 * 1024
VMEM_LIMIT_BYTES = V7X_VMEM_BYTES * 7 // 8

D_MODEL = 2048
MLA_HEADS = 16
Q_LORA = 512
KV_LORA = 512
QK_NOPE = 128
QK_ROPE = 64
QK_HEAD = QK_NOPE + QK_ROPE
V_HEAD = 128
MLA_WIDTH = MLA_HEADS * V_HEAD
ROPE_THETA = 10000.0
DIL_GROUPS = ((128, 1), (512, 4), (2048, 16))
N_GROUPS = 3
DIL_HEADS = 16
DIL_HEAD_DIM = 128
DIL_WIDTH = DIL_HEADS * DIL_HEAD_DIM
LOOKBACK = 128
BAND_BLK = 128
ALIBI_MAX = 8.0
EPS = 1e-6
NEG_BIG = -1e30
LOG2E = math.log2(math.e)

QK_PAD = 2 * V7X_LANES
A_IN_PAD = Q_LORA + KV_LORA + MLA_WIDTH + V7X_LANES

DIL_TILE = BAND_BLK * DIL_GROUPS[-1][1]
DIL_HEAD_CHUNK = 2
DIL_COLS = DIL_HEAD_CHUNK * DIL_HEAD_DIM
V7X_BF16_SUBLANES = 16
A_IN_ROW_TILE = 512
A_IN_ROW_CHUNK = 256
UP_HEAD_GROUP = 4
OUT_ROW_CHUNK = 256
MLA_KEY_CHUNK = 512
MLA_UNROLL = 8
VT_ROWS = V_HEAD + V7X_BF16_SUBLANES
FIRST_SHIFT_KEYS = V7X_BF16_SUBLANES
LAG_LIMIT = 64.0
STRIDE_STEP = 4
PROJ_ROW_CHUNK = 256
PROJ_COL_TILE = 1024


def _params(semantics):
    return pltpu.CompilerParams(dimension_semantics=semantics, vmem_limit_bytes=VMEM_LIMIT_BYTES)


def _inv_rms(x, n):
    return lax.rsqrt(jnp.sum(x * x, axis=-1, keepdims=True) * (1.0 / n) + EPS)


def _rope(rv, cos_t, s1_t, s2_t):
    return rv * cos_t + pltpu.roll(rv, 96, 1) * s1_t + pltpu.roll(rv, 32, 1) * s2_t


def _rope_table_kernel(inv_ref, invt_ref, cos_ref, s1_ref, s2_ref, cost_ref, sint_ref):
    ts = cos_ref.shape[0]
    row = lax.broadcasted_iota(jnp.int32, (ts, V7X_LANES), 0) + pl.program_id(0) * ts
    lane = lax.broadcasted_iota(jnp.int32, (ts, V7X_LANES), 1)
    ang = row.astype(F32) * inv_ref[...]
    c = jnp.cos(ang)
    s = jnp.sin(ang)
    half = QK_ROPE // 2
    cos_ref[...] = jnp.where(lane < QK_ROPE, c, 0.0)
    s1_ref[...] = jnp.where(lane < half, -s, 0.0)
    s2_ref[...] = jnp.where((lane >= half) & (lane < QK_ROPE), s, 0.0)
    pos = lax.broadcasted_iota(jnp.int32, (half, ts), 1) + pl.program_id(0) * ts
    ang_t = pos.astype(F32) * jnp.tile(invt_ref[...], (1, ts // V7X_LANES))
    cost_ref[...] = jnp.cos(ang_t)
    sint_ref[...] = jnp.sin(ang_t)


def _rope_tables(seq):
    half = QK_ROPE // 2
    inv = ROPE_THETA ** (-jnp.arange(half, dtype=F32) / half)
    inv_lanes = jnp.concatenate([inv, inv, jnp.zeros((V7X_LANES - QK_ROPE,), F32)])[None, :]
    inv_rows = jnp.broadcast_to(inv[:, None], (half, V7X_LANES))
    ts = min(seq, 1024)
    tab = jax.ShapeDtypeStruct((seq, V7X_LANES), F32)
    tab_t = jax.ShapeDtypeStruct((half, seq), F32)
    spec = pl.BlockSpec((ts, V7X_LANES), lambda i: (i, 0))
    spec_t = pl.BlockSpec((half, ts), lambda i: (0, i))
    outs = pl.pallas_call(
        _rope_table_kernel,
        grid=(seq // ts,),
        in_specs=[pl.BlockSpec((1, V7X_LANES), lambda i: (0, 0)), pl.BlockSpec((half, V7X_LANES), lambda i: (0, 0))],
        out_specs=[spec, spec, spec, spec_t, spec_t],
        out_shape=[tab, tab, tab, tab_t, tab_t],
        name="rope_tables",
        compiler_params=_params(("arbitrary",)),
    )(inv_lanes, inv_rows)
    return outs[:3], outs[3:]


def _a_in_kernel(x_ref, ln_ref, w_ref, lnq_ref, lnkv_ref, cq_ref, ckv_ref, sg_ref, kpe_ref):
    g0 = Q_LORA + KV_LORA
    for c in range(x_ref.shape[0] // A_IN_ROW_CHUNK):
        rows = slice(c * A_IN_ROW_CHUNK, (c + 1) * A_IN_ROW_CHUNK)
        x = x_ref[rows, :]
        xn = ((x * _inv_rms(x, D_MODEL)) * ln_ref[...]).astype(BF16)
        u = jnp.dot(xn, w_ref[...], preferred_element_type=F32)
        cq = u[:, :Q_LORA]
        cq_ref[:, rows] = ((cq * _inv_rms(cq, Q_LORA)) * lnq_ref[...]).T.astype(BF16)
        ckv = u[:, Q_LORA:g0]
        ckv_ref[rows, :] = ((ckv * _inv_rms(ckv, KV_LORA)) * lnkv_ref[...]).astype(BF16)
        gate = u[:, g0:g0 + MLA_WIDTH]
        sg_ref[rows, :] = (gate * jax.nn.sigmoid(gate)).astype(BF16)
        kpe_ref[rows, :] = u[:, g0 + MLA_WIDTH:]


def _a_in(x2, ln, w_pad, ln_q, ln_kv, tm):
    m = x2.shape[0]
    row = lambda n: pl.BlockSpec((tm, n), lambda i: (i, 0))
    full = lambda a: pl.BlockSpec(a.shape, lambda i: (0, 0))
    return pl.pallas_call(
        _a_in_kernel,
        grid=(m // tm,),
        in_specs=[row(D_MODEL), full(ln),
                  pl.BlockSpec(w_pad.shape, lambda i: (0, 0), pipeline_mode=pl.Buffered(1)),
                  full(ln_q), full(ln_kv)],
        out_specs=[pl.BlockSpec((Q_LORA, tm), lambda i: (0, i)), row(KV_LORA), row(MLA_WIDTH), row(V7X_LANES)],
        out_shape=[jax.ShapeDtypeStruct((Q_LORA, m), BF16), jax.ShapeDtypeStruct((m, KV_LORA), BF16),
                   jax.ShapeDtypeStruct((m, MLA_WIDTH), BF16), jax.ShapeDtypeStruct((m, V7X_LANES), F32)],
        name="a_in_proj",
        compiler_params=_params(("arbitrary",)),
    )(x2, ln, w_pad, ln_q, ln_kv)


def _q_up_kernel(cqt_ref, wt_ref, g_ref, cos_ref, sin_ref, q_ref, *, scale):
    tm = cqt_ref.shape[1]
    g = jnp.tile(g_ref[...], (1, tm // V7X_LANES))
    cos_t, sin_t = cos_ref[...], sin_ref[...]
    half = QK_ROPE // 2
    zeros = jnp.zeros((QK_PAD - QK_HEAD, tm), BF16)
    for h in range(MLA_HEADS):
        if h % UP_HEAD_GROUP == 0:
            w_rows = slice(h * QK_HEAD, (h + UP_HEAD_GROUP) * QK_HEAD)
            yt = jnp.dot(wt_ref[w_rows, :], cqt_ref[...], preferred_element_type=F32)
        hl = h % UP_HEAD_GROUP
        blk = yt[hl * QK_HEAD:(hl + 1) * QK_HEAD, :]
        r = lax.rsqrt(jnp.sum(blk * blk, axis=0, keepdims=True) * (1.0 / QK_HEAD) + EPS)
        z = (blk * r) * g
        x1, x2 = z[QK_NOPE:QK_NOPE + half, :], z[QK_NOPE + half:, :]
        q_ref[h, :QK_NOPE, :] = (z[:QK_NOPE, :] * scale).astype(BF16)
        q_ref[h, QK_NOPE:QK_NOPE + half, :] = ((x1 * cos_t - x2 * sin_t) * scale).astype(BF16)
        q_ref[h, QK_NOPE + half:QK_HEAD, :] = ((x2 * cos_t + x1 * sin_t) * scale).astype(BF16)
        q_ref[h, QK_HEAD:, :] = zeros


def _kv_up_kernel(ckv_ref, kpe_ref, wk_ref, wvt_ref, g_ref, cos_ref, s1_ref, s2_ref, k_ref, vt_ref):
    g_n = g_ref[:, :QK_NOPE]
    g_r = g_ref[:, QK_NOPE:]
    kpe = kpe_ref[...]
    ss_pe = jnp.sum(kpe * kpe, axis=-1, keepdims=True)
    cos_t, s1_t, s2_t = cos_ref[...], s1_ref[...], s2_ref[...]
    pad_row = lax.broadcasted_iota(jnp.int32, (VT_ROWS - V_HEAD, ckv_ref.shape[0]), 0)
    ones_rows = jnp.where(pad_row == 0, 1.0, 0.0).astype(BF16)
    kr = _rope(kpe * g_r, cos_t, s1_t, s2_t)
    ckv = ckv_ref[...]
    for h in range(MLA_HEADS):
        if h % UP_HEAD_GROUP == 0:
            grp = slice(h * QK_NOPE, (h + UP_HEAD_GROUP) * QK_NOPE)
            yk = jnp.dot(ckv, wk_ref[:, grp], preferred_element_type=F32)
            vt = lax.dot_general(wvt_ref[grp, :], ckv, (((1,), (1,)), ((), ())),
                                 preferred_element_type=F32)
        hl = slice((h % UP_HEAD_GROUP) * QK_NOPE, (h % UP_HEAD_GROUP + 1) * QK_NOPE)
        kn = yk[:, hl]
        r = lax.rsqrt((jnp.sum(kn * kn, axis=-1, keepdims=True) + ss_pe) * (1.0 / QK_HEAD) + EPS)
        k_ref[h, :, :QK_NOPE] = ((kn * r) * g_n).astype(BF16)
        k_ref[h, :, QK_NOPE:] = (kr * r).astype(BF16)
        vt_ref[h, :V_HEAD, :] = vt[hl, :].astype(BF16)
        vt_ref[h, V_HEAD:, :] = ones_rows


def _seq_spec(tm, n):
    return pl.BlockSpec((None, tm, n), lambda b, i: (b, i, 0))


def _tab_spec(tm):
    return pl.BlockSpec((tm, V7X_LANES), lambda b, i: (i, 0))


def _head_spec(tm, n):
    return pl.BlockSpec((None, MLA_HEADS, tm, n), lambda b, i: (b, 0, i, 0))


def _q_up(cqt, wt, g_rows, tabs_t, batch, seq, tm):
    full = lambda a: pl.BlockSpec(a.shape, lambda bb, i: (0, 0))
    scale = QK_HEAD ** -0.5 * LOG2E
    nblk = seq // tm
    tab = pl.BlockSpec((QK_ROPE // 2, tm), lambda bb, i: (0, i))
    return pl.pallas_call(
        functools.partial(_q_up_kernel, scale=scale),
        grid=(batch, nblk),
        in_specs=[pl.BlockSpec((Q_LORA, tm), lambda bb, i: (0, bb * nblk + i)), full(wt), full(g_rows), tab, tab],
        out_specs=pl.BlockSpec((None, MLA_HEADS, QK_PAD, tm), lambda bb, i: (bb, 0, 0, i)),
        out_shape=jax.ShapeDtypeStruct((batch, MLA_HEADS, QK_PAD, seq), BF16),
        name="a_q_up",
        compiler_params=_params(("arbitrary", "arbitrary")),
    )(cqt, wt, g_rows, *tabs_t)


def _kv_up(ckv3, kpe3, wk, wvt, g_pad, tabs, tm):
    b, s, _ = ckv3.shape
    full = lambda a: pl.BlockSpec(a.shape, lambda bb, i: (0, 0))
    return pl.pallas_call(
        _kv_up_kernel,
        grid=(b, s // tm),
        in_specs=[_seq_spec(tm, KV_LORA), _seq_spec(tm, V7X_LANES), full(wk), full(wvt), full(g_pad),
                  _tab_spec(tm), _tab_spec(tm), _tab_spec(tm)],
        out_specs=[_head_spec(tm, QK_PAD),
                   pl.BlockSpec((None, MLA_HEADS, None, VT_ROWS, tm), lambda bb, i: (bb, 0, i, 0, 0))],
        out_shape=[jax.ShapeDtypeStruct((b, MLA_HEADS, s, QK_PAD), BF16),
                   jax.ShapeDtypeStruct((b, MLA_HEADS, s // tm, VT_ROWS, tm), BF16)],
        name="a_kv_up",
        compiler_params=_params(("arbitrary", "arbitrary")),
    )(ckv3, kpe3, wk, wvt, g_pad, *tabs)


def _mla_attn_kernel(q_ref, k_ref, vt_ref, o_ref, sig_sc, viol_sc, acc_sc, *, tq, tk):
    qi = pl.program_id(2)
    qt = q_ref[...]
    n_chunks = 2 * qi + 2

    def qk(kc):
        k = k_ref[pl.ds(pl.multiple_of(kc * tk, tk), tk), :]
        return jnp.dot(k, qt, preferred_element_type=F32)

    def causal(s, key_offset):
        key = lax.broadcasted_iota(jnp.int32, (tk, tq), 0) + key_offset
        qry = lax.broadcasted_iota(jnp.int32, (tk, tq), 1)
        return jnp.where(key <= qry, s, NEG_BIG)

    def finish():
        o_ref[...] = (acc_sc[:V_HEAD, :] / acc_sc[V_HEAD:V_HEAD + 1, :]).T.astype(BF16)

    def lagged_chunk(kc, key_offset):
        col0 = 0 if key_offset is None else key_offset
        cs = slice(col0, tq)
        k = k_ref[pl.ds(pl.multiple_of(kc * tk, tk), tk), :]
        s = jnp.dot(k, qt[:, cs], preferred_element_type=F32)
        if key_offset is not None:
            key = lax.broadcasted_iota(jnp.int32, s.shape, 0)
            qry = lax.broadcasted_iota(jnp.int32, s.shape, 1)
            s = jnp.where(key <= qry, s, NEG_BIG)
        shift = sig_sc[:, cs]
        mx = jnp.max(s, axis=0, keepdims=True)
        p = jnp.exp2(s - shift).astype(BF16)
        new_shift = jnp.maximum(shift, mx)
        pv = jnp.dot(vt_ref[kc], p, preferred_element_type=F32)
        acc_sc[:, cs] = (acc_sc[:, cs] + pv) * jnp.exp2(shift - new_shift)
        sig_sc[:, cs] = new_shift
        viol_sc[:, cs] = jnp.maximum(viol_sc[:, cs], mx - shift)

    s0 = jnp.dot(k_ref[:FIRST_SHIFT_KEYS, :], qt, preferred_element_type=F32)
    key0 = lax.broadcasted_iota(jnp.int32, s0.shape, 0) - qi * tq
    s0 = jnp.where(key0 <= lax.broadcasted_iota(jnp.int32, s0.shape, 1), s0, NEG_BIG)
    sig_sc[...] = jnp.max(s0, axis=0, keepdims=True)
    viol_sc[...] = jnp.zeros(viol_sc.shape, F32)
    acc_sc[...] = jnp.zeros(acc_sc.shape, F32)

    def octet(io, carry):
        for u in range(MLA_UNROLL):
            lagged_chunk(MLA_UNROLL * io + u, None)
        return carry

    per_iter = MLA_UNROLL // 2
    n_oct = qi // per_iter
    lax.fori_loop(0, n_oct, octet, 0)
    for rem in range(per_iter):
        @pl.when(qi % per_iter == rem)
        def _(rem=rem):
            for u in range(2 * rem):
                lagged_chunk(MLA_UNROLL * n_oct + u, None)
            lagged_chunk(2 * qi, 0)
            lagged_chunk(2 * qi + 1, tk)
    finish()

    @pl.when(jnp.max(viol_sc[...]) > LAG_LIMIT)
    def _():
        sig_sc[...] = jnp.full(sig_sc.shape, NEG_BIG, F32)
        acc_sc[...] = jnp.zeros(acc_sc.shape, F32)

        def exact_chunk(kc, carry):
            s = causal(qk(kc), kc * tk - qi * tq)
            m_old = sig_sc[...]
            m_new = jnp.maximum(m_old, jnp.max(s, axis=0, keepdims=True))
            p = jnp.exp2(s - m_new).astype(BF16)
            acc_sc[...] = jnp.exp2(m_old - m_new) * acc_sc[...] + jnp.dot(vt_ref[kc], p, preferred_element_type=F32)
            sig_sc[...] = m_new
            return carry

        lax.fori_loop(0, n_chunks, exact_chunk, 0)
        finish()


def _mla_attention(qt, k, vt, tk):
    b, h, s, _ = k.shape
    tq = 2 * tk
    assert vt.shape[-1] == tk and s % tq == 0
    return pl.pallas_call(
        functools.partial(_mla_attn_kernel, tq=tq, tk=tk),
        grid=(b, h, s // tq),
        in_specs=[pl.BlockSpec((None, None, QK_PAD, tq), lambda bb, hh, i: (bb, hh, 0, i)),
                  pl.BlockSpec((None, None, s, QK_PAD), lambda bb, hh, i: (bb, hh, 0, 0)),
                  pl.BlockSpec((None, None, s // tk, VT_ROWS, tk), lambda bb, hh, i: (bb, hh, 0, 0, 0))],
        out_specs=pl.BlockSpec((None, tq, V_HEAD), lambda bb, hh, i: (bb, i, hh)),
        out_shape=jax.ShapeDtypeStruct((b, s, MLA_WIDTH), BF16),
        scratch_shapes=[pltpu.VMEM((1, tq), F32), pltpu.VMEM((1, tq), F32), pltpu.VMEM((VT_ROWS, tq), F32)],
        name="a_mla_attention",
        compiler_params=_params(("arbitrary", "arbitrary", "arbitrary")),
    )(qt, k, vt)


def _a_out_kernel(o_ref, sg_ref, x_ref, w_ref, lnkv_ref, lnb_ref, h_ref, xkv_ref, xb_ref):
    tm = o_ref.shape[0]
    for c in range(tm // OUT_ROW_CHUNK):
        rows = slice(c * OUT_ROW_CHUNK, (c + 1) * OUT_ROW_CHUNK)
        a = (o_ref[rows, :] * sg_ref[rows, :]).astype(BF16)
        h = x_ref[rows, :] + jnp.dot(a, w_ref[...], preferred_element_type=F32)
        h_ref[rows, :] = h
        hn = h * _inv_rms(h, D_MODEL)
        xkv_ref[rows, :] = (hn * lnkv_ref[...]).astype(BF16)
        xb_ref[rows, :] = (hn * lnb_ref[...]).astype(BF16)


def _a_out(o2, sg, x2, w, ln_kv, ln_b, tm):
    m = x2.shape[0]
    row = pl.BlockSpec((tm, D_MODEL), lambda i: (i, 0))
    full = lambda a: pl.BlockSpec(a.shape, lambda i: (0, 0))
    return pl.pallas_call(
        _a_out_kernel,
        grid=(m // tm,),
        in_specs=[row, row, row, pl.BlockSpec(w.shape, lambda i: (0, 0), pipeline_mode=pl.Buffered(1)),
                  full(ln_kv), full(ln_b)],
        out_specs=[row, row, row],
        out_shape=[jax.ShapeDtypeStruct((m, D_MODEL), F32), jax.ShapeDtypeStruct((m, D_MODEL), BF16),
                   jax.ShapeDtypeStruct((m, D_MODEL), BF16)],
        name="a_out_proj",
        compiler_params=_params(("arbitrary",)),
    )(o2, sg, x2, w, ln_kv, ln_b)


def _branch_proj_kernel(x_ref, w_ref, g_ref, o_ref, y_sc, y2_sc, *, mode, dilation, scale):
    tm, tn = o_ref.shape
    rows = PROJ_ROW_CHUNK // dilation
    for c in range(tm // PROJ_ROW_CHUNK):
        y = jnp.dot(x_ref[c * PROJ_ROW_CHUNK:(c + 1) * PROJ_ROW_CHUNK, :], w_ref[...], preferred_element_type=F32)
        for hc in range(tn // DIL_HEAD_DIM):
            sl = slice(hc * DIL_HEAD_DIM, (hc + 1) * DIL_HEAD_DIM)
            yh = y[:, sl]
            if mode == "norm":
                yh = (yh * _inv_rms(yh, DIL_HEAD_DIM)) * g_ref[:, sl]
                if scale != 1.0:
                    yh = yh * scale
            elif mode == "silu":
                yh = yh * jax.nn.sigmoid(yh)
            if dilation == 1:
                o_ref[c * PROJ_ROW_CHUNK:(c + 1) * PROJ_ROW_CHUNK, sl] = yh.astype(BF16)
            else:
                base = c * PROJ_ROW_CHUNK
                y_sc[hc, base:base + PROJ_ROW_CHUNK, :] = yh
                if dilation == STRIDE_STEP * STRIDE_STEP:
                    quarter = PROJ_ROW_CHUNK // STRIDE_STEP
                    for a in range(STRIDE_STEP):
                        y2_sc[hc, base + a * quarter:base + (a + 1) * quarter, :] = (
                            y_sc[hc, pl.ds(base + a, quarter, stride=STRIDE_STEP), :])
                    for r in range(dilation):
                        a, b = r % STRIDE_STEP, r // STRIDE_STEP
                        dst = r * (tm // dilation) + c * rows
                        o_ref[dst:dst + rows, sl] = (
                            y2_sc[hc, pl.ds(base + a * quarter + b, rows, stride=STRIDE_STEP), :].astype(BF16))
                else:
                    for r in range(dilation):
                        dst = r * (tm // dilation) + c * rows
                        o_ref[dst:dst + rows, sl] = y_sc[hc, pl.ds(base + r, rows, stride=dilation), :].astype(BF16)


def _branch_proj(xn, w, g_cols, col0, mode, dilation, scale, name):
    m, kdim = xn.shape
    tm, tn = DIL_TILE, PROJ_COL_TILE
    blk0 = col0 // tn
    return pl.pallas_call(
        functools.partial(_branch_proj_kernel, mode=mode, dilation=dilation, scale=scale),
        grid=(m // tm, DIL_WIDTH // tn),
        in_specs=[pl.BlockSpec((tm, kdim), lambda i, j: (i, 0)),
                  pl.BlockSpec((kdim, tn), lambda i, j: (0, blk0 + j)),
                  pl.BlockSpec((1, tn), lambda i, j: (0, blk0 + j))],
        out_specs=pl.BlockSpec((tm, tn), lambda i, j: (i, j)),
        out_shape=jax.ShapeDtypeStruct((m, DIL_WIDTH), BF16),
        scratch_shapes=[pltpu.VMEM((tn // DIL_HEAD_DIM, tm, DIL_HEAD_DIM), F32)] * 2,
        name=name,
        compiler_params=_params(("arbitrary", "arbitrary")),
    )(xn, w, g_cols)


def _dilated_attn_kernel(*refs):
    qs, kps, kcs, vps, vcs = (refs[i * N_GROUPS:(i + 1) * N_GROUPS] for i in range(5))
    sg_ref, out_ref, o_sc, lse_sc = refs[5 * N_GROUPS:]
    t = pl.program_id(1)
    c = pl.program_id(2)
    row = lax.broadcasted_iota(jnp.int32, (BAND_BLK, 2 * BAND_BLK), 0)
    col = lax.broadcasted_iota(jnp.int32, (BAND_BLK, 2 * BAND_BLK), 1)
    rel = row + BAND_BLK - col
    band = (rel >= 0) & (rel <= LOOKBACK)
    first_band = band & (col >= jnp.where(t > 0, 0, BAND_BLK))
    relf = rel.astype(F32)
    assert DIL_GROUPS[0][1] == 1
    for g in (1, 2, 0):
        dil = DIL_GROUPS[g][1]
        nblk = DIL_TILE // dil // BAND_BLK
        q_ref, kp_ref, kc_ref, vp_ref, vc_ref = qs[g], kps[g], kcs[g], vps[g], vcs[g]
        for hh in range(DIL_HEAD_CHUNK):
            hs = slice(hh * DIL_HEAD_DIM, (hh + 1) * DIL_HEAD_DIM)
            head = (c * DIL_HEAD_CHUNK + hh + 1).astype(F32)
            slope = jnp.exp2(jnp.full((1, 2 * BAND_BLK), -ALIBI_MAX / DIL_HEADS, F32) * head)
            bias = (slope * (-float(dil) * LOG2E)) * relf
            bias_band = jnp.where(band, bias, NEG_BIG)
            bias_first = jnp.where(first_band, bias, NEG_BIG)
            for r in range(dil):
                for a in range(nblk):
                    q = q_ref[r, a, :, hs]
                    k_lo = kp_ref[r, :, hs] if a == 0 else kc_ref[r, a - 1, :, hs]
                    v_lo = vp_ref[r, :, hs] if a == 0 else vc_ref[r, a - 1, :, hs]
                    k = jnp.concatenate([k_lo, kc_ref[r, a, :, hs]], axis=0)
                    v = jnp.concatenate([v_lo, vc_ref[r, a, :, hs]], axis=0)
                    s = lax.dot_general(q, k, (((1,), (1,)), ((), ())), preferred_element_type=F32)
                    s = s + (bias_first if a == 0 else bias_band)
                    m = jnp.max(s, axis=-1, keepdims=True)
                    p = jnp.exp2(s - m)
                    den = jnp.sum(p, axis=-1, keepdims=True)
                    o = jnp.dot(p.astype(BF16), v, preferred_element_type=F32) / den
                    lse = jnp.broadcast_to(m + jnp.log2(den), (BAND_BLK, V7X_LANES))
                    if dil > 1:
                        tok = pl.ds(a * BAND_BLK * dil + r, BAND_BLK, stride=dil)
                        o_sc[g - 1, hh, tok, :] = o
                        lse_sc[g - 1, hh, tok, :] = lse
                    else:
                        tok = slice(a * BAND_BLK, (a + 1) * BAND_BLK)
                        l1, l2 = lse_sc[0, hh, tok, :], lse_sc[1, hh, tok, :]
                        mx = jnp.maximum(jnp.maximum(lse, l1), l2)
                        e0, e1, e2 = jnp.exp2(lse - mx), jnp.exp2(l1 - mx), jnp.exp2(l2 - mx)
                        merged = (e0 * o + e1 * o_sc[0, hh, tok, :] + e2 * o_sc[1, hh, tok, :]) / (e0 + e1 + e2)
                        out_ref[tok, hs] = (merged * sg_ref[tok, hs].astype(F32)).astype(BF16)


def _dilated_attention(qs, ks, vs, gate, batch, seq):
    nt = seq // DIL_TILE
    ins, specs = [], []

    def add(arrs, prev):
        for arr, (_, dil) in zip(arrs, DIL_GROUPS):
            nblk = DIL_TILE // dil // BAND_BLK
            ins.append(arr.reshape(batch, nt, dil, nblk, BAND_BLK, DIL_WIDTH))
            if prev:
                specs.append(pl.BlockSpec((None, None, dil, None, BAND_BLK, DIL_COLS),
                                          lambda b, t, c, nblk=nblk: (b, jnp.maximum(t - 1, 0), 0, nblk - 1, 0, c)))
            else:
                specs.append(pl.BlockSpec((None, None, dil, nblk, BAND_BLK, DIL_COLS),
                                          lambda b, t, c: (b, t, 0, 0, 0, c)))

    add(qs, False)
    add(ks, True)
    add(ks, False)
    add(vs, True)
    add(vs, False)
    ins.append(gate.reshape(batch, nt, DIL_TILE, DIL_WIDTH))
    specs.append(pl.BlockSpec((None, None, DIL_TILE, DIL_COLS), lambda b, t, c: (b, t, 0, c)))
    out = pl.pallas_call(
        _dilated_attn_kernel,
        grid=(batch, nt, DIL_HEADS // DIL_HEAD_CHUNK),
        in_specs=specs,
        out_specs=pl.BlockSpec((None, None, DIL_TILE, DIL_COLS), lambda b, t, c: (b, t, 0, c)),
        out_shape=jax.ShapeDtypeStruct((batch, nt, DIL_TILE, DIL_WIDTH), BF16),
        scratch_shapes=[pltpu.VMEM((N_GROUPS - 1, DIL_HEAD_CHUNK, DIL_TILE, DIL_HEAD_DIM), F32),
                        pltpu.VMEM((N_GROUPS - 1, DIL_HEAD_CHUNK, DIL_TILE, V7X_LANES), F32)],
        name="b_dilated_attention",
        compiler_params=_params(("arbitrary", "arbitrary", "arbitrary")),
    )(*ins)
    return out.reshape(batch * seq, DIL_WIDTH)


def _b_out_kernel(a_ref, h_ref, w_ref, out_ref):
    for c in range(out_ref.shape[0] // OUT_ROW_CHUNK):
        rows = slice(c * OUT_ROW_CHUNK, (c + 1) * OUT_ROW_CHUNK)
        out_ref[rows, :] = h_ref[rows, :] + jnp.dot(a_ref[rows, :], w_ref[...], preferred_element_type=F32)


def _b_out(a, h1, w, tm):
    m = h1.shape[0]
    row = pl.BlockSpec((tm, D_MODEL), lambda i: (i, 0))
    return pl.pallas_call(
        _b_out_kernel,
        grid=(m // tm,),
        in_specs=[row, row, pl.BlockSpec(w.shape, lambda i: (0, 0), pipeline_mode=pl.Buffered(1))],
        out_specs=row,
        out_shape=jax.ShapeDtypeStruct((m, D_MODEL), F32),
        name="b_out_proj",
        compiler_params=_params(("arbitrary",)),
    )(a, h1, w)


def _tile(n, want):
    t = min(n, want)
    assert n % t == 0, (n, t)
    return t


def kernel(x, a_ln, a_w_in, a_ln_q, a_w_q_up, a_ln_kv, a_w_kv_up, a_q_norm, a_k_norm, a_w_o,
           kv_ln, kv_w, kv_k_norm, b_ln, b_w_in, b_q_norm, b_w_o):
    batch, seq, d = x.shape
    assert d == D_MODEL and a_ln.shape[0] == 1 and b_ln.shape[0] == 1
    assert seq % DIL_TILE == 0
    m = batch * seq
    x2 = x.reshape(m, d)

    lat = Q_LORA + KV_LORA
    w_in = a_w_in[0]
    w_in_pad = jnp.concatenate(
        [w_in[:, :lat], w_in[:, lat + QK_ROPE:], w_in[:, lat:lat + QK_ROPE],
         jnp.zeros((d, V7X_LANES - QK_ROPE), F32)], axis=1).astype(BF16)
    assert w_in_pad.shape == (d, A_IN_PAD)
    w_q_t = a_w_q_up[0].T.astype(BF16)
    w_kv = a_w_kv_up[0].reshape(KV_LORA, MLA_HEADS, QK_NOPE + V_HEAD)
    w_k = w_kv[:, :, :QK_NOPE].reshape(KV_LORA, MLA_HEADS * QK_NOPE).astype(BF16)
    w_vt = w_kv[:, :, QK_NOPE:].reshape(KV_LORA, MLA_HEADS * V_HEAD).T.astype(BF16)
    gq_rows = jnp.broadcast_to(a_q_norm[0][:, None], (QK_HEAD, V7X_LANES))
    gk_pad = jnp.pad(a_k_norm[0], (0, QK_PAD - QK_HEAD))[None, :]
    w_o_a = a_w_o[0].astype(BF16)
    w_kvs = kv_w.astype(BF16)
    w_b_in = b_w_in[0].astype(BF16)
    w_o_b = b_w_o[0].astype(BF16)
    k_cols = N_GROUPS * DIL_WIDTH
    gk_cols = jnp.concatenate([jnp.tile(kv_k_norm, (1, DIL_HEADS)).reshape(1, k_cols),
                               jnp.ones((1, k_cols), F32)], axis=1)
    gq_cols = jnp.concatenate([jnp.tile(b_q_norm[0], (1, DIL_HEADS)).reshape(1, k_cols),
                               jnp.ones((1, DIL_WIDTH), F32)], axis=1)

    tabs, tabs_t = _rope_tables(seq)
    cqt, ckv, sg_a, kpe = _a_in(x2, a_ln, w_in_pad, a_ln_q, a_ln_kv, _tile(m, A_IN_ROW_TILE))
    blk = _tile(seq, MLA_KEY_CHUNK)
    q = _q_up(cqt, w_q_t, gq_rows, tabs_t, batch, seq, blk)
    k, vt = _kv_up(ckv.reshape(batch, seq, KV_LORA), kpe.reshape(batch, seq, V7X_LANES), w_k, w_vt, gk_pad, tabs, blk)
    o_a = _mla_attention(q, k, vt, blk)
    h1, xn_kv, xn_b = _a_out(o_a.reshape(m, MLA_WIDTH), sg_a, x2, w_o_a, kv_ln[None, :], b_ln,
                             _tile(m, 2 * OUT_ROW_CHUNK))

    q_scale = DIL_HEAD_DIM ** -0.5 * LOG2E
    qs, ks, vs = [], [], []
    for g, (_, dil) in enumerate(DIL_GROUPS):
        ks.append(_branch_proj(xn_kv, w_kvs, gk_cols, g * DIL_WIDTH, "norm", dil, 1.0, f"b_k_proj_d{dil}"))
        vs.append(_branch_proj(xn_kv, w_kvs, gk_cols, k_cols + g * DIL_WIDTH, "copy", dil, 1.0, f"b_v_proj_d{dil}"))
        qs.append(_branch_proj(xn_b, w_b_in, gq_cols, g * DIL_WIDTH, "norm", dil, q_scale, f"b_q_proj_d{dil}"))
    sg_b = _branch_proj(xn_b, w_b_in, gq_cols, k_cols, "silu", 1, 1.0, "b_gate_proj")
    a_b = _dilated_attention(qs, ks, vs, sg_b, batch, seq)
    out = _b_out(a_b, h1, w_o_b, _tile(m, 2 * OUT_ROW_CHUNK))
    return out.reshape(batch, seq, d)
```

```python
import functools
import math

import jax
import jax.numpy as jnp
from jax import lax
from jax.experimental import pallas as pl
from jax.experimental.pallas import tpu as pltpu

F32 = jnp.float32
BF16 = jnp.bfloat16

V7X_LANES = 128
V7X_VMEM_BYTES = 64 * 1024 * 1024
VMEM_LIMIT_BYTES = V7X_VMEM_BYTES * 7 // 8

D_MODEL = 2048
MLA_HEADS = 16
Q_LORA = 512
KV_LORA = 512
QK_NOPE = 128
QK_ROPE = 64
QK_HEAD = QK_NOPE + QK_ROPE
V_HEAD = 128
MLA_WIDTH = MLA_HEADS * V_HEAD
ROPE_THETA = 10000.0
DIL_GROUPS = ((128, 1), (512, 4), (2048, 16))
N_GROUPS = 3
DIL_HEADS = 16
DIL_HEAD_DIM = 128
DIL_WIDTH = DIL_HEADS * DIL_HEAD_DIM
LOOKBACK = 128
BAND_BLK = 128
ALIBI_MAX = 8.0
EPS = 1e-6
NEG_BIG = -1e30
LOG2E = math.log2(math.e)

QK_PAD = 2 * V7X_LANES
A_IN_PAD = Q_LORA + KV_LORA + MLA_WIDTH + V7X_LANES

DIL_TILE = BAND_BLK * DIL_GROUPS[-1][1]
DIL_HEAD_CHUNK = 2
DIL_COLS = DIL_HEAD_CHUNK * DIL_HEAD_DIM
V7X_BF16_SUBLANES = 16
A_IN_ROW_TILE = 512
A_IN_ROW_CHUNK = 256
UP_HEAD_GROUP = 4
OUT_ROW_CHUNK = 256
MLA_KEY_CHUNK = 512
MLA_UNROLL = 8
VT_ROWS = V_HEAD + V7X_BF16_SUBLANES
FIRST_SHIFT_KEYS = V7X_BF16_SUBLANES
LAG_LIMIT = 64.0
STRIDE_STEP = 4
PROJ_ROW_CHUNK = 256
PROJ_COL_TILE = 1024


def _params(semantics):
    return pltpu.CompilerParams(dimension_semantics=semantics, vmem_limit_bytes=VMEM_LIMIT_BYTES)


def _inv_rms(x, n):
    return lax.rsqrt(jnp.sum(x * x, axis=-1, keepdims=True) * (1.0 / n) + EPS)


def _rope(rv, cos_t, s1_t, s2_t):
    return rv * cos_t + pltpu.roll(rv, 96, 1) * s1_t + pltpu.roll(rv, 32, 1) * s2_t


def _rope_table_kernel(invt_ref, cos_ref, s1_ref, s2_ref, cost_ref, sint_ref):
    ts = cos_ref.shape[0]
    half = QK_ROPE // 2
    pos = lax.broadcasted_iota(jnp.int32, (half, ts), 1) + pl.program_id(0) * ts
    ang_t = pos.astype(F32) * jnp.tile(invt_ref[...], (1, ts // V7X_LANES))
    c_t = jnp.cos(ang_t)
    s_t = jnp.sin(ang_t)
    cost_ref[...] = c_t
    sint_ref[...] = s_t
    c = c_t.T
    s = s_t.T
    z32 = jnp.zeros((ts, half), F32)
    z64 = jnp.zeros((ts, V7X_LANES - QK_ROPE), F32)
    cos_ref[...] = jnp.concatenate([c, c, z64], axis=1)
    s1_ref[...] = jnp.concatenate([-s, z32, z64], axis=1)
    s2_ref[...] = jnp.concatenate([z32, s, z64], axis=1)


def _rope_tables(seq):
    half = QK_ROPE // 2
    inv = ROPE_THETA ** (-jnp.arange(half, dtype=F32) / half)
    inv_rows = jnp.broadcast_to(inv[:, None], (half, V7X_LANES))
    ts = min(seq, 1024)
    tab = jax.ShapeDtypeStruct((seq, V7X_LANES), F32)
    tab_t = jax.ShapeDtypeStruct((half, seq), F32)
    spec = pl.BlockSpec((ts, V7X_LANES), lambda i: (i, 0))
    spec_t = pl.BlockSpec((half, ts), lambda i: (0, i))
    outs = pl.pallas_call(
        _rope_table_kernel,
        grid=(seq // ts,),
        in_specs=[pl.BlockSpec((half, V7X_LANES), lambda i: (0, 0))],
        out_specs=[spec, spec, spec, spec_t, spec_t],
        out_shape=[tab, tab, tab, tab_t, tab_t],
        name="rope_tables",
        compiler_params=_params(("arbitrary",)),
    )(inv_rows)
    return outs[:3], outs[3:]


def _a_in_kernel(x_ref, ln_ref, w_ref, lnq_ref, lnkv_ref, cq_ref, ckv_ref, sg_ref, kpe_ref):
    g0 = Q_LORA + KV_LORA
    for c in range(x_ref.shape[0] // A_IN_ROW_CHUNK):
        rows = slice(c * A_IN_ROW_CHUNK, (c + 1) * A_IN_ROW_CHUNK)
        x = x_ref[rows, :]
        xn = ((x * _inv_rms(x, D_MODEL)) * ln_ref[...]).astype(BF16)
        u = jnp.dot(xn, w_ref[...], preferred_element_type=F32)
        cq = u[:, :Q_LORA]
        cq_ref[:, rows] = ((cq * _inv_rms(cq, Q_LORA)) * lnq_ref[...]).T.astype(BF16)
        ckv = u[:, Q_LORA:g0]
        ckv_ref[rows, :] = ((ckv * _inv_rms(ckv, KV_LORA)) * lnkv_ref[...]).astype(BF16)
        gate = u[:, g0:g0 + MLA_WIDTH]
        sg_ref[rows, :] = (gate * jax.nn.sigmoid(gate)).astype(BF16)
        kpe_ref[rows, :] = u[:, g0 + MLA_WIDTH:]


def _a_in(x2, ln, w_pad, ln_q, ln_kv, tm):
    m = x2.shape[0]
    row = lambda n: pl.BlockSpec((tm, n), lambda i: (i, 0))
    full = lambda a: pl.BlockSpec(a.shape, lambda i: (0, 0))
    return pl.pallas_call(
        _a_in_kernel,
        grid=(m // tm,),
        in_specs=[row(D_MODEL), full(ln),
                  pl.BlockSpec(w_pad.shape, lambda i: (0, 0), pipeline_mode=pl.Buffered(1)),
                  full(ln_q), full(ln_kv)],
        out_specs=[pl.BlockSpec((Q_LORA, tm), lambda i: (0, i)), row(KV_LORA), row(MLA_WIDTH), row(V7X_LANES)],
        out_shape=[jax.ShapeDtypeStruct((Q_LORA, m), BF16), jax.ShapeDtypeStruct((m, KV_LORA), BF16),
                   jax.ShapeDtypeStruct((m, MLA_WIDTH), BF16), jax.ShapeDtypeStruct((m, V7X_LANES), F32)],
        name="a_in_proj",
        compiler_params=_params(("arbitrary",)),
    )(x2, ln, w_pad, ln_q, ln_kv)


def _q_up_kernel(cqt_ref, wt_ref, g_ref, cos_ref, sin_ref, q_ref, *, scale):
    tm = cqt_ref.shape[1]
    g = jnp.tile(g_ref[...], (1, tm // V7X_LANES))
    cos_t, sin_t = cos_ref[...], sin_ref[...]
    half = QK_ROPE // 2
    zeros = jnp.zeros((QK_PAD - QK_HEAD, tm), BF16)
    for h in range(MLA_HEADS):
        if h % UP_HEAD_GROUP == 0:
            w_rows = slice(h * QK_HEAD, (h + UP_HEAD_GROUP) * QK_HEAD)
            yt = jnp.dot(wt_ref[w_rows, :], cqt_ref[...], preferred_element_type=F32)
        hl = h % UP_HEAD_GROUP
        blk = yt[hl * QK_HEAD:(hl + 1) * QK_HEAD, :]
        r = lax.rsqrt(jnp.sum(blk * blk, axis=0, keepdims=True) * (1.0 / QK_HEAD) + EPS)
        z = (blk * r) * g
        x1, x2 = z[QK_NOPE:QK_NOPE + half, :], z[QK_NOPE + half:, :]
        q_ref[h, :QK_NOPE, :] = (z[:QK_NOPE, :] * scale).astype(BF16)
        q_ref[h, QK_NOPE:QK_NOPE + half, :] = ((x1 * cos_t - x2 * sin_t) * scale).astype(BF16)
        q_ref[h, QK_NOPE + half:QK_HEAD, :] = ((x2 * cos_t + x1 * sin_t) * scale).astype(BF16)
        q_ref[h, QK_HEAD:, :] = zeros


def _kv_up_kernel(ckv_ref, kpe_ref, wk_ref, wvt_ref, g_ref, cos_ref, s1_ref, s2_ref, k_ref, vt_ref):
    g_n = g_ref[:, :QK_NOPE]
    g_r = g_ref[:, QK_NOPE:]
    kpe = kpe_ref[...]
    ss_pe = jnp.sum(kpe * kpe, axis=-1, keepdims=True)
    cos_t, s1_t, s2_t = cos_ref[...], s1_ref[...], s2_ref[...]
    pad_row = lax.broadcasted_iota(jnp.int32, (VT_ROWS - V_HEAD, ckv_ref.shape[0]), 0)
    ones_rows = jnp.where(pad_row == 0, 1.0, 0.0).astype(BF16)
    kr = _rope(kpe * g_r, cos_t, s1_t, s2_t)
    ckv = ckv_ref[...]
    for h in range(MLA_HEADS):
        if h % UP_HEAD_GROUP == 0:
            grp = slice(h * QK_NOPE, (h + UP_HEAD_GROUP) * QK_NOPE)
            yk = jnp.dot(ckv, wk_ref[:, grp], preferred_element_type=F32)
            vt = lax.dot_general(wvt_ref[grp, :], ckv, (((1,), (1,)), ((), ())),
                                 preferred_element_type=F32)
        hl = slice((h % UP_HEAD_GROUP) * QK_NOPE, (h % UP_HEAD_GROUP + 1) * QK_NOPE)
        kn = yk[:, hl]
        r = lax.rsqrt((jnp.sum(kn * kn, axis=-1, keepdims=True) + ss_pe) * (1.0 / QK_HEAD) + EPS)
        k_ref[h, :, :QK_NOPE] = ((kn * r) * g_n).astype(BF16)
        k_ref[h, :, QK_NOPE:] = (kr * r).astype(BF16)
        vt_ref[h, :V_HEAD, :] = vt[hl, :].astype(BF16)
        vt_ref[h, V_HEAD:, :] = ones_rows


def _seq_spec(tm, n):
    return pl.BlockSpec((None, tm, n), lambda b, i: (b, i, 0))


def _tab_spec(tm):
    return pl.BlockSpec((tm, V7X_LANES), lambda b, i: (i, 0))


def _head_spec(tm, n):
    return pl.BlockSpec((None, MLA_HEADS, tm, n), lambda b, i: (b, 0, i, 0))


def _q_up(cqt, wt, g_rows, tabs_t, batch, seq, tm):
    full = lambda a: pl.BlockSpec(a.shape, lambda bb, i: (0, 0))
    scale = QK_HEAD ** -0.5 * LOG2E
    nblk = seq // tm
    tab = pl.BlockSpec((QK_ROPE // 2, tm), lambda bb, i: (0, i))
    return pl.pallas_call(
        functools.partial(_q_up_kernel, scale=scale),
        grid=(batch, nblk),
        in_specs=[pl.BlockSpec((Q_LORA, tm), lambda bb, i: (0, bb * nblk + i)), full(wt), full(g_rows), tab, tab],
        out_specs=pl.BlockSpec((None, MLA_HEADS, QK_PAD, tm), lambda bb, i: (bb, 0, 0, i)),
        out_shape=jax.ShapeDtypeStruct((batch, MLA_HEADS, QK_PAD, seq), BF16),
        name="a_q_up",
        compiler_params=_params(("arbitrary", "arbitrary")),
    )(cqt, wt, g_rows, *tabs_t)


def _kv_up(ckv3, kpe3, wk, wvt, g_pad, tabs, tm):
    b, s, _ = ckv3.shape
    full = lambda a: pl.BlockSpec(a.shape, lambda bb, i: (0, 0))
    return pl.pallas_call(
        _kv_up_kernel,
        grid=(b, s // tm),
        in_specs=[_seq_spec(tm, KV_LORA), _seq_spec(tm, V7X_LANES), full(wk), full(wvt), full(g_pad),
                  _tab_spec(tm), _tab_spec(tm), _tab_spec(tm)],
        out_specs=[_head_spec(tm, QK_PAD),
                   pl.BlockSpec((None, MLA_HEADS, None, VT_ROWS, tm), lambda bb, i: (bb, 0, i, 0, 0))],
        out_shape=[jax.ShapeDtypeStruct((b, MLA_HEADS, s, QK_PAD), BF16),
                   jax.ShapeDtypeStruct((b, MLA_HEADS, s // tm, VT_ROWS, tm), BF16)],
        name="a_kv_up",
        compiler_params=_params(("arbitrary", "arbitrary")),
    )(ckv3, kpe3, wk, wvt, g_pad, *tabs)


def _mla_attn_kernel(q_ref, k_ref, vt_ref, o_ref, sig_sc, viol_sc, acc_sc, *, tq, tk):
    qi = pl.program_id(2)
    qt = q_ref[...]
    n_chunks = 2 * qi + 2

    def qk(kc):
        k = k_ref[pl.ds(pl.multiple_of(kc * tk, tk), tk), :]
        return jnp.dot(k, qt, preferred_element_type=F32)

    def causal(s, key_offset):
        key = lax.broadcasted_iota(jnp.int32, (tk, tq), 0) + key_offset
        qry = lax.broadcasted_iota(jnp.int32, (tk, tq), 1)
        return jnp.where(key <= qry, s, NEG_BIG)

    def finish():
        o_ref[...] = (acc_sc[:V_HEAD, :] / acc_sc[V_HEAD:V_HEAD + 1, :]).T.astype(BF16)

    def lagged_chunk(kc, key_offset):
        col0 = 0 if key_offset is None else key_offset
        cs = slice(col0, tq)
        k = k_ref[pl.ds(pl.multiple_of(kc * tk, tk), tk), :]
        s = jnp.dot(k, qt[:, cs], preferred_element_type=F32)
        if key_offset is not None:
            key = lax.broadcasted_iota(jnp.int32, s.shape, 0)
            qry = lax.broadcasted_iota(jnp.int32, s.shape, 1)
            s = jnp.where(key <= qry, s, NEG_BIG)
        shift = sig_sc[:, cs]
        mx = jnp.max(s, axis=0, keepdims=True)
        p = jnp.exp2(s - shift).astype(BF16)
        new_shift = jnp.maximum(shift, mx)
        pv = jnp.dot(vt_ref[kc], p, preferred_element_type=F32)
        acc_sc[:, cs] = (acc_sc[:, cs] + pv) * jnp.exp2(shift - new_shift)
        sig_sc[:, cs] = new_shift
        viol_sc[:, cs] = jnp.maximum(viol_sc[:, cs], mx - shift)

    s0 = jnp.dot(k_ref[:FIRST_SHIFT_KEYS, :], qt, preferred_element_type=F32)
    key0 = lax.broadcasted_iota(jnp.int32, s0.shape, 0) - qi * tq
    s0 = jnp.where(key0 <= lax.broadcasted_iota(jnp.int32, s0.shape, 1), s0, NEG_BIG)
    sig_sc[...] = jnp.max(s0, axis=0, keepdims=True)
    viol_sc[...] = jnp.zeros(viol_sc.shape, F32)
    acc_sc[...] = jnp.zeros(acc_sc.shape, F32)

    def octet(io, carry):
        for u in range(MLA_UNROLL):
            lagged_chunk(MLA_UNROLL * io + u, None)
        return carry

    per_iter = MLA_UNROLL // 2
    n_oct = qi // per_iter
    lax.fori_loop(0, n_oct, octet, 0)
    for rem in range(per_iter):
        @pl.when(qi % per_iter == rem)
        def _(rem=rem):
            for u in range(2 * rem):
                lagged_chunk(MLA_UNROLL * n_oct + u, None)
            lagged_chunk(2 * qi, 0)
            lagged_chunk(2 * qi + 1, tk)
    finish()

    @pl.when(jnp.max(viol_sc[...]) > LAG_LIMIT)
    def _():
        sig_sc[...] = jnp.full(sig_sc.shape, NEG_BIG, F32)
        acc_sc[...] = jnp.zeros(acc_sc.shape, F32)

        def exact_chunk(kc, carry):
            s = causal(qk(kc), kc * tk - qi * tq)
            m_old = sig_sc[...]
            m_new = jnp.maximum(m_old, jnp.max(s, axis=0, keepdims=True))
            p = jnp.exp2(s - m_new).astype(BF16)
            acc_sc[...] = jnp.exp2(m_old - m_new) * acc_sc[...] + jnp.dot(vt_ref[kc], p, preferred_element_type=F32)
            sig_sc[...] = m_new
            return carry

        lax.fori_loop(0, n_chunks, exact_chunk, 0)
        finish()


def _mla_attention(qt, k, vt, tk):
    b, h, s, _ = k.shape
    tq = 2 * tk
    assert vt.shape[-1] == tk and s % tq == 0
    return pl.pallas_call(
        functools.partial(_mla_attn_kernel, tq=tq, tk=tk),
        grid=(b, h, s // tq),
        in_specs=[pl.BlockSpec((None, None, QK_PAD, tq), lambda bb, hh, i: (bb, hh, 0, i)),
                  pl.BlockSpec((None, None, s, QK_PAD), lambda bb, hh, i: (bb, hh, 0, 0)),
                  pl.BlockSpec((None, None, s // tk, VT_ROWS, tk), lambda bb, hh, i: (bb, hh, 0, 0, 0))],
        out_specs=pl.BlockSpec((None, tq, V_HEAD), lambda bb, hh, i: (bb, i, hh)),
        out_shape=jax.ShapeDtypeStruct((b, s, MLA_WIDTH), BF16),
        scratch_shapes=[pltpu.VMEM((1, tq), F32), pltpu.VMEM((1, tq), F32), pltpu.VMEM((VT_ROWS, tq), F32)],
        name="a_mla_attention",
        compiler_params=_params(("arbitrary", "arbitrary", "arbitrary")),
    )(qt, k, vt)


def _a_out_kernel(o_ref, sg_ref, x_ref, w_ref, lnkv_ref, lnb_ref, h_ref, xkv_ref, xb_ref):
    tm = o_ref.shape[0]
    for c in range(tm // OUT_ROW_CHUNK):
        rows = slice(c * OUT_ROW_CHUNK, (c + 1) * OUT_ROW_CHUNK)
        a = (o_ref[rows, :] * sg_ref[rows, :]).astype(BF16)
        h = x_ref[rows, :] + jnp.dot(a, w_ref[...], preferred_element_type=F32)
        h_ref[rows, :] = h
        hn = h * _inv_rms(h, D_MODEL)
        xkv_ref[rows, :] = (hn * lnkv_ref[...]).astype(BF16)
        xb_ref[rows, :] = (hn * lnb_ref[...]).astype(BF16)


def _a_out(o2, sg, x2, w, ln_kv, ln_b, tm):
    m = x2.shape[0]
    row = pl.BlockSpec((tm, D_MODEL), lambda i: (i, 0))
    full = lambda a: pl.BlockSpec(a.shape, lambda i: (0, 0))
    return pl.pallas_call(
        _a_out_kernel,
        grid=(m // tm,),
        in_specs=[row, row, row, pl.BlockSpec(w.shape, lambda i: (0, 0), pipeline_mode=pl.Buffered(1)),
                  full(ln_kv), full(ln_b)],
        out_specs=[row, row, row],
        out_shape=[jax.ShapeDtypeStruct((m, D_MODEL), F32), jax.ShapeDtypeStruct((m, D_MODEL), BF16),
                   jax.ShapeDtypeStruct((m, D_MODEL), BF16)],
        name="a_out_proj",
        compiler_params=_params(("arbitrary",)),
    )(o2, sg, x2, w, ln_kv, ln_b)


def _branch_proj_kernel(x_ref, w_ref, g_ref, o_ref, y_sc, y2_sc, *, mode, dilation, scale):
    tm, tn = o_ref.shape
    rows = PROJ_ROW_CHUNK // dilation
    for c in range(tm // PROJ_ROW_CHUNK):
        y = jnp.dot(x_ref[c * PROJ_ROW_CHUNK:(c + 1) * PROJ_ROW_CHUNK, :], w_ref[...], preferred_element_type=F32)
        for hc in range(tn // DIL_HEAD_DIM):
            sl = slice(hc * DIL_HEAD_DIM, (hc + 1) * DIL_HEAD_DIM)
            yh = y[:, sl]
            if mode == "norm":
                yh = (yh * _inv_rms(yh, DIL_HEAD_DIM)) * g_ref[:, sl]
                if scale != 1.0:
                    yh = yh * scale
            elif mode == "silu":
                yh = yh * jax.nn.sigmoid(yh)
            if dilation == 1:
                o_ref[c * PROJ_ROW_CHUNK:(c + 1) * PROJ_ROW_CHUNK, sl] = yh.astype(BF16)
            else:
                base = c * PROJ_ROW_CHUNK
                y_sc[hc, base:base + PROJ_ROW_CHUNK, :] = yh
                if dilation == STRIDE_STEP * STRIDE_STEP:
                    quarter = PROJ_ROW_CHUNK // STRIDE_STEP
                    for a in range(STRIDE_STEP):
                        y2_sc[hc, base + a * quarter:base + (a + 1) * quarter, :] = (
                            y_sc[hc, pl.ds(base + a, quarter, stride=STRIDE_STEP), :])
                    for r in range(dilation):
                        a, b = r % STRIDE_STEP, r // STRIDE_STEP
                        dst = r * (tm // dilation) + c * rows
                        o_ref[dst:dst + rows, sl] = (
                            y2_sc[hc, pl.ds(base + a * quarter + b, rows, stride=STRIDE_STEP), :].astype(BF16))
                else:
                    for r in range(dilation):
                        dst = r * (tm // dilation) + c * rows
                        o_ref[dst:dst + rows, sl] = y_sc[hc, pl.ds(base + r, rows, stride=dilation), :].astype(BF16)


def _branch_proj(xn, w, g_cols, col0, mode, dilation, scale, name):
    m, kdim = xn.shape
    tm, tn = DIL_TILE, PROJ_COL_TILE
    blk0 = col0 // tn
    return pl.pallas_call(
        functools.partial(_branch_proj_kernel, mode=mode, dilation=dilation, scale=scale),
        grid=(m // tm, DIL_WIDTH // tn),
        in_specs=[pl.BlockSpec((tm, kdim), lambda i, j: (i, 0)),
                  pl.BlockSpec((kdim, tn), lambda i, j: (0, blk0 + j)),
                  pl.BlockSpec((1, tn), lambda i, j: (0, blk0 + j))],
        out_specs=pl.BlockSpec((tm, tn), lambda i, j: (i, j)),
        out_shape=jax.ShapeDtypeStruct((m, DIL_WIDTH), BF16),
        scratch_shapes=[pltpu.VMEM((tn // DIL_HEAD_DIM, tm, DIL_HEAD_DIM), F32)] * 2,
        name=name,
        compiler_params=_params(("arbitrary", "arbitrary")),
    )(xn, w, g_cols)


def _dilated_attn_kernel(*refs):
    qs, kps, kcs, vps, vcs = (refs[i * N_GROUPS:(i + 1) * N_GROUPS] for i in range(5))
    sg_ref, out_ref, o_sc, lse_sc = refs[5 * N_GROUPS:]
    t = pl.program_id(1)
    c = pl.program_id(2)
    row = lax.broadcasted_iota(jnp.int32, (BAND_BLK, 2 * BAND_BLK), 0)
    col = lax.broadcasted_iota(jnp.int32, (BAND_BLK, 2 * BAND_BLK), 1)
    rel = row + BAND_BLK - col
    band = (rel >= 0) & (rel <= LOOKBACK)
    first_band = band & (col >= jnp.where(t > 0, 0, BAND_BLK))
    relf = rel.astype(F32)
    assert DIL_GROUPS[0][1] == 1
    for g in (1, 2, 0):
        dil = DIL_GROUPS[g][1]
        nblk = DIL_TILE // dil // BAND_BLK
        q_ref, kp_ref, kc_ref, vp_ref, vc_ref = qs[g], kps[g], kcs[g], vps[g], vcs[g]
        for hh in range(DIL_HEAD_CHUNK):
            hs = slice(hh * DIL_HEAD_DIM, (hh + 1) * DIL_HEAD_DIM)
            head = (c * DIL_HEAD_CHUNK + hh + 1).astype(F32)
            slope = jnp.exp2(jnp.full((1, 2 * BAND_BLK), -ALIBI_MAX / DIL_HEADS, F32) * head)
            bias = (slope * (-float(dil) * LOG2E)) * relf
            bias_band = jnp.where(band, bias, NEG_BIG)
            bias_first = jnp.where(first_band, bias, NEG_BIG)
            for r in range(dil):
                for a in range(nblk):
                    q = q_ref[r, a, :, hs]
                    k_lo = kp_ref[r, :, hs] if a == 0 else kc_ref[r, a - 1, :, hs]
                    v_lo = vp_ref[r, :, hs] if a == 0 else vc_ref[r, a - 1, :, hs]
                    k = jnp.concatenate([k_lo, kc_ref[r, a, :, hs]], axis=0)
                    v = jnp.concatenate([v_lo, vc_ref[r, a, :, hs]], axis=0)
                    s = lax.dot_general(q, k, (((1,), (1,)), ((), ())), preferred_element_type=F32)
                    s = s + (bias_first if a == 0 else bias_band)
                    m = jnp.max(s, axis=-1, keepdims=True)
                    p = jnp.exp2(s - m)
                    den = jnp.sum(p, axis=-1, keepdims=True)
                    o = jnp.dot(p.astype(BF16), v, preferred_element_type=F32) / den
                    lse = jnp.broadcast_to(m + jnp.log2(den), (BAND_BLK, V7X_LANES))
                    if dil > 1:
                        tok = pl.ds(a * BAND_BLK * dil + r, BAND_BLK, stride=dil)
                        o_sc[g - 1, hh, tok, :] = o
                        lse_sc[g - 1, hh, tok, :] = lse
                    else:
                        tok = slice(a * BAND_BLK, (a + 1) * BAND_BLK)
                        l1, l2 = lse_sc[0, hh, tok, :], lse_sc[1, hh, tok, :]
                        mx = jnp.maximum(jnp.maximum(lse, l1), l2)
                        e0, e1, e2 = jnp.exp2(lse - mx), jnp.exp2(l1 - mx), jnp.exp2(l2 - mx)
                        merged = (e0 * o + e1 * o_sc[0, hh, tok, :] + e2 * o_sc[1, hh, tok, :]) / (e0 + e1 + e2)
                        out_ref[tok, hs] = (merged * sg_ref[tok, hs].astype(F32)).astype(BF16)


def _dilated_attention(qs, ks, vs, gate, batch, seq):
    nt = seq // DIL_TILE
    ins, specs = [], []

    def add(arrs, prev):
        for arr, (_, dil) in zip(arrs, DIL_GROUPS):
            nblk = DIL_TILE // dil // BAND_BLK
            ins.append(arr.reshape(batch, nt, dil, nblk, BAND_BLK, DIL_WIDTH))
            if prev:
                specs.append(pl.BlockSpec((None, None, dil, None, BAND_BLK, DIL_COLS),
                                          lambda b, t, c, nblk=nblk: (b, jnp.maximum(t - 1, 0), 0, nblk - 1, 0, c)))
            else:
                specs.append(pl.BlockSpec((None, None, dil, nblk, BAND_BLK, DIL_COLS),
                                          lambda b, t, c: (b, t, 0, 0, 0, c)))

    add(qs, False)
    add(ks, True)
    add(ks, False)
    add(vs, True)
    add(vs, False)
    ins.append(gate.reshape(batch, nt, DIL_TILE, DIL_WIDTH))
    specs.append(pl.BlockSpec((None, None, DIL_TILE, DIL_COLS), lambda b, t, c: (b, t, 0, c)))
    out = pl.pallas_call(
        _dilated_attn_kernel,
        grid=(batch, nt, DIL_HEADS // DIL_HEAD_CHUNK),
        in_specs=specs,
        out_specs=pl.BlockSpec((None, None, DIL_TILE, DIL_COLS), lambda b, t, c: (b, t, 0, c)),
        out_shape=jax.ShapeDtypeStruct((batch, nt, DIL_TILE, DIL_WIDTH), BF16),
        scratch_shapes=[pltpu.VMEM((N_GROUPS - 1, DIL_HEAD_CHUNK, DIL_TILE, DIL_HEAD_DIM), F32),
                        pltpu.VMEM((N_GROUPS - 1, DIL_HEAD_CHUNK, DIL_TILE, V7X_LANES), F32)],
        name="b_dilated_attention",
        compiler_params=_params(("arbitrary", "arbitrary", "arbitrary")),
    )(*ins)
    return out.reshape(batch * seq, DIL_WIDTH)


def _b_out_kernel(a_ref, h_ref, w_ref, out_ref):
    for c in range(out_ref.shape[0] // OUT_ROW_CHUNK):
        rows = slice(c * OUT_ROW_CHUNK, (c + 1) * OUT_ROW_CHUNK)
        out_ref[rows, :] = h_ref[rows, :] + jnp.dot(a_ref[rows, :], w_ref[...], preferred_element_type=F32)


def _b_out(a, h1, w, tm):
    m = h1.shape[0]
    row = pl.BlockSpec((tm, D_MODEL), lambda i: (i, 0))
    return pl.pallas_call(
        _b_out_kernel,
        grid=(m // tm,),
        in_specs=[row, row, pl.BlockSpec(w.shape, lambda i: (0, 0), pipeline_mode=pl.Buffered(1))],
        out_specs=row,
        out_shape=jax.ShapeDtypeStruct((m, D_MODEL), F32),
        name="b_out_proj",
        compiler_params=_params(("arbitrary",)),
    )(a, h1, w)


def _tile(n, want):
    t = min(n, want)
    assert n % t == 0, (n, t)
    return t


def kernel(x, a_ln, a_w_in, a_ln_q, a_w_q_up, a_ln_kv, a_w_kv_up, a_q_norm, a_k_norm, a_w_o,
           kv_ln, kv_w, kv_k_norm, b_ln, b_w_in, b_q_norm, b_w_o):
    batch, seq, d = x.shape
    assert d == D_MODEL and a_ln.shape[0] == 1 and b_ln.shape[0] == 1
    assert seq % DIL_TILE == 0
    m = batch * seq
    x2 = x.reshape(m, d)

    lat = Q_LORA + KV_LORA
    w_in = a_w_in[0]
    w_in_pad = jnp.concatenate(
        [w_in[:, :lat], w_in[:, lat + QK_ROPE:], w_in[:, lat:lat + QK_ROPE],
         jnp.zeros((d, V7X_LANES - QK_ROPE), F32)], axis=1).astype(BF16)
    assert w_in_pad.shape == (d, A_IN_PAD)
    w_q_t = a_w_q_up[0].T.astype(BF16)
    w_kv = a_w_kv_up[0].reshape(KV_LORA, MLA_HEADS, QK_NOPE + V_HEAD)
    w_k = w_kv[:, :, :QK_NOPE].reshape(KV_LORA, MLA_HEADS * QK_NOPE).astype(BF16)
    w_vt = w_kv[:, :, QK_NOPE:].reshape(KV_LORA, MLA_HEADS * V_HEAD).T.astype(BF16)
    gq_rows = jnp.broadcast_to(a_q_norm[0][:, None], (QK_HEAD, V7X_LANES))
    gk_pad = jnp.pad(a_k_norm[0], (0, QK_PAD - QK_HEAD))[None, :]
    w_o_a = a_w_o[0].astype(BF16)
    w_kvs = kv_w.astype(BF16)
    w_b_in = b_w_in[0].astype(BF16)
    w_o_b = b_w_o[0].astype(BF16)
    k_cols = N_GROUPS * DIL_WIDTH
    gk_cols = jnp.concatenate([jnp.tile(kv_k_norm, (1, DIL_HEADS)).reshape(1, k_cols),
                               jnp.ones((1, k_cols), F32)], axis=1)
    gq_cols = jnp.concatenate([jnp.tile(b_q_norm[0], (1, DIL_HEADS)).reshape(1, k_cols),
                               jnp.ones((1, DIL_WIDTH), F32)], axis=1)

    tabs, tabs_t = _rope_tables(seq)
    cqt, ckv, sg_a, kpe = _a_in(x2, a_ln, w_in_pad, a_ln_q, a_ln_kv, _tile(m, A_IN_ROW_TILE))
    blk = _tile(seq, MLA_KEY_CHUNK)
    q = _q_up(cqt, w_q_t, gq_rows, tabs_t, batch, seq, blk)
    k, vt = _kv_up(ckv.reshape(batch, seq, KV_LORA), kpe.reshape(batch, seq, V7X_LANES), w_k, w_vt, gk_pad, tabs, blk)
    o_a = _mla_attention(q, k, vt, blk)
    h1, xn_kv, xn_b = _a_out(o_a.reshape(m, MLA_WIDTH), sg_a, x2, w_o_a, kv_ln[None, :], b_ln,
                             _tile(m, 2 * OUT_ROW_CHUNK))

    q_scale = DIL_HEAD_DIM ** -0.5 * LOG2E
    qs, ks, vs = [], [], []
    for g, (_, dil) in enumerate(DIL_GROUPS):
        ks.append(_branch_proj(xn_kv, w_kvs, gk_cols, g * DIL_WIDTH, "norm", dil, 1.0, f"b_k_proj_d{dil}"))
        vs.append(_branch_proj(xn_kv, w_kvs, gk_cols, k_cols + g * DIL_WIDTH, "copy", dil, 1.0, f"b_v_proj_d{dil}"))
        qs.append(_branch_proj(xn_b, w_b_in, gq_cols, g * DIL_WIDTH, "norm", dil, q_scale, f"b_q_proj_d{dil}"))
    sg_b = _branch_proj(xn_b, w_b_in, gq_cols, k_cols, "silu", 1, 1.0, "b_gate_proj")
    a_b = _dilated_attention(qs, ks, vs, sg_b, batch, seq)
    out = _b_out(a_b, h1, w_o_b, _tile(m, 2 * OUT_ROW_CHUNK))
    return out.reshape(batch, seq, d)
```

```python
import functools
import math

import jax
import jax.numpy as jnp
from jax import lax
from jax.experimental import pallas as pl
from jax.experimental.pallas import tpu as pltpu

F32 = jnp.float32
BF16 = jnp.bfloat16

V7X_LANES = 128
V7X_VMEM_BYTES = 64 * 1024 * 1024
VMEM_LIMIT_BYTES = V7X_VMEM_BYTES * 7 // 8

D_MODEL = 2048
MLA_HEADS = 16
Q_LORA = 512
KV_LORA = 512
QK_NOPE = 128
QK_ROPE = 64
QK_HEAD = QK_NOPE + QK_ROPE
V_HEAD = 128
MLA_WIDTH = MLA_HEADS * V_HEAD
ROPE_THETA = 10000.0
DIL_GROUPS = ((128, 1), (512, 4), (2048, 16))
N_GROUPS = 3
DIL_HEADS = 16
DIL_HEAD_DIM = 128
DIL_WIDTH = DIL_HEADS * DIL_HEAD_DIM
LOOKBACK = 128
BAND_BLK = 128
ALIBI_MAX = 8.0
EPS = 1e-6
NEG_BIG = -1e30
LOG2E = math.log2(math.e)

QK_PAD = 2 * V7X_LANES
A_IN_PAD = Q_LORA + KV_LORA + MLA_WIDTH + V7X_LANES

DIL_TILE = BAND_BLK * DIL_GROUPS[-1][1]
DIL_HEAD_CHUNK = 2
DIL_COLS = DIL_HEAD_CHUNK * DIL_HEAD_DIM
V7X_BF16_SUBLANES = 16
A_IN_ROW_TILE = 512
A_IN_ROW_CHUNK = 256
UP_HEAD_GROUP = 2
OUT_ROW_CHUNK = 256
MLA_KEY_CHUNK = 512
MLA_UNROLL = 8
VT_ROWS = V_HEAD + V7X_BF16_SUBLANES
FIRST_SHIFT_KEYS = V7X_BF16_SUBLANES
LAG_LIMIT = 64.0
STRIDE_STEP = 4
PROJ_ROW_CHUNK = 256
PROJ_COL_TILE = 1024


def _params(semantics):
    return pltpu.CompilerParams(dimension_semantics=semantics, vmem_limit_bytes=VMEM_LIMIT_BYTES)


def _inv_rms(x, n):
    return lax.rsqrt(jnp.sum(x * x, axis=-1, keepdims=True) * (1.0 / n) + EPS)


def _rope(rv, cos_t, s1_t, s2_t):
    return rv * cos_t + pltpu.roll(rv, 96, 1) * s1_t + pltpu.roll(rv, 32, 1) * s2_t


def _rope_table_kernel(invt_ref, cos_ref, s1_ref, s2_ref, cost_ref, sint_ref):
    ts = cos_ref.shape[0]
    half = QK_ROPE // 2
    pos = lax.broadcasted_iota(jnp.int32, (half, ts), 1) + pl.program_id(0) * ts
    ang_t = pos.astype(F32) * jnp.tile(invt_ref[...], (1, ts // V7X_LANES))
    c_t = jnp.cos(ang_t)
    s_t = jnp.sin(ang_t)
    cost_ref[...] = c_t
    sint_ref[...] = s_t
    c = c_t.T
    s = s_t.T
    z32 = jnp.zeros((ts, half), F32)
    z64 = jnp.zeros((ts, V7X_LANES - QK_ROPE), F32)
    cos_ref[...] = jnp.concatenate([c, c, z64], axis=1)
    s1_ref[...] = jnp.concatenate([-s, z32, z64], axis=1)
    s2_ref[...] = jnp.concatenate([z32, s, z64], axis=1)


def _rope_tables(seq):
    half = QK_ROPE // 2
    inv = ROPE_THETA ** (-jnp.arange(half, dtype=F32) / half)
    inv_rows = jnp.broadcast_to(inv[:, None], (half, V7X_LANES))
    ts = min(seq, 1024)
    tab = jax.ShapeDtypeStruct((seq, V7X_LANES), F32)
    tab_t = jax.ShapeDtypeStruct((half, seq), F32)
    spec = pl.BlockSpec((ts, V7X_LANES), lambda i: (i, 0))
    spec_t = pl.BlockSpec((half, ts), lambda i: (0, i))
    outs = pl.pallas_call(
        _rope_table_kernel,
        grid=(seq // ts,),
        in_specs=[pl.BlockSpec((half, V7X_LANES), lambda i: (0, 0))],
        out_specs=[spec, spec, spec, spec_t, spec_t],
        out_shape=[tab, tab, tab, tab_t, tab_t],
        name="rope_tables",
        compiler_params=_params(("arbitrary",)),
    )(inv_rows)
    return outs[:3], outs[3:]


def _a_in_kernel(x_ref, ln_ref, w_ref, lnq_ref, lnkv_ref, cq_ref, ckv_ref, sg_ref, kpe_ref):
    g0 = Q_LORA + KV_LORA
    for c in range(x_ref.shape[0] // A_IN_ROW_CHUNK):
        rows = slice(c * A_IN_ROW_CHUNK, (c + 1) * A_IN_ROW_CHUNK)
        x = x_ref[rows, :]
        xn = ((x * _inv_rms(x, D_MODEL)) * ln_ref[...]).astype(BF16)
        u = jnp.dot(xn, w_ref[...], preferred_element_type=F32)
        cq = u[:, :Q_LORA]
        cq_ref[:, rows] = ((cq * _inv_rms(cq, Q_LORA)) * lnq_ref[...]).T.astype(BF16)
        ckv = u[:, Q_LORA:g0]
        ckv_ref[rows, :] = ((ckv * _inv_rms(ckv, KV_LORA)) * lnkv_ref[...]).astype(BF16)
        gate = u[:, g0:g0 + MLA_WIDTH]
        sg_ref[rows, :] = (gate * jax.nn.sigmoid(gate)).astype(BF16)
        kpe_ref[rows, :] = u[:, g0 + MLA_WIDTH:]


def _a_in(x2, ln, w_pad, ln_q, ln_kv, tm):
    m = x2.shape[0]
    row = lambda n: pl.BlockSpec((tm, n), lambda i: (i, 0))
    full = lambda a: pl.BlockSpec(a.shape, lambda i: (0, 0))
    return pl.pallas_call(
        _a_in_kernel,
        grid=(m // tm,),
        in_specs=[row(D_MODEL), full(ln),
                  pl.BlockSpec(w_pad.shape, lambda i: (0, 0), pipeline_mode=pl.Buffered(1)),
                  full(ln_q), full(ln_kv)],
        out_specs=[pl.BlockSpec((Q_LORA, tm), lambda i: (0, i)), row(KV_LORA), row(MLA_WIDTH), row(V7X_LANES)],
        out_shape=[jax.ShapeDtypeStruct((Q_LORA, m), BF16), jax.ShapeDtypeStruct((m, KV_LORA), BF16),
                   jax.ShapeDtypeStruct((m, MLA_WIDTH), BF16), jax.ShapeDtypeStruct((m, V7X_LANES), F32)],
        name="a_in_proj",
        compiler_params=_params(("arbitrary",)),
    )(x2, ln, w_pad, ln_q, ln_kv)


def _q_up_kernel(cqt_ref, wt_ref, g_ref, cos_ref, sin_ref, q_ref, *, scale):
    tm = cqt_ref.shape[1]
    g = jnp.tile(g_ref[...], (1, tm // V7X_LANES))
    cos_t, sin_t = cos_ref[...], sin_ref[...]
    half = QK_ROPE // 2
    zeros = jnp.zeros((QK_PAD - QK_HEAD, tm), BF16)
    for h in range(MLA_HEADS):
        if h % UP_HEAD_GROUP == 0:
            w_rows = slice(h * QK_HEAD, (h + UP_HEAD_GROUP) * QK_HEAD)
            yt = jnp.dot(wt_ref[w_rows, :], cqt_ref[...], preferred_element_type=F32)
        hl = h % UP_HEAD_GROUP
        blk = yt[hl * QK_HEAD:(hl + 1) * QK_HEAD, :]
        r = lax.rsqrt(jnp.sum(blk * blk, axis=0, keepdims=True) * (1.0 / QK_HEAD) + EPS)
        z = (blk * r) * g
        x1, x2 = z[QK_NOPE:QK_NOPE + half, :], z[QK_NOPE + half:, :]
        q_ref[h, :QK_NOPE, :] = (z[:QK_NOPE, :] * scale).astype(BF16)
        q_ref[h, QK_NOPE:QK_NOPE + half, :] = ((x1 * cos_t - x2 * sin_t) * scale).astype(BF16)
        q_ref[h, QK_NOPE + half:QK_HEAD, :] = ((x2 * cos_t + x1 * sin_t) * scale).astype(BF16)
        q_ref[h, QK_HEAD:, :] = zeros


def _kv_up_kernel(ckv_ref, kpe_ref, wk_ref, wvt_ref, g_ref, cos_ref, s1_ref, s2_ref, k_ref, vt_ref):
    g_n = g_ref[:, :QK_NOPE]
    g_r = g_ref[:, QK_NOPE:]
    kpe = kpe_ref[...]
    ss_pe = jnp.sum(kpe * kpe, axis=-1, keepdims=True)
    cos_t, s1_t, s2_t = cos_ref[...], s1_ref[...], s2_ref[...]
    pad_row = lax.broadcasted_iota(jnp.int32, (VT_ROWS - V_HEAD, ckv_ref.shape[0]), 0)
    ones_rows = jnp.where(pad_row == 0, 1.0, 0.0).astype(BF16)
    kr = _rope(kpe * g_r, cos_t, s1_t, s2_t)
    ckv = ckv_ref[...]
    for h in range(MLA_HEADS):
        if h % UP_HEAD_GROUP == 0:
            grp = slice(h * QK_NOPE, (h + UP_HEAD_GROUP) * QK_NOPE)
            yk = jnp.dot(ckv, wk_ref[:, grp], preferred_element_type=F32)
            vt = lax.dot_general(wvt_ref[grp, :], ckv, (((1,), (1,)), ((), ())),
                                 preferred_element_type=F32)
        hl = slice((h % UP_HEAD_GROUP) * QK_NOPE, (h % UP_HEAD_GROUP + 1) * QK_NOPE)
        kn = yk[:, hl]
        r = lax.rsqrt((jnp.sum(kn * kn, axis=-1, keepdims=True) + ss_pe) * (1.0 / QK_HEAD) + EPS)
        k_ref[h, :, :QK_NOPE] = ((kn * r) * g_n).astype(BF16)
        k_ref[h, :, QK_NOPE:] = (kr * r).astype(BF16)
        vt_ref[h, :V_HEAD, :] = vt[hl, :].astype(BF16)
        vt_ref[h, V_HEAD:, :] = ones_rows


def _seq_spec(tm, n):
    return pl.BlockSpec((None, tm, n), lambda b, i: (b, i, 0))


def _tab_spec(tm):
    return pl.BlockSpec((tm, V7X_LANES), lambda b, i: (i, 0))


def _head_spec(tm, n):
    return pl.BlockSpec((None, MLA_HEADS, tm, n), lambda b, i: (b, 0, i, 0))


def _q_up(cqt, wt, g_rows, tabs_t, batch, seq, tm):
    full = lambda a: pl.BlockSpec(a.shape, lambda bb, i: (0, 0))
    scale = QK_HEAD ** -0.5 * LOG2E
    nblk = seq // tm
    tab = pl.BlockSpec((QK_ROPE // 2, tm), lambda bb, i: (0, i))
    return pl.pallas_call(
        functools.partial(_q_up_kernel, scale=scale),
        grid=(batch, nblk),
        in_specs=[pl.BlockSpec((Q_LORA, tm), lambda bb, i: (0, bb * nblk + i)), full(wt), full(g_rows), tab, tab],
        out_specs=pl.BlockSpec((None, MLA_HEADS, QK_PAD, tm), lambda bb, i: (bb, 0, 0, i)),
        out_shape=jax.ShapeDtypeStruct((batch, MLA_HEADS, QK_PAD, seq), BF16),
        name="a_q_up",
        compiler_params=_params(("arbitrary", "arbitrary")),
    )(cqt, wt, g_rows, *tabs_t)


def _kv_up(ckv3, kpe3, wk, wvt, g_pad, tabs, tm):
    b, s, _ = ckv3.shape
    full = lambda a: pl.BlockSpec(a.shape, lambda bb, i: (0, 0))
    return pl.pallas_call(
        _kv_up_kernel,
        grid=(b, s // tm),
        in_specs=[_seq_spec(tm, KV_LORA), _seq_spec(tm, V7X_LANES), full(wk), full(wvt), full(g_pad),
                  _tab_spec(tm), _tab_spec(tm), _tab_spec(tm)],
        out_specs=[_head_spec(tm, QK_PAD),
                   pl.BlockSpec((None, MLA_HEADS, None, VT_ROWS, tm), lambda bb, i: (bb, 0, i, 0, 0))],
        out_shape=[jax.ShapeDtypeStruct((b, MLA_HEADS, s, QK_PAD), BF16),
                   jax.ShapeDtypeStruct((b, MLA_HEADS, s // tm, VT_ROWS, tm), BF16)],
        name="a_kv_up",
        compiler_params=_params(("arbitrary", "arbitrary")),
    )(ckv3, kpe3, wk, wvt, g_pad, *tabs)


def _mla_attn_kernel(q_ref, k_ref, vt_ref, o_ref, sig_sc, viol_sc, acc_sc, *, tq, tk):
    qi = pl.program_id(2)
    qt = q_ref[...]
    n_chunks = 2 * qi + 2

    def qk(kc):
        k = k_ref[pl.ds(pl.multiple_of(kc * tk, tk), tk), :]
        return jnp.dot(k, qt, preferred_element_type=F32)

    def causal(s, key_offset):
        key = lax.broadcasted_iota(jnp.int32, (tk, tq), 0) + key_offset
        qry = lax.broadcasted_iota(jnp.int32, (tk, tq), 1)
        return jnp.where(key <= qry, s, NEG_BIG)

    def finish():
        o_ref[...] = (acc_sc[:V_HEAD, :] / acc_sc[V_HEAD:V_HEAD + 1, :]).T.astype(BF16)

    def lagged_chunk(kc, key_offset):
        col0 = 0 if key_offset is None else key_offset
        cs = slice(col0, tq)
        k = k_ref[pl.ds(pl.multiple_of(kc * tk, tk), tk), :]
        s = jnp.dot(k, qt[:, cs], preferred_element_type=F32)
        if key_offset is not None:
            key = lax.broadcasted_iota(jnp.int32, s.shape, 0)
            qry = lax.broadcasted_iota(jnp.int32, s.shape, 1)
            s = jnp.where(key <= qry, s, NEG_BIG)
        shift = sig_sc[:, cs]
        mx = jnp.max(s, axis=0, keepdims=True)
        p = jnp.exp2(s - shift).astype(BF16)
        new_shift = jnp.maximum(shift, mx)
        pv = jnp.dot(vt_ref[kc], p, preferred_element_type=F32)
        acc_sc[:, cs] = (acc_sc[:, cs] + pv) * jnp.exp2(shift - new_shift)
        sig_sc[:, cs] = new_shift
        viol_sc[:, cs] = jnp.maximum(viol_sc[:, cs], mx - shift)

    s0 = jnp.dot(k_ref[:FIRST_SHIFT_KEYS, :], qt, preferred_element_type=F32)
    key0 = lax.broadcasted_iota(jnp.int32, s0.shape, 0) - qi * tq
    s0 = jnp.where(key0 <= lax.broadcasted_iota(jnp.int32, s0.shape, 1), s0, NEG_BIG)
    sig_sc[...] = jnp.max(s0, axis=0, keepdims=True)
    viol_sc[...] = jnp.zeros(viol_sc.shape, F32)
    acc_sc[...] = jnp.zeros(acc_sc.shape, F32)

    def octet(io, carry):
        for u in range(MLA_UNROLL):
            lagged_chunk(MLA_UNROLL * io + u, None)
        return carry

    per_iter = MLA_UNROLL // 2
    n_oct = qi // per_iter
    lax.fori_loop(0, n_oct, octet, 0)
    for rem in range(per_iter):
        @pl.when(qi % per_iter == rem)
        def _(rem=rem):
            for u in range(2 * rem):
                lagged_chunk(MLA_UNROLL * n_oct + u, None)
            lagged_chunk(2 * qi, 0)
            lagged_chunk(2 * qi + 1, tk)
    finish()

    @pl.when(jnp.max(viol_sc[...]) > LAG_LIMIT)
    def _():
        sig_sc[...] = jnp.full(sig_sc.shape, NEG_BIG, F32)
        acc_sc[...] = jnp.zeros(acc_sc.shape, F32)

        def exact_chunk(kc, carry):
            s = causal(qk(kc), kc * tk - qi * tq)
            m_old = sig_sc[...]
            m_new = jnp.maximum(m_old, jnp.max(s, axis=0, keepdims=True))
            p = jnp.exp2(s - m_new).astype(BF16)
            acc_sc[...] = jnp.exp2(m_old - m_new) * acc_sc[...] + jnp.dot(vt_ref[kc], p, preferred_element_type=F32)
            sig_sc[...] = m_new
            return carry

        lax.fori_loop(0, n_chunks, exact_chunk, 0)
        finish()


def _mla_attention(qt, k, vt, tk):
    b, h, s, _ = k.shape
    tq = 2 * tk
    assert vt.shape[-1] == tk and s % tq == 0
    return pl.pallas_call(
        functools.partial(_mla_attn_kernel, tq=tq, tk=tk),
        grid=(b, h, s // tq),
        in_specs=[pl.BlockSpec((None, None, QK_PAD, tq), lambda bb, hh, i: (bb, hh, 0, i)),
                  pl.BlockSpec((None, None, s, QK_PAD), lambda bb, hh, i: (bb, hh, 0, 0)),
                  pl.BlockSpec((None, None, s // tk, VT_ROWS, tk), lambda bb, hh, i: (bb, hh, 0, 0, 0))],
        out_specs=pl.BlockSpec((None, tq, V_HEAD), lambda bb, hh, i: (bb, i, hh)),
        out_shape=jax.ShapeDtypeStruct((b, s, MLA_WIDTH), BF16),
        scratch_shapes=[pltpu.VMEM((1, tq), F32), pltpu.VMEM((1, tq), F32), pltpu.VMEM((VT_ROWS, tq), F32)],
        name="a_mla_attention",
        compiler_params=_params(("arbitrary", "arbitrary", "arbitrary")),
    )(qt, k, vt)


def _a_out_kernel(o_ref, sg_ref, x_ref, w_ref, lnkv_ref, lnb_ref, h_ref, xkv_ref, xb_ref):
    tm = o_ref.shape[0]
    for c in range(tm // OUT_ROW_CHUNK):
        rows = slice(c * OUT_ROW_CHUNK, (c + 1) * OUT_ROW_CHUNK)
        a = (o_ref[rows, :] * sg_ref[rows, :]).astype(BF16)
        h = x_ref[rows, :] + jnp.dot(a, w_ref[...], preferred_element_type=F32)
        h_ref[rows, :] = h
        hn = h * _inv_rms(h, D_MODEL)
        xkv_ref[rows, :] = (hn * lnkv_ref[...]).astype(BF16)
        xb_ref[rows, :] = (hn * lnb_ref[...]).astype(BF16)


def _a_out(o2, sg, x2, w, ln_kv, ln_b, tm):
    m = x2.shape[0]
    row = pl.BlockSpec((tm, D_MODEL), lambda i: (i, 0))
    full = lambda a: pl.BlockSpec(a.shape, lambda i: (0, 0))
    return pl.pallas_call(
        _a_out_kernel,
        grid=(m // tm,),
        in_specs=[row, row, row, pl.BlockSpec(w.shape, lambda i: (0, 0), pipeline_mode=pl.Buffered(1)),
                  full(ln_kv), full(ln_b)],
        out_specs=[row, row, row],
        out_shape=[jax.ShapeDtypeStruct((m, D_MODEL), F32), jax.ShapeDtypeStruct((m, D_MODEL), BF16),
                   jax.ShapeDtypeStruct((m, D_MODEL), BF16)],
        name="a_out_proj",
        compiler_params=_params(("arbitrary",)),
    )(o2, sg, x2, w, ln_kv, ln_b)


def _branch_proj_kernel(x_ref, w_ref, g_ref, o_ref, y_sc, y2_sc, *, mode, dilation, scale):
    tm, tn = o_ref.shape
    rows = PROJ_ROW_CHUNK // dilation
    for c in range(tm // PROJ_ROW_CHUNK):
        y = jnp.dot(x_ref[c * PROJ_ROW_CHUNK:(c + 1) * PROJ_ROW_CHUNK, :], w_ref[...], preferred_element_type=F32)
        for hc in range(tn // DIL_HEAD_DIM):
            sl = slice(hc * DIL_HEAD_DIM, (hc + 1) * DIL_HEAD_DIM)
            yh = y[:, sl]
            if mode == "norm":
                yh = (yh * _inv_rms(yh, DIL_HEAD_DIM)) * g_ref[:, sl]
                if scale != 1.0:
                    yh = yh * scale
            elif mode == "silu":
                yh = yh * jax.nn.sigmoid(yh)
            if dilation == 1:
                o_ref[c * PROJ_ROW_CHUNK:(c + 1) * PROJ_ROW_CHUNK, sl] = yh.astype(BF16)
            else:
                base = c * PROJ_ROW_CHUNK
                y_sc[hc, base:base + PROJ_ROW_CHUNK, :] = yh
                if dilation == STRIDE_STEP * STRIDE_STEP:
                    quarter = PROJ_ROW_CHUNK // STRIDE_STEP
                    for a in range(STRIDE_STEP):
                        y2_sc[hc, base + a * quarter:base + (a + 1) * quarter, :] = (
                            y_sc[hc, pl.ds(base + a, quarter, stride=STRIDE_STEP), :])
                    for r in range(dilation):
                        a, b = r % STRIDE_STEP, r // STRIDE_STEP
                        dst = r * (tm // dilation) + c * rows
                        o_ref[dst:dst + rows, sl] = (
                            y2_sc[hc, pl.ds(base + a * quarter + b, rows, stride=STRIDE_STEP), :].astype(BF16))
                else:
                    for r in range(dilation):
                        dst = r * (tm // dilation) + c * rows
                        o_ref[dst:dst + rows, sl] = y_sc[hc, pl.ds(base + r, rows, stride=dilation), :].astype(BF16)


def _branch_proj(xn, w, g_cols, col0, mode, dilation, scale, name):
    m, kdim = xn.shape
    tm, tn = DIL_TILE, PROJ_COL_TILE
    blk0 = col0 // tn
    return pl.pallas_call(
        functools.partial(_branch_proj_kernel, mode=mode, dilation=dilation, scale=scale),
        grid=(m // tm, DIL_WIDTH // tn),
        in_specs=[pl.BlockSpec((tm, kdim), lambda i, j: (i, 0)),
                  pl.BlockSpec((kdim, tn), lambda i, j: (0, blk0 + j)),
                  pl.BlockSpec((1, tn), lambda i, j: (0, blk0 + j))],
        out_specs=pl.BlockSpec((tm, tn), lambda i, j: (i, j)),
        out_shape=jax.ShapeDtypeStruct((m, DIL_WIDTH), BF16),
        scratch_shapes=[pltpu.VMEM((tn // DIL_HEAD_DIM, tm, DIL_HEAD_DIM), F32)] * 2,
        name=name,
        compiler_params=_params(("arbitrary", "arbitrary")),
    )(xn, w, g_cols)


def _dilated_attn_kernel(*refs):
    qs, kps, kcs, vps, vcs = (refs[i * N_GROUPS:(i + 1) * N_GROUPS] for i in range(5))
    sg_ref, out_ref, o_sc, lse_sc = refs[5 * N_GROUPS:]
    t = pl.program_id(1)
    c = pl.program_id(2)
    row = lax.broadcasted_iota(jnp.int32, (BAND_BLK, 2 * BAND_BLK), 0)
    col = lax.broadcasted_iota(jnp.int32, (BAND_BLK, 2 * BAND_BLK), 1)
    rel = row + BAND_BLK - col
    band = (rel >= 0) & (rel <= LOOKBACK)
    first_band = band & (col >= jnp.where(t > 0, 0, BAND_BLK))
    relf = rel.astype(F32)
    assert DIL_GROUPS[0][1] == 1
    for g in (1, 2, 0):
        dil = DIL_GROUPS[g][1]
        nblk = DIL_TILE // dil // BAND_BLK
        q_ref, kp_ref, kc_ref, vp_ref, vc_ref = qs[g], kps[g], kcs[g], vps[g], vcs[g]
        for hh in range(DIL_HEAD_CHUNK):
            hs = slice(hh * DIL_HEAD_DIM, (hh + 1) * DIL_HEAD_DIM)
            head = (c * DIL_HEAD_CHUNK + hh + 1).astype(F32)
            slope = jnp.exp2(jnp.full((1, 2 * BAND_BLK), -ALIBI_MAX / DIL_HEADS, F32) * head)
            bias = (slope * (-float(dil) * LOG2E)) * relf
            bias_band = jnp.where(band, bias, NEG_BIG)
            bias_first = jnp.where(first_band, bias, NEG_BIG)
            for r in range(dil):
                for a in range(nblk):
                    q = q_ref[r, a, :, hs]
                    k_lo = kp_ref[r, :, hs] if a == 0 else kc_ref[r, a - 1, :, hs]
                    v_lo = vp_ref[r, :, hs] if a == 0 else vc_ref[r, a - 1, :, hs]
                    k = jnp.concatenate([k_lo, kc_ref[r, a, :, hs]], axis=0)
                    v = jnp.concatenate([v_lo, vc_ref[r, a, :, hs]], axis=0)
                    s = lax.dot_general(q, k, (((1,), (1,)), ((), ())), preferred_element_type=F32)
                    s = s + (bias_first if a == 0 else bias_band)
                    m = jnp.max(s, axis=-1, keepdims=True)
                    p = jnp.exp2(s - m)
                    den = jnp.sum(p, axis=-1, keepdims=True)
                    o = jnp.dot(p.astype(BF16), v, preferred_element_type=F32) / den
                    lse = jnp.broadcast_to(m + jnp.log2(den), (BAND_BLK, V7X_LANES))
                    if dil > 1:
                        tok = pl.ds(a * BAND_BLK * dil + r, BAND_BLK, stride=dil)
                        o_sc[g - 1, hh, tok, :] = o
                        lse_sc[g - 1, hh, tok, :] = lse
                    else:
                        tok = slice(a * BAND_BLK, (a + 1) * BAND_BLK)
                        l1, l2 = lse_sc[0, hh, tok, :], lse_sc[1, hh, tok, :]
                        mx = jnp.maximum(jnp.maximum(lse, l1), l2)
                        e0, e1, e2 = jnp.exp2(lse - mx), jnp.exp2(l1 - mx), jnp.exp2(l2 - mx)
                        merged = (e0 * o + e1 * o_sc[0, hh, tok, :] + e2 * o_sc[1, hh, tok, :]) / (e0 + e1 + e2)
                        out_ref[tok, hs] = (merged * sg_ref[tok, hs].astype(F32)).astype(BF16)


def _dilated_attention(qs, ks, vs, gate, batch, seq):
    nt = seq // DIL_TILE
    ins, specs = [], []

    def add(arrs, prev):
        for arr, (_, dil) in zip(arrs, DIL_GROUPS):
            nblk = DIL_TILE // dil // BAND_BLK
            ins.append(arr.reshape(batch, nt, dil, nblk, BAND_BLK, DIL_WIDTH))
            if prev:
                specs.append(pl.BlockSpec((None, None, dil, None, BAND_BLK, DIL_COLS),
                                          lambda b, t, c, nblk=nblk: (b, jnp.maximum(t - 1, 0), 0, nblk - 1, 0, c)))
            else:
                specs.append(pl.BlockSpec((None, None, dil, nblk, BAND_BLK, DIL_COLS),
                                          lambda b, t, c: (b, t, 0, 0, 0, c)))

    add(qs, False)
    add(ks, True)
    add(ks, False)
    add(vs, True)
    add(vs, False)
    ins.append(gate.reshape(batch, nt, DIL_TILE, DIL_WIDTH))
    specs.append(pl.BlockSpec((None, None, DIL_TILE, DIL_COLS), lambda b, t, c: (b, t, 0, c)))
    out = pl.pallas_call(
        _dilated_attn_kernel,
        grid=(batch, nt, DIL_HEADS // DIL_HEAD_CHUNK),
        in_specs=specs,
        out_specs=pl.BlockSpec((None, None, DIL_TILE, DIL_COLS), lambda b, t, c: (b, t, 0, c)),
        out_shape=jax.ShapeDtypeStruct((batch, nt, DIL_TILE, DIL_WIDTH), BF16),
        scratch_shapes=[pltpu.VMEM((N_GROUPS - 1, DIL_HEAD_CHUNK, DIL_TILE, DIL_HEAD_DIM), F32),
                        pltpu.VMEM((N_GROUPS - 1, DIL_HEAD_CHUNK, DIL_TILE, V7X_LANES), F32)],
        name="b_dilated_attention",
        compiler_params=_params(("arbitrary", "arbitrary", "arbitrary")),
    )(*ins)
    return out.reshape(batch * seq, DIL_WIDTH)


def _b_out_kernel(a_ref, h_ref, w_ref, out_ref):
    for c in range(out_ref.shape[0] // OUT_ROW_CHUNK):
        rows = slice(c * OUT_ROW_CHUNK, (c + 1) * OUT_ROW_CHUNK)
        out_ref[rows, :] = h_ref[rows, :] + jnp.dot(a_ref[rows, :], w_ref[...], preferred_element_type=F32)


def _b_out(a, h1, w, tm):
    m = h1.shape[0]
    row = pl.BlockSpec((tm, D_MODEL), lambda i: (i, 0))
    return pl.pallas_call(
        _b_out_kernel,
        grid=(m // tm,),
        in_specs=[row, row, pl.BlockSpec(w.shape, lambda i: (0, 0), pipeline_mode=pl.Buffered(1))],
        out_specs=row,
        out_shape=jax.ShapeDtypeStruct((m, D_MODEL), F32),
        name="b_out_proj",
        compiler_params=_params(("arbitrary",)),
    )(a, h1, w)


def _tile(n, want):
    t = min(n, want)
    assert n % t == 0, (n, t)
    return t


def kernel(x, a_ln, a_w_in, a_ln_q, a_w_q_up, a_ln_kv, a_w_kv_up, a_q_norm, a_k_norm, a_w_o,
           kv_ln, kv_w, kv_k_norm, b_ln, b_w_in, b_q_norm, b_w_o):
    batch, seq, d = x.shape
    assert d == D_MODEL and a_ln.shape[0] == 1 and b_ln.shape[0] == 1
    assert seq % DIL_TILE == 0
    m = batch * seq
    x2 = x.reshape(m, d)

    lat = Q_LORA + KV_LORA
    w_in = a_w_in[0]
    w_in_pad = jnp.concatenate(
        [w_in[:, :lat], w_in[:, lat + QK_ROPE:], w_in[:, lat:lat + QK_ROPE],
         jnp.zeros((d, V7X_LANES - QK_ROPE), F32)], axis=1).astype(BF16)
    assert w_in_pad.shape == (d, A_IN_PAD)
    w_q_t = a_w_q_up[0].T.astype(BF16)
    w_kv = a_w_kv_up[0].reshape(KV_LORA, MLA_HEADS, QK_NOPE + V_HEAD)
    w_k = w_kv[:, :, :QK_NOPE].reshape(KV_LORA, MLA_HEADS * QK_NOPE).astype(BF16)
    w_vt = w_kv[:, :, QK_NOPE:].reshape(KV_LORA, MLA_HEADS * V_HEAD).T.astype(BF16)
    gq_rows = jnp.broadcast_to(a_q_norm[0][:, None], (QK_HEAD, V7X_LANES))
    gk_pad = jnp.pad(a_k_norm[0], (0, QK_PAD - QK_HEAD))[None, :]
    w_o_a = a_w_o[0].astype(BF16)
    w_kvs = kv_w.astype(BF16)
    w_b_in = b_w_in[0].astype(BF16)
    w_o_b = b_w_o[0].astype(BF16)
    k_cols = N_GROUPS * DIL_WIDTH
    gk_cols = jnp.concatenate([jnp.tile(kv_k_norm, (1, DIL_HEADS)).reshape(1, k_cols),
                               jnp.ones((1, k_cols), F32)], axis=1)
    gq_cols = jnp.concatenate([jnp.tile(b_q_norm[0], (1, DIL_HEADS)).reshape(1, k_cols),
                               jnp.ones((1, DIL_WIDTH), F32)], axis=1)

    tabs, tabs_t = _rope_tables(seq)
    cqt, ckv, sg_a, kpe = _a_in(x2, a_ln, w_in_pad, a_ln_q, a_ln_kv, _tile(m, A_IN_ROW_TILE))
    blk = _tile(seq, MLA_KEY_CHUNK)
    q = _q_up(cqt, w_q_t, gq_rows, tabs_t, batch, seq, blk)
    k, vt = _kv_up(ckv.reshape(batch, seq, KV_LORA), kpe.reshape(batch, seq, V7X_LANES), w_k, w_vt, gk_pad, tabs, blk)
    o_a = _mla_attention(q, k, vt, blk)
    h1, xn_kv, xn_b = _a_out(o_a.reshape(m, MLA_WIDTH), sg_a, x2, w_o_a, kv_ln[None, :], b_ln,
                             _tile(m, 2 * OUT_ROW_CHUNK))

    q_scale = DIL_HEAD_DIM ** -0.5 * LOG2E
    qs, ks, vs = [], [], []
    for g, (_, dil) in enumerate(DIL_GROUPS):
        ks.append(_branch_proj(xn_kv, w_kvs, gk_cols, g * DIL_WIDTH, "norm", dil, 1.0, f"b_k_proj_d{dil}"))
        vs.append(_branch_proj(xn_kv, w_kvs, gk_cols, k_cols + g * DIL_WIDTH, "copy", dil, 1.0, f"b_v_proj_d{dil}"))
        qs.append(_branch_proj(xn_b, w_b_in, gq_cols, g * DIL_WIDTH, "norm", dil, q_scale, f"b_q_proj_d{dil}"))
    sg_b = _branch_proj(xn_b, w_b_in, gq_cols, k_cols, "silu", 1, 1.0, "b_gate_proj")
    a_b = _dilated_attention(qs, ks, vs, sg_b, batch, seq)
    out = _b_out(a_b, h1, w_o_b, _tile(m, 2 * OUT_ROW_CHUNK))
    return out.reshape(batch, seq, d)
```

```python
import functools
import math

import jax
import jax.numpy as jnp
from jax import lax
from jax.experimental import pallas as pl
from jax.experimental.pallas import tpu as pltpu

F32 = jnp.float32
BF16 = jnp.bfloat16

V7X_LANES = 128
V7X_VMEM_BYTES = 64 * 1024 * 1024
VMEM_LIMIT_BYTES = V7X_VMEM_BYTES * 7 // 8

D_MODEL = 2048
MLA_HEADS = 16
Q_LORA = 512
KV_LORA = 512
QK_NOPE = 128
QK_ROPE = 64
QK_HEAD = QK_NOPE + QK_ROPE
V_HEAD = 128
MLA_WIDTH = MLA_HEADS * V_HEAD
ROPE_THETA = 10000.0
DIL_GROUPS = ((128, 1), (512, 4), (2048, 16))
N_GROUPS = 3
DIL_HEADS = 16
DIL_HEAD_DIM = 128
DIL_WIDTH = DIL_HEADS * DIL_HEAD_DIM
LOOKBACK = 128
BAND_BLK = 128
ALIBI_MAX = 8.0
EPS = 1e-6
NEG_BIG = -1e30
LOG2E = math.log2(math.e)

QK_PAD = 2 * V7X_LANES
A_IN_PAD = Q_LORA + KV_LORA + MLA_WIDTH + V7X_LANES

DIL_TILE = BAND_BLK * DIL_GROUPS[-1][1]
DIL_HEAD_CHUNK = 2
DIL_COLS = DIL_HEAD_CHUNK * DIL_HEAD_DIM
V7X_BF16_SUBLANES = 16
A_IN_ROW_TILE = 512
A_IN_ROW_CHUNK = 256
UP_HEAD_GROUP = 4
OUT_ROW_CHUNK = 256
MLA_KEY_CHUNK = 512
MLA_UNROLL = 8
VT_ROWS = V_HEAD + V7X_BF16_SUBLANES
FIRST_SHIFT_KEYS = V7X_BF16_SUBLANES
LAG_LIMIT = 64.0
STRIDE_STEP = 4
PROJ_ROW_CHUNK = 256
PAIR_COL_TILE = 512
PROJ_COL_TILE = 1024


def _params(semantics):
    return pltpu.CompilerParams(dimension_semantics=semantics, vmem_limit_bytes=VMEM_LIMIT_BYTES)


def _inv_rms(x, n):
    return lax.rsqrt(jnp.sum(x * x, axis=-1, keepdims=True) * (1.0 / n) + EPS)


def _rope(rv, cos_t, s1_t, s2_t):
    return rv * cos_t + pltpu.roll(rv, 96, 1) * s1_t + pltpu.roll(rv, 32, 1) * s2_t


def _rope_table_kernel(invt_ref, cos_ref, s1_ref, s2_ref, cost_ref, sint_ref):
    ts = cos_ref.shape[0]
    half = QK_ROPE // 2
    pos = lax.broadcasted_iota(jnp.int32, (half, ts), 1) + pl.program_id(0) * ts
    ang_t = pos.astype(F32) * jnp.tile(invt_ref[...], (1, ts // V7X_LANES))
    c_t = jnp.cos(ang_t)
    s_t = jnp.sin(ang_t)
    cost_ref[...] = c_t
    sint_ref[...] = s_t
    c = c_t.T
    s = s_t.T
    z32 = jnp.zeros((ts, half), F32)
    z64 = jnp.zeros((ts, V7X_LANES - QK_ROPE), F32)
    cos_ref[...] = jnp.concatenate([c, c, z64], axis=1)
    s1_ref[...] = jnp.concatenate([-s, z32, z64], axis=1)
    s2_ref[...] = jnp.concatenate([z32, s, z64], axis=1)


def _rope_tables(seq):
    half = QK_ROPE // 2
    inv = ROPE_THETA ** (-jnp.arange(half, dtype=F32) / half)
    inv_rows = jnp.broadcast_to(inv[:, None], (half, V7X_LANES))
    ts = min(seq, 1024)
    tab = jax.ShapeDtypeStruct((seq, V7X_LANES), F32)
    tab_t = jax.ShapeDtypeStruct((half, seq), F32)
    spec = pl.BlockSpec((ts, V7X_LANES), lambda i: (i, 0))
    spec_t = pl.BlockSpec((half, ts), lambda i: (0, i))
    outs = pl.pallas_call(
        _rope_table_kernel,
        grid=(seq // ts,),
        in_specs=[pl.BlockSpec((half, V7X_LANES), lambda i: (0, 0))],
        out_specs=[spec, spec, spec, spec_t, spec_t],
        out_shape=[tab, tab, tab, tab_t, tab_t],
        name="rope_tables",
        compiler_params=_params(("arbitrary",)),
    )(inv_rows)
    return outs[:3], outs[3:]


def _a_in_kernel(x_ref, ln_ref, w_ref, lnq_ref, lnkv_ref, cq_ref, ckv_ref, sg_ref, kpe_ref):
    g0 = Q_LORA + KV_LORA
    for c in range(x_ref.shape[0] // A_IN_ROW_CHUNK):
        rows = slice(c * A_IN_ROW_CHUNK, (c + 1) * A_IN_ROW_CHUNK)
        x = x_ref[rows, :]
        xn = ((x * _inv_rms(x, D_MODEL)) * ln_ref[...]).astype(BF16)
        u = jnp.dot(xn, w_ref[...], preferred_element_type=F32)
        cq = u[:, :Q_LORA]
        cq_ref[:, rows] = ((cq * _inv_rms(cq, Q_LORA)) * lnq_ref[...]).T.astype(BF16)
        ckv = u[:, Q_LORA:g0]
        ckv_ref[rows, :] = ((ckv * _inv_rms(ckv, KV_LORA)) * lnkv_ref[...]).astype(BF16)
        gate = u[:, g0:g0 + MLA_WIDTH]
        sg_ref[rows, :] = (gate * jax.nn.sigmoid(gate)).astype(BF16)
        kpe_ref[rows, :] = u[:, g0 + MLA_WIDTH:]


def _a_in(x2, ln, w_pad, ln_q, ln_kv, tm):
    m = x2.shape[0]
    row = lambda n: pl.BlockSpec((tm, n), lambda i: (i, 0))
    full = lambda a: pl.BlockSpec(a.shape, lambda i: (0, 0))
    return pl.pallas_call(
        _a_in_kernel,
        grid=(m // tm,),
        in_specs=[row(D_MODEL), full(ln),
                  pl.BlockSpec(w_pad.shape, lambda i: (0, 0), pipeline_mode=pl.Buffered(1)),
                  full(ln_q), full(ln_kv)],
        out_specs=[pl.BlockSpec((Q_LORA, tm), lambda i: (0, i)), row(KV_LORA), row(MLA_WIDTH), row(V7X_LANES)],
        out_shape=[jax.ShapeDtypeStruct((Q_LORA, m), BF16), jax.ShapeDtypeStruct((m, KV_LORA), BF16),
                   jax.ShapeDtypeStruct((m, MLA_WIDTH), BF16), jax.ShapeDtypeStruct((m, V7X_LANES), F32)],
        name="a_in_proj",
        compiler_params=_params(("arbitrary",)),
    )(x2, ln, w_pad, ln_q, ln_kv)


def _q_up_kernel(cqt_ref, wt_ref, g_ref, cos_ref, sin_ref, q_ref, *, scale):
    tm = cqt_ref.shape[1]
    g = jnp.tile(g_ref[...], (1, tm // V7X_LANES))
    cos_t, sin_t = cos_ref[...], sin_ref[...]
    half = QK_ROPE // 2
    zeros = jnp.zeros((QK_PAD - QK_HEAD, tm), BF16)
    for h in range(MLA_HEADS):
        if h % UP_HEAD_GROUP == 0:
            w_rows = slice(h * QK_HEAD, (h + UP_HEAD_GROUP) * QK_HEAD)
            yt = jnp.dot(wt_ref[w_rows, :], cqt_ref[...], preferred_element_type=F32)
        hl = h % UP_HEAD_GROUP
        blk = yt[hl * QK_HEAD:(hl + 1) * QK_HEAD, :]
        r = lax.rsqrt(jnp.sum(blk * blk, axis=0, keepdims=True) * (1.0 / QK_HEAD) + EPS)
        z = (blk * r) * g
        x1, x2 = z[QK_NOPE:QK_NOPE + half, :], z[QK_NOPE + half:, :]
        q_ref[h, :QK_NOPE, :] = (z[:QK_NOPE, :] * scale).astype(BF16)
        q_ref[h, QK_NOPE:QK_NOPE + half, :] = ((x1 * cos_t - x2 * sin_t) * scale).astype(BF16)
        q_ref[h, QK_NOPE + half:QK_HEAD, :] = ((x2 * cos_t + x1 * sin_t) * scale).astype(BF16)
        q_ref[h, QK_HEAD:, :] = zeros


def _kv_up_kernel(ckv_ref, kpe_ref, wk_ref, wvt_ref, g_ref, cos_ref, s1_ref, s2_ref, k_ref, vt_ref):
    g_n = g_ref[:, :QK_NOPE]
    g_r = g_ref[:, QK_NOPE:]
    kpe = kpe_ref[...]
    ss_pe = jnp.sum(kpe * kpe, axis=-1, keepdims=True)
    cos_t, s1_t, s2_t = cos_ref[...], s1_ref[...], s2_ref[...]
    pad_row = lax.broadcasted_iota(jnp.int32, (VT_ROWS - V_HEAD, ckv_ref.shape[0]), 0)
    ones_rows = jnp.where(pad_row == 0, 1.0, 0.0).astype(BF16)
    kr = _rope(kpe * g_r, cos_t, s1_t, s2_t)
    ckv = ckv_ref[...]
    for h in range(MLA_HEADS):
        if h % UP_HEAD_GROUP == 0:
            grp = slice(h * QK_NOPE, (h + UP_HEAD_GROUP) * QK_NOPE)
            yk = jnp.dot(ckv, wk_ref[:, grp], preferred_element_type=F32)
            vt = lax.dot_general(wvt_ref[grp, :], ckv, (((1,), (1,)), ((), ())),
                                 preferred_element_type=F32)
        hl = slice((h % UP_HEAD_GROUP) * QK_NOPE, (h % UP_HEAD_GROUP + 1) * QK_NOPE)
        kn = yk[:, hl]
        r = lax.rsqrt((jnp.sum(kn * kn, axis=-1, keepdims=True) + ss_pe) * (1.0 / QK_HEAD) + EPS)
        k_ref[h, :, :QK_NOPE] = ((kn * r) * g_n).astype(BF16)
        k_ref[h, :, QK_NOPE:] = (kr * r).astype(BF16)
        vt_ref[h, :V_HEAD, :] = vt[hl, :].astype(BF16)
        vt_ref[h, V_HEAD:, :] = ones_rows


def _seq_spec(tm, n):
    return pl.BlockSpec((None, tm, n), lambda b, i: (b, i, 0))


def _tab_spec(tm):
    return pl.BlockSpec((tm, V7X_LANES), lambda b, i: (i, 0))


def _head_spec(tm, n):
    return pl.BlockSpec((None, MLA_HEADS, tm, n), lambda b, i: (b, 0, i, 0))


def _q_up(cqt, wt, g_rows, tabs_t, batch, seq, tm):
    full = lambda a: pl.BlockSpec(a.shape, lambda bb, i: (0, 0))
    scale = QK_HEAD ** -0.5 * LOG2E
    nblk = seq // tm
    tab = pl.BlockSpec((QK_ROPE // 2, tm), lambda bb, i: (0, i))
    return pl.pallas_call(
        functools.partial(_q_up_kernel, scale=scale),
        grid=(batch, nblk),
        in_specs=[pl.BlockSpec((Q_LORA, tm), lambda bb, i: (0, bb * nblk + i)), full(wt), full(g_rows), tab, tab],
        out_specs=pl.BlockSpec((None, MLA_HEADS, QK_PAD, tm), lambda bb, i: (bb, 0, 0, i)),
        out_shape=jax.ShapeDtypeStruct((batch, MLA_HEADS, QK_PAD, seq), BF16),
        name="a_q_up",
        compiler_params=_params(("arbitrary", "arbitrary")),
    )(cqt, wt, g_rows, *tabs_t)


def _kv_up(ckv3, kpe3, wk, wvt, g_pad, tabs, tm):
    b, s, _ = ckv3.shape
    full = lambda a: pl.BlockSpec(a.shape, lambda bb, i: (0, 0))
    return pl.pallas_call(
        _kv_up_kernel,
        grid=(b, s // tm),
        in_specs=[_seq_spec(tm, KV_LORA), _seq_spec(tm, V7X_LANES), full(wk), full(wvt), full(g_pad),
                  _tab_spec(tm), _tab_spec(tm), _tab_spec(tm)],
        out_specs=[_head_spec(tm, QK_PAD),
                   pl.BlockSpec((None, MLA_HEADS, None, VT_ROWS, tm), lambda bb, i: (bb, 0, i, 0, 0))],
        out_shape=[jax.ShapeDtypeStruct((b, MLA_HEADS, s, QK_PAD), BF16),
                   jax.ShapeDtypeStruct((b, MLA_HEADS, s // tm, VT_ROWS, tm), BF16)],
        name="a_kv_up",
        compiler_params=_params(("arbitrary", "arbitrary")),
    )(ckv3, kpe3, wk, wvt, g_pad, *tabs)


def _mla_attn_kernel(q_ref, k_ref, vt_ref, o_ref, sig_sc, viol_sc, acc_sc, *, tq, tk):
    qi = pl.program_id(2)
    qt = q_ref[...]
    n_chunks = 2 * qi + 2

    def qk(kc):
        k = k_ref[pl.ds(pl.multiple_of(kc * tk, tk), tk), :]
        return jnp.dot(k, qt, preferred_element_type=F32)

    def causal(s, key_offset):
        key = lax.broadcasted_iota(jnp.int32, (tk, tq), 0) + key_offset
        qry = lax.broadcasted_iota(jnp.int32, (tk, tq), 1)
        return jnp.where(key <= qry, s, NEG_BIG)

    def finish():
        o_ref[...] = (acc_sc[:V_HEAD, :] / acc_sc[V_HEAD:V_HEAD + 1, :]).T.astype(BF16)

    def lagged_chunk(kc, key_offset):
        col0 = 0 if key_offset is None else key_offset
        cs = slice(col0, tq)
        k = k_ref[pl.ds(pl.multiple_of(kc * tk, tk), tk), :]
        s = jnp.dot(k, qt[:, cs], preferred_element_type=F32)
        if key_offset is not None:
            key = lax.broadcasted_iota(jnp.int32, s.shape, 0)
            qry = lax.broadcasted_iota(jnp.int32, s.shape, 1)
            s = jnp.where(key <= qry, s, NEG_BIG)
        shift = sig_sc[:, cs]
        mx = jnp.max(s, axis=0, keepdims=True)
        p = jnp.exp2(s - shift).astype(BF16)
        new_shift = jnp.maximum(shift, mx)
        pv = jnp.dot(vt_ref[kc], p, preferred_element_type=F32)
        acc_sc[:, cs] = (acc_sc[:, cs] + pv) * jnp.exp2(shift - new_shift)
        sig_sc[:, cs] = new_shift
        viol_sc[:, cs] = jnp.maximum(viol_sc[:, cs], mx - shift)

    s0 = jnp.dot(k_ref[:FIRST_SHIFT_KEYS, :], qt, preferred_element_type=F32)
    key0 = lax.broadcasted_iota(jnp.int32, s0.shape, 0) - qi * tq
    s0 = jnp.where(key0 <= lax.broadcasted_iota(jnp.int32, s0.shape, 1), s0, NEG_BIG)
    sig_sc[...] = jnp.max(s0, axis=0, keepdims=True)
    viol_sc[...] = jnp.zeros(viol_sc.shape, F32)
    acc_sc[...] = jnp.zeros(acc_sc.shape, F32)

    def octet(io, carry):
        for u in range(MLA_UNROLL):
            lagged_chunk(MLA_UNROLL * io + u, None)
        return carry

    per_iter = MLA_UNROLL // 2
    n_oct = qi // per_iter
    lax.fori_loop(0, n_oct, octet, 0)
    for rem in range(per_iter):
        @pl.when(qi % per_iter == rem)
        def _(rem=rem):
            for u in range(2 * rem):
                lagged_chunk(MLA_UNROLL * n_oct + u, None)
            lagged_chunk(2 * qi, 0)
            lagged_chunk(2 * qi + 1, tk)
    finish()

    @pl.when(jnp.max(viol_sc[...]) > LAG_LIMIT)
    def _():
        sig_sc[...] = jnp.full(sig_sc.shape, NEG_BIG, F32)
        acc_sc[...] = jnp.zeros(acc_sc.shape, F32)

        def exact_chunk(kc, carry):
            s = causal(qk(kc), kc * tk - qi * tq)
            m_old = sig_sc[...]
            m_new = jnp.maximum(m_old, jnp.max(s, axis=0, keepdims=True))
            p = jnp.exp2(s - m_new).astype(BF16)
            acc_sc[...] = jnp.exp2(m_old - m_new) * acc_sc[...] + jnp.dot(vt_ref[kc], p, preferred_element_type=F32)
            sig_sc[...] = m_new
            return carry

        lax.fori_loop(0, n_chunks, exact_chunk, 0)
        finish()


def _mla_attention(qt, k, vt, tk):
    b, h, s, _ = k.shape
    tq = 2 * tk
    assert vt.shape[-1] == tk and s % tq == 0
    return pl.pallas_call(
        functools.partial(_mla_attn_kernel, tq=tq, tk=tk),
        grid=(b, h, s // tq),
        in_specs=[pl.BlockSpec((None, None, QK_PAD, tq), lambda bb, hh, i: (bb, hh, 0, i)),
                  pl.BlockSpec((None, None, s, QK_PAD), lambda bb, hh, i: (bb, hh, 0, 0)),
                  pl.BlockSpec((None, None, s // tk, VT_ROWS, tk), lambda bb, hh, i: (bb, hh, 0, 0, 0))],
        out_specs=pl.BlockSpec((None, tq, V_HEAD), lambda bb, hh, i: (bb, i, hh)),
        out_shape=jax.ShapeDtypeStruct((b, s, MLA_WIDTH), BF16),
        scratch_shapes=[pltpu.VMEM((1, tq), F32), pltpu.VMEM((1, tq), F32), pltpu.VMEM((VT_ROWS, tq), F32)],
        name="a_mla_attention",
        compiler_params=_params(("arbitrary", "arbitrary", "arbitrary")),
    )(qt, k, vt)


def _a_out_kernel(o_ref, sg_ref, x_ref, w_ref, lnkv_ref, lnb_ref, h_ref, xkv_ref, xb_ref):
    tm = o_ref.shape[0]
    for c in range(tm // OUT_ROW_CHUNK):
        rows = slice(c * OUT_ROW_CHUNK, (c + 1) * OUT_ROW_CHUNK)
        a = (o_ref[rows, :] * sg_ref[rows, :]).astype(BF16)
        h = x_ref[rows, :] + jnp.dot(a, w_ref[...], preferred_element_type=F32)
        h_ref[rows, :] = h
        hn = h * _inv_rms(h, D_MODEL)
        xkv_ref[rows, :] = (hn * lnkv_ref[...]).astype(BF16)
        xb_ref[rows, :] = (hn * lnb_ref[...]).astype(BF16)


def _a_out(o2, sg, x2, w, ln_kv, ln_b, tm):
    m = x2.shape[0]
    row = pl.BlockSpec((tm, D_MODEL), lambda i: (i, 0))
    full = lambda a: pl.BlockSpec(a.shape, lambda i: (0, 0))
    return pl.pallas_call(
        _a_out_kernel,
        grid=(m // tm,),
        in_specs=[row, row, row, pl.BlockSpec(w.shape, lambda i: (0, 0), pipeline_mode=pl.Buffered(1)),
                  full(ln_kv), full(ln_b)],
        out_specs=[row, row, row],
        out_shape=[jax.ShapeDtypeStruct((m, D_MODEL), F32), jax.ShapeDtypeStruct((m, D_MODEL), BF16),
                   jax.ShapeDtypeStruct((m, D_MODEL), BF16)],
        name="a_out_proj",
        compiler_params=_params(("arbitrary",)),
    )(o2, sg, x2, w, ln_kv, ln_b)


def _branch_proj_kernel(x_ref, w_ref, g_ref, o_ref, y_sc, y2_sc, *, mode, dilation, scale):
    tm, tn = o_ref.shape
    rows = PROJ_ROW_CHUNK // dilation
    for c in range(tm // PROJ_ROW_CHUNK):
        y = jnp.dot(x_ref[c * PROJ_ROW_CHUNK:(c + 1) * PROJ_ROW_CHUNK, :], w_ref[...], preferred_element_type=F32)
        for hc in range(tn // DIL_HEAD_DIM):
            sl = slice(hc * DIL_HEAD_DIM, (hc + 1) * DIL_HEAD_DIM)
            yh = y[:, sl]
            if mode == "norm":
                yh = (yh * _inv_rms(yh, DIL_HEAD_DIM)) * g_ref[:, sl]
                if scale != 1.0:
                    yh = yh * scale
            elif mode == "silu":
                yh = yh * jax.nn.sigmoid(yh)
            if dilation == 1:
                o_ref[c * PROJ_ROW_CHUNK:(c + 1) * PROJ_ROW_CHUNK, sl] = yh.astype(BF16)
            else:
                base = c * PROJ_ROW_CHUNK
                y_sc[hc, base:base + PROJ_ROW_CHUNK, :] = yh
                if dilation == STRIDE_STEP * STRIDE_STEP:
                    quarter = PROJ_ROW_CHUNK // STRIDE_STEP
                    for a in range(STRIDE_STEP):
                        y2_sc[hc, base + a * quarter:base + (a + 1) * quarter, :] = (
                            y_sc[hc, pl.ds(base + a, quarter, stride=STRIDE_STEP), :])
                    for r in range(dilation):
                        a, b = r % STRIDE_STEP, r // STRIDE_STEP
                        dst = r * (tm // dilation) + c * rows
                        o_ref[dst:dst + rows, sl] = (
                            y2_sc[hc, pl.ds(base + a * quarter + b, rows, stride=STRIDE_STEP), :].astype(BF16))
                else:
                    for r in range(dilation):
                        dst = r * (tm // dilation) + c * rows
                        o_ref[dst:dst + rows, sl] = y_sc[hc, pl.ds(base + r, rows, stride=dilation), :].astype(BF16)


def _branch_proj(xn, w, g_cols, col0, mode, dilation, scale, name):
    m, kdim = xn.shape
    tm, tn = DIL_TILE, PROJ_COL_TILE
    blk0 = col0 // tn
    return pl.pallas_call(
        functools.partial(_branch_proj_kernel, mode=mode, dilation=dilation, scale=scale),
        grid=(m // tm, DIL_WIDTH // tn),
        in_specs=[pl.BlockSpec((tm, kdim), lambda i, j: (i, 0)),
                  pl.BlockSpec((kdim, tn), lambda i, j: (0, blk0 + j)),
                  pl.BlockSpec((1, tn), lambda i, j: (0, blk0 + j))],
        out_specs=pl.BlockSpec((tm, tn), lambda i, j: (i, j)),
        out_shape=jax.ShapeDtypeStruct((m, DIL_WIDTH), BF16),
        scratch_shapes=[pltpu.VMEM((tn // DIL_HEAD_DIM, tm, DIL_HEAD_DIM), F32)] * 2,
        name=name,
        compiler_params=_params(("arbitrary", "arbitrary")),
    )(xn, w, g_cols)


def _kv_pair_proj_kernel(x_ref, wk_ref, wv_ref, g_ref, ok_ref, ov_ref, yk_sc, yk2_sc, yv_sc, yv2_sc, *, dilation):
    _branch_proj_kernel(x_ref, wk_ref, g_ref, ok_ref, yk_sc, yk2_sc, mode="norm", dilation=dilation, scale=1.0)
    _branch_proj_kernel(x_ref, wv_ref, g_ref, ov_ref, yv_sc, yv2_sc, mode="copy", dilation=dilation, scale=1.0)


def _kv_pair_proj(xn, w, g_cols, group, dilation):
    m, kdim = xn.shape
    tm, tn = DIL_TILE, PAIR_COL_TILE
    kb = group * DIL_WIDTH // tn
    vb = (N_GROUPS + group) * DIL_WIDTH // tn
    out = jax.ShapeDtypeStruct((m, DIL_WIDTH), BF16)
    slab = pltpu.VMEM((tn // DIL_HEAD_DIM, tm, DIL_HEAD_DIM), F32)
    return pl.pallas_call(
        functools.partial(_kv_pair_proj_kernel, dilation=dilation),
        grid=(m // tm, DIL_WIDTH // tn),
        in_specs=[pl.BlockSpec((tm, kdim), lambda i, j: (i, 0)),
                  pl.BlockSpec((kdim, tn), lambda i, j: (0, kb + j)),
                  pl.BlockSpec((kdim, tn), lambda i, j: (0, vb + j)),
                  pl.BlockSpec((1, tn), lambda i, j: (0, kb + j))],
        out_specs=[pl.BlockSpec((tm, tn), lambda i, j: (i, j))] * 2,
        out_shape=[out, out],
        scratch_shapes=[slab] * 4,
        name=f"b_kv_proj_d{dilation}",
        compiler_params=_params(("arbitrary", "arbitrary")),
    )(xn, w, w, g_cols)


def _dilated_attn_kernel(*refs):
    qs, kps, kcs, vps, vcs = (refs[i * N_GROUPS:(i + 1) * N_GROUPS] for i in range(5))
    sg_ref, out_ref, o_sc, lse_sc = refs[5 * N_GROUPS:]
    t = pl.program_id(1)
    c = pl.program_id(2)
    row = lax.broadcasted_iota(jnp.int32, (BAND_BLK, 2 * BAND_BLK), 0)
    col = lax.broadcasted_iota(jnp.int32, (BAND_BLK, 2 * BAND_BLK), 1)
    rel = row + BAND_BLK - col
    band = (rel >= 0) & (rel <= LOOKBACK)
    first_band = band & (col >= jnp.where(t > 0, 0, BAND_BLK))
    relf = rel.astype(F32)
    assert DIL_GROUPS[0][1] == 1
    for g in (1, 2, 0):
        dil = DIL_GROUPS[g][1]
        nblk = DIL_TILE // dil // BAND_BLK
        q_ref, kp_ref, kc_ref, vp_ref, vc_ref = qs[g], kps[g], kcs[g], vps[g], vcs[g]
        for hh in range(DIL_HEAD_CHUNK):
            hs = slice(hh * DIL_HEAD_DIM, (hh + 1) * DIL_HEAD_DIM)
            head = (c * DIL_HEAD_CHUNK + hh + 1).astype(F32)
            slope = jnp.exp2(jnp.full((1, 2 * BAND_BLK), -ALIBI_MAX / DIL_HEADS, F32) * head)
            bias = (slope * (-float(dil) * LOG2E)) * relf
            bias_band = jnp.where(band, bias, NEG_BIG)
            bias_first = jnp.where(first_band, bias, NEG_BIG)
            for r in range(dil):
                for a in range(nblk):
                    q = q_ref[r, a, :, hs]
                    k_lo = kp_ref[r, :, hs] if a == 0 else kc_ref[r, a - 1, :, hs]
                    v_lo = vp_ref[r, :, hs] if a == 0 else vc_ref[r, a - 1, :, hs]
                    k = jnp.concatenate([k_lo, kc_ref[r, a, :, hs]], axis=0)
                    v = jnp.concatenate([v_lo, vc_ref[r, a, :, hs]], axis=0)
                    s = lax.dot_general(q, k, (((1,), (1,)), ((), ())), preferred_element_type=F32)
                    s = s + (bias_first if a == 0 else bias_band)
                    m = jnp.max(s, axis=-1, keepdims=True)
                    p = jnp.exp2(s - m)
                    den = jnp.sum(p, axis=-1, keepdims=True)
                    o = jnp.dot(p.astype(BF16), v, preferred_element_type=F32) / den
                    lse = jnp.broadcast_to(m + jnp.log2(den), (BAND_BLK, V7X_LANES))
                    if dil > 1:
                        tok = pl.ds(a * BAND_BLK * dil + r, BAND_BLK, stride=dil)
                        o_sc[g - 1, hh, tok, :] = o
                        lse_sc[g - 1, hh, tok, :] = lse
                    else:
                        tok = slice(a * BAND_BLK, (a + 1) * BAND_BLK)
                        l1, l2 = lse_sc[0, hh, tok, :], lse_sc[1, hh, tok, :]
                        mx = jnp.maximum(jnp.maximum(lse, l1), l2)
                        e0, e1, e2 = jnp.exp2(lse - mx), jnp.exp2(l1 - mx), jnp.exp2(l2 - mx)
                        merged = (e0 * o + e1 * o_sc[0, hh, tok, :] + e2 * o_sc[1, hh, tok, :]) / (e0 + e1 + e2)
                        out_ref[tok, hs] = (merged * sg_ref[tok, hs].astype(F32)).astype(BF16)


def _dilated_attention(qs, ks, vs, gate, batch, seq):
    nt = seq // DIL_TILE
    ins, specs = [], []

    def add(arrs, prev):
        for arr, (_, dil) in zip(arrs, DIL_GROUPS):
            nblk = DIL_TILE // dil // BAND_BLK
            ins.append(arr.reshape(batch, nt, dil, nblk, BAND_BLK, DIL_WIDTH))
            if prev:
                specs.append(pl.BlockSpec((None, None, dil, None, BAND_BLK, DIL_COLS),
                                          lambda b, t, c, nblk=nblk: (b, jnp.maximum(t - 1, 0), 0, nblk - 1, 0, c)))
            else:
                specs.append(pl.BlockSpec((None, None, dil, nblk, BAND_BLK, DIL_COLS),
                                          lambda b, t, c: (b, t, 0, 0, 0, c)))

    add(qs, False)
    add(ks, True)
    add(ks, False)
    add(vs, True)
    add(vs, False)
    ins.append(gate.reshape(batch, nt, DIL_TILE, DIL_WIDTH))
    specs.append(pl.BlockSpec((None, None, DIL_TILE, DIL_COLS), lambda b, t, c: (b, t, 0, c)))
    out = pl.pallas_call(
        _dilated_attn_kernel,
        grid=(batch, nt, DIL_HEADS // DIL_HEAD_CHUNK),
        in_specs=specs,
        out_specs=pl.BlockSpec((None, None, DIL_TILE, DIL_COLS), lambda b, t, c: (b, t, 0, c)),
        out_shape=jax.ShapeDtypeStruct((batch, nt, DIL_TILE, DIL_WIDTH), BF16),
        scratch_shapes=[pltpu.VMEM((N_GROUPS - 1, DIL_HEAD_CHUNK, DIL_TILE, DIL_HEAD_DIM), F32),
                        pltpu.VMEM((N_GROUPS - 1, DIL_HEAD_CHUNK, DIL_TILE, V7X_LANES), F32)],
        name="b_dilated_attention",
        compiler_params=_params(("arbitrary", "arbitrary", "arbitrary")),
    )(*ins)
    return out.reshape(batch * seq, DIL_WIDTH)


def _b_out_kernel(a_ref, h_ref, w_ref, out_ref):
    for c in range(out_ref.shape[0] // OUT_ROW_CHUNK):
        rows = slice(c * OUT_ROW_CHUNK, (c + 1) * OUT_ROW_CHUNK)
        out_ref[rows, :] = h_ref[rows, :] + jnp.dot(a_ref[rows, :], w_ref[...], preferred_element_type=F32)


def _b_out(a, h1, w, tm):
    m = h1.shape[0]
    row = pl.BlockSpec((tm, D_MODEL), lambda i: (i, 0))
    return pl.pallas_call(
        _b_out_kernel,
        grid=(m // tm,),
        in_specs=[row, row, pl.BlockSpec(w.shape, lambda i: (0, 0), pipeline_mode=pl.Buffered(1))],
        out_specs=row,
        out_shape=jax.ShapeDtypeStruct((m, D_MODEL), F32),
        name="b_out_proj",
        compiler_params=_params(("arbitrary",)),
    )(a, h1, w)


def _tile(n, want):
    t = min(n, want)
    assert n % t == 0, (n, t)
    return t


def kernel(x, a_ln, a_w_in, a_ln_q, a_w_q_up, a_ln_kv, a_w_kv_up, a_q_norm, a_k_norm, a_w_o,
           kv_ln, kv_w, kv_k_norm, b_ln, b_w_in, b_q_norm, b_w_o):
    batch, seq, d = x.shape
    assert d == D_MODEL and a_ln.shape[0] == 1 and b_ln.shape[0] == 1
    assert seq % DIL_TILE == 0
    m = batch * seq
    x2 = x.reshape(m, d)

    lat = Q_LORA + KV_LORA
    w_in = a_w_in[0]
    w_in_pad = jnp.concatenate(
        [w_in[:, :lat], w_in[:, lat + QK_ROPE:], w_in[:, lat:lat + QK_ROPE],
         jnp.zeros((d, V7X_LANES - QK_ROPE), F32)], axis=1).astype(BF16)
    assert w_in_pad.shape == (d, A_IN_PAD)
    w_q_t = a_w_q_up[0].T.astype(BF16)
    w_kv = a_w_kv_up[0].reshape(KV_LORA, MLA_HEADS, QK_NOPE + V_HEAD)
    w_k = w_kv[:, :, :QK_NOPE].reshape(KV_LORA, MLA_HEADS * QK_NOPE).astype(BF16)
    w_vt = w_kv[:, :, QK_NOPE:].reshape(KV_LORA, MLA_HEADS * V_HEAD).T.astype(BF16)
    gq_rows = jnp.broadcast_to(a_q_norm[0][:, None], (QK_HEAD, V7X_LANES))
    gk_pad = jnp.pad(a_k_norm[0], (0, QK_PAD - QK_HEAD))[None, :]
    w_o_a = a_w_o[0].astype(BF16)
    w_kvs = kv_w.astype(BF16)
    w_b_in = b_w_in[0].astype(BF16)
    w_o_b = b_w_o[0].astype(BF16)
    k_cols = N_GROUPS * DIL_WIDTH
    gk_cols = jnp.concatenate([jnp.tile(kv_k_norm, (1, DIL_HEADS)).reshape(1, k_cols),
                               jnp.ones((1, k_cols), F32)], axis=1)
    gq_cols = jnp.concatenate([jnp.tile(b_q_norm[0], (1, DIL_HEADS)).reshape(1, k_cols),
                               jnp.ones((1, DIL_WIDTH), F32)], axis=1)

    tabs, tabs_t = _rope_tables(seq)
    cqt, ckv, sg_a, kpe = _a_in(x2, a_ln, w_in_pad, a_ln_q, a_ln_kv, _tile(m, A_IN_ROW_TILE))
    blk = _tile(seq, MLA_KEY_CHUNK)
    q = _q_up(cqt, w_q_t, gq_rows, tabs_t, batch, seq, blk)
    k, vt = _kv_up(ckv.reshape(batch, seq, KV_LORA), kpe.reshape(batch, seq, V7X_LANES), w_k, w_vt, gk_pad, tabs, blk)
    o_a = _mla_attention(q, k, vt, blk)
    h1, xn_kv, xn_b = _a_out(o_a.reshape(m, MLA_WIDTH), sg_a, x2, w_o_a, kv_ln[None, :], b_ln,
                             _tile(m, 2 * OUT_ROW_CHUNK))

    q_scale = DIL_HEAD_DIM ** -0.5 * LOG2E
    qs, ks, vs = [], [], []
    for g, (_, dil) in enumerate(DIL_GROUPS):
        k_g, v_g = _kv_pair_proj(xn_kv, w_kvs, gk_cols, g, dil)
        ks.append(k_g)
        vs.append(v_g)
        qs.append(_branch_proj(xn_b, w_b_in, gq_cols, g * DIL_WIDTH, "norm", dil, q_scale, f"b_q_proj_d{dil}"))
    sg_b = _branch_proj(xn_b, w_b_in, gq_cols, k_cols, "silu", 1, 1.0, "b_gate_proj")
    a_b = _dilated_attention(qs, ks, vs, sg_b, batch, seq)
    out = _b_out(a_b, h1, w_o_b, _tile(m, 2 * OUT_ROW_CHUNK))
    return out.reshape(batch, seq, d)
```
